```python
import jax, jax.numpy as jnp
from jax import lax
import numpy as np

D_MODEL = 1024
BATCH = 16
SEQ = 4096
DEPTH = 4

HEAD_DIM = 64
ROPE_THETA = 10000.0
NORM_EPS = 1e-6
BLOCK = 128
NEG_INF = -1e30

A_Q_HEADS = 6
A_KV_HEADS = 2
A_WINDOW = 128
A_WIDTH = A_Q_HEADS * HEAD_DIM

B_HEADS = 6
B_Q_RANK = 384
B_KV_RANK = 256
B_NOPE = 64
B_ROPE = 32
B_V = 64
B_WIDTH = B_HEADS * B_V

C_GROUPS = 4
C_GROUP_DIM = 64
C_WIDTH = C_GROUPS * C_GROUP_DIM
C_CHUNK = 128

MIX_WIDTH = A_WIDTH + B_WIDTH + C_WIDTH
IN_SIZES = (A_WIDTH, A_KV_HEADS * HEAD_DIM, A_KV_HEADS * HEAD_DIM,
            B_Q_RANK, B_KV_RANK, B_ROPE, C_WIDTH, C_WIDTH)
IN_COLS = A_WIDTH + 2 * A_KV_HEADS * HEAD_DIM + B_Q_RANK + B_KV_RANK + B_ROPE + 2 * C_WIDTH

FFN_HIDDEN = ((8 * D_MODEL // 3 + 255) // 256) * 256
N_MOD = 6

kernel_name = "hybrid_parallel_swa_mla_sgu_block"


def rms_norm(x, w):
    xf = x.astype(jnp.float32)
    y = xf * lax.rsqrt(jnp.mean(xf * xf, axis=-1, keepdims=True) + NORM_EPS)
    return (y * w.astype(jnp.float32)).astype(x.dtype)


def layer_norm(x, w, b):
    xf = x.astype(jnp.float32)
    mu = jnp.mean(xf, axis=-1, keepdims=True)
    var = jnp.mean(jnp.square(xf - mu), axis=-1, keepdims=True)
    y = (xf - mu) * lax.rsqrt(var + NORM_EPS)
    return (y * w.astype(jnp.float32) + b.astype(jnp.float32)).astype(x.dtype)


def rope_tables(positions, dim):
    inv = 1.0 / (ROPE_THETA ** (jnp.arange(0, dim, 2, dtype=jnp.float32) / dim))
    ang = positions.astype(jnp.float32)[..., None] * inv
    return jnp.cos(ang), jnp.sin(ang)


def apply_rope(x, cos, sin):
    x1, x2 = jnp.split(x, 2, axis=-1)
    c = cos[:, :, None, :].astype(x.dtype)
    s = sin[:, :, None, :].astype(x.dtype)
    return jnp.concatenate([x1 * c - x2 * s, x2 * c + x1 * s], axis=-1)


def sliding_window_gqa(q, k, v, sinks):
    b_, s_, hq, d = q.shape
    g = hq // A_KV_HEADS
    nb = s_ // BLOCK
    qb = q.reshape(b_, nb, BLOCK, A_KV_HEADS, g, d)
    pad = ((0, 0), (BLOCK, 0), (0, 0), (0, 0))
    kb = jnp.pad(k, pad).reshape(b_, nb + 1, BLOCK, A_KV_HEADS, d)
    vb = jnp.pad(v, pad).reshape(b_, nb + 1, BLOCK, A_KV_HEADS, d)
    kcat = jnp.concatenate([kb[:, :-1], kb[:, 1:]], axis=2)
    vcat = jnp.concatenate([vb[:, :-1], vb[:, 1:]], axis=2)
    scores = jnp.einsum('bnqhgd,bnkhd->bnhgqk', qb, kcat).astype(jnp.float32) * (d ** -0.5)
    qi = jnp.arange(BLOCK)[:, None]
    kj = jnp.arange(2 * BLOCK)[None, :]
    rel = qi + BLOCK - kj
    band = (rel >= 0) & (rel < A_WINDOW)
    not_pad = (jnp.arange(nb)[:, None, None] > 0) | (kj[None] >= BLOCK)
    mask = (band[None] & not_pad)[None, :, None, None]
    scores = jnp.where(mask, scores, NEG_INF)
    sink = sinks.astype(jnp.float32).reshape(A_KV_HEADS, g)[None, None, :, :, None, None]
    m = jnp.maximum(jnp.max(scores, axis=-1, keepdims=True), sink)
    p = jnp.exp(scores - m)
    p = p / (jnp.sum(p, axis=-1, keepdims=True) + jnp.exp(sink - m))
    out = jnp.einsum('bnhgqk,bnkhd->bnqhgd', p.astype(v.dtype), vcat)
    return out.reshape(b_, s_, hq * d)


def mla(c_q, c_kv, k_rope, q_norm_w, w_uq, kv_norm_w, w_ukv, cos, sin):
    b_, s_, _ = c_q.shape
    q = (rms_norm(c_q, q_norm_w) @ w_uq).reshape(b_, s_, B_HEADS, B_NOPE + B_ROPE)
    q_nope, q_rope = q[..., :B_NOPE], q[..., B_NOPE:]
    q_rope = apply_rope(q_rope, cos, sin)
    kv = (rms_norm(c_kv, kv_norm_w) @ w_ukv).reshape(b_, s_, B_HEADS, B_NOPE + B_V)
    k_nope, v = kv[..., :B_NOPE], kv[..., B_NOPE:]
    k_r = apply_rope(k_rope[:, :, None, :], cos, sin)[:, :, 0, :]
    scale = (B_NOPE + B_ROPE) ** -0.5
    nb = s_ // BLOCK
    qn = q_nope.reshape(b_, nb, BLOCK, B_HEADS, B_NOPE).transpose(1, 0, 2, 3, 4)
    qr = q_rope.reshape(b_, nb, BLOCK, B_HEADS, B_ROPE).transpose(1, 0, 2, 3, 4)
    key_pos = jnp.arange(s_)

    def one_block(args):
        qn_b, qr_b, n = args
        sc = (jnp.einsum('bqhd,bkhd->bhqk', qn_b, k_nope)
              + jnp.einsum('bqhd,bkd->bhqk', qr_b, k_r)).astype(jnp.float32) * scale
        qpos = n * BLOCK + jnp.arange(BLOCK)
        sc = jnp.where(key_pos[None, :] <= qpos[:, None], sc, NEG_INF)
        p = jax.nn.softmax(sc, axis=-1)
        return jnp.einsum('bhqk,bkhd->bqhd', p.astype(v.dtype), v)

    out = lax.map(one_block, (qn, qr, jnp.arange(nb)))
    return out.transpose(1, 0, 2, 3, 4).reshape(b_, s_, B_WIDTH)


def chunked_spatial_gating(u, v, ln_w, ln_b, w_s, b_s):
    v = layer_norm(v, ln_w, ln_b)
    b_, s_, _ = v.shape
    nc = s_ // C_CHUNK
    vc = v.reshape(b_, nc, C_CHUNK, C_GROUPS, C_GROUP_DIM)
    tri = jnp.tril(jnp.ones((C_CHUNK, C_CHUNK), dtype=bool))
    w = jnp.where(tri[None], w_s, jnp.zeros_like(w_s))
    mixed = jnp.einsum('gts,bnsgd->bntgd', w, vc) + b_s.T[None, None, :, :, None]
    return u * mixed.reshape(b_, s_, C_WIDTH)


def setup_inputs(seed: int = 0) -> dict:
    key = jax.random.key(seed)
    ks = jax.random.split(key, 24)
    nrm = lambda k, shape, s: jax.random.normal(k, shape, dtype=jnp.float32) * s
    gain = lambda k, shape: 1.0 + nrm(k, shape, 0.02)
    x = nrm(ks[0], (BATCH, SEQ, D_MODEL), 1.0)
    c = nrm(ks[1], (BATCH, D_MODEL), 1.0)
    offs = jax.random.randint(ks[2], (BATCH,), 0, 1024, dtype=jnp.int32)
    positions = offs[:, None] + jnp.arange(SEQ, dtype=jnp.int32)[None, :]
    return {
        "x": x,
        "c": c,
        "positions": positions,
        "ada_w": nrm(ks[3], (DEPTH, D_MODEL, N_MOD * D_MODEL), 0.5 * D_MODEL ** -0.5),
        "ada_b": nrm(ks[4], (DEPTH, N_MOD * D_MODEL), 0.02),
        "norm1_w": gain(ks[5], (DEPTH, D_MODEL)),
        "w_in": nrm(ks[6], (DEPTH, D_MODEL, IN_COLS), D_MODEL ** -0.5),
        "a_sinks": nrm(ks[7], (DEPTH, A_Q_HEADS), 0.5),
        "b_q_norm_w": gain(ks[8], (DEPTH, B_Q_RANK)),
        "b_w_uq": nrm(ks[9], (DEPTH, B_Q_RANK, B_HEADS * (B_NOPE + B_ROPE)), B_Q_RANK ** -0.5),
        "b_kv_norm_w": gain(ks[10], (DEPTH, B_KV_RANK)),
        "b_w_ukv": nrm(ks[11], (DEPTH, B_KV_RANK, B_HEADS * (B_NOPE + B_V)), B_KV_RANK ** -0.5),
        "c_ln_w": gain(ks[12], (DEPTH, C_WIDTH)),
        "c_ln_b": nrm(ks[13], (DEPTH, C_WIDTH), 0.02),
        "c_w_s": nrm(ks[14], (DEPTH, C_GROUPS, C_CHUNK, C_CHUNK), C_CHUNK ** -0.5),
        "c_b_s": gain(ks[15], (DEPTH, C_GROUPS, C_CHUNK)),
        "out_norm_w": gain(ks[16], (DEPTH, MIX_WIDTH)),
        "w_out": nrm(ks[17], (DEPTH, MIX_WIDTH, D_MODEL), MIX_WIDTH ** -0.5),
        "norm2_w": gain(ks[18], (DEPTH, D_MODEL)),
        "w_gate_up": nrm(ks[19], (DEPTH, D_MODEL, 2 * FFN_HIDDEN), D_MODEL ** -0.5),
        "w_down": nrm(ks[20], (DEPTH, FFN_HIDDEN, D_MODEL), FFN_HIDDEN ** -0.5),
        "final_norm_w": gain(ks[21], (D_MODEL,)),
    }


def reference(x, c, positions, ada_w, ada_b, norm1_w, w_in, a_sinks, b_q_norm_w, b_w_uq,
              b_kv_norm_w, b_w_ukv, c_ln_w, c_ln_b, c_w_s, c_b_s, out_norm_w, w_out,
              norm2_w, w_gate_up, w_down, final_norm_w):
    b_, s_, _ = x.shape
    cos_a, sin_a = rope_tables(positions, HEAD_DIM)
    cos_b, sin_b = rope_tables(positions, B_ROPE)
    split_at = np.cumsum(IN_SIZES)[:-1].tolist()
    c_act = jax.nn.silu(c)
    for l in range(DEPTH):
        mod = (c_act @ ada_w[l] + ada_b[l])[:, None, :]
        sh1, sc1, g1, sh2, sc2, g2 = jnp.split(mod, N_MOD, axis=-1)

        h = rms_norm(x, norm1_w[l]) * (1.0 + sc1) + sh1
        proj = h @ w_in[l]
        a_q, a_k, a_v, b_cq, b_ckv, b_kr, c_u, c_v = jnp.split(proj, split_at, axis=-1)

        qa = apply_rope(a_q.reshape(b_, s_, A_Q_HEADS, HEAD_DIM), cos_a, sin_a)
        ka = apply_rope(a_k.reshape(b_, s_, A_KV_HEADS, HEAD_DIM), cos_a, sin_a)
        va = a_v.reshape(b_, s_, A_KV_HEADS, HEAD_DIM)
        y_a = sliding_window_gqa(qa, ka, va, a_sinks[l])

        y_b = mla(b_cq, b_ckv, b_kr, b_q_norm_w[l], b_w_uq[l], b_kv_norm_w[l], b_w_ukv[l],
                  cos_b, sin_b)

        y_c = chunked_spatial_gating(jax.nn.gelu(c_u, approximate=False),
                                     jax.nn.gelu(c_v, approximate=False),
                                     c_ln_w[l], c_ln_b[l], c_w_s[l], c_b_s[l])

        gw = out_norm_w[l]
        y = jnp.concatenate([
            rms_norm(y_a, gw[:A_WIDTH]),
            rms_norm(y_b, gw[A_WIDTH:A_WIDTH + B_WIDTH]),
            rms_norm(y_c, gw[A_WIDTH + B_WIDTH:]),
        ], axis=-1)
        x = x + g1 * (y @ w_out[l])

        h = rms_norm(x, norm2_w[l]) * (1.0 + sc2) + sh2
        gate, up = jnp.split(h @ w_gate_up[l], 2, axis=-1)
        x = x + g2 * ((jax.nn.silu(gate) * up) @ w_down[l])
    return rms_norm(x, final_norm_w)
```

```python
import functools

import numpy as np
import jax
import jax.numpy as jnp
from jax import lax
from jax.experimental import pallas as pl
from jax.experimental.pallas import tpu as pltpu

F32 = jnp.float32
BF16 = jnp.bfloat16

LANES = 128
HEAD_DIM = 64
ROPE_THETA = 10000.0
NORM_EPS = 1e-6
NEG_INF = -1e30
BLOCK = 128

A_Q_HEADS = 6
A_KV_HEADS = 2
A_WIDTH = A_Q_HEADS * HEAD_DIM
A_KV_WIDTH = A_KV_HEADS * HEAD_DIM
A_HEAD_ORDER = (0, 3, 1, 4, 2, 5)

B_HEADS = 6
B_Q_RANK = 384
B_KV_RANK = 256
B_NOPE = 64
B_ROPE = 32
B_V = 64
B_WIDTH = B_HEADS * B_V
B_QK_PAD = LANES

C_GROUPS = 4
C_GROUP_DIM = 64
C_WIDTH = C_GROUPS * C_GROUP_DIM

N_MOD = 6

OFF_AQ = 0
OFF_AK = OFF_AQ + A_WIDTH
OFF_AV = OFF_AK + A_KV_WIDTH
OFF_BCQ = OFF_AV + A_KV_WIDTH
OFF_BCKV = OFF_BCQ + B_Q_RANK
OFF_CU = OFF_BCKV + B_KV_RANK
OFF_CV = OFF_CU + C_WIDTH
OFF_BKR = OFF_CV + C_WIDTH
IN_COLS_PAD = OFF_BKR + LANES

VMEM_LIMIT = 56 * 1024 * 1024


def _rms(x, w):
    return x * lax.rsqrt(jnp.mean(x * x, axis=-1, keepdims=True) + NORM_EPS) * w


def _gelu(x):
    return 0.5 * x * (1.0 + lax.erf(x * float(np.sqrt(0.5))))


def _lane_iota(shape):
    return lax.broadcasted_iota(jnp.int32, shape, len(shape) - 1)


def _mod_kernel(c_ref, w_ref, b_ref, o_ref):
    c = c_ref[...]
    act = (c * jax.nn.sigmoid(c)).astype(BF16)
    o_ref[0] = jnp.dot(act, w_ref[0].astype(BF16), preferred_element_type=F32) + b_ref[0]


def _modulation(c, ada_w, ada_b):
    depth, d, n = ada_w.shape
    b = c.shape[0]
    tn = 1536
    return pl.pallas_call(
        _mod_kernel,
        grid=(depth, n // tn),
        in_specs=[
            pl.BlockSpec((b, d), lambda l, j: (0, 0)),
            pl.BlockSpec((1, d, tn), lambda l, j: (l, 0, j)),
            pl.BlockSpec((1, 1, tn), lambda l, j: (l, 0, j)),
        ],
        out_specs=pl.BlockSpec((1, b, tn), lambda l, j: (l, 0, j)),
        out_shape=jax.ShapeDtypeStruct((depth, b, n), F32),
        compiler_params=pltpu.CompilerParams(
            dimension_semantics=("arbitrary", "arbitrary"), vmem_limit_bytes=VMEM_LIMIT),
        name="modulation",
    )(c, ada_w, ada_b.reshape(depth, 1, n))


def _in_kernel(x_ref, mod_ref, n1w_ref, win_ref, wuq_ref, wukv_ref, qnw_ref, kvnw_ref,
               ca_ref, sa_ref, cb_ref, sb_ref, lnw_ref, lnb_ref, ws_ref, bs_ref, gwc_ref,
               qa_ref, ka_ref, va_ref, qb_ref, kb_ref, vb_ref, yc_ref):
    tm = x_ref.shape[1]
    x = x_ref[0]
    mod = mod_ref[0]
    h = (_rms(x, n1w_ref[...]) * (1.0 + mod[1:2]) + mod[0:1]).astype(BF16)
    proj = jnp.dot(h, win_ref[...], preferred_element_type=F32)

    lane = _lane_iota((tm, LANES))

    ca = ca_ref[0]
    sa = sa_ref[0]
    a_first = (lane % HEAD_DIM) < (HEAD_DIM // 2)

    def rope_a(t):
        sw = jnp.where(a_first, pltpu.roll(t, LANES - HEAD_DIM // 2, 1), pltpu.roll(t, HEAD_DIM // 2, 1))
        return t * ca + sw * sa

    for p in range(A_WIDTH // LANES):
        sl = slice(OFF_AQ + p * LANES, OFF_AQ + (p + 1) * LANES)
        qa_ref[0, :, p * LANES:(p + 1) * LANES] = rope_a(proj[:, sl]).astype(BF16)
    ka_ref[0] = rope_a(proj[:, OFF_AK:OFF_AK + A_KV_WIDTH]).astype(BF16)
    va_ref[0] = proj[:, OFF_AV:OFF_AV + A_KV_WIDTH].astype(BF16)

    cb = cb_ref[0]
    sb = sb_ref[0]
    b_first = lane < (B_NOPE + B_ROPE // 2)

    def rope_b(t):
        sw = jnp.where(b_first, pltpu.roll(t, LANES - B_ROPE // 2, 1), pltpu.roll(t, B_ROPE // 2, 1))
        return t * cb + sw * sb

    cq = _rms(proj[:, OFF_BCQ:OFF_BCQ + B_Q_RANK], qnw_ref[...]).astype(BF16)
    qb = jnp.dot(cq, wuq_ref[...], preferred_element_type=F32)
    b_scale = float((B_NOPE + B_ROPE) ** -0.5)
    ckv = _rms(proj[:, OFF_BCKV:OFF_BCKV + B_KV_RANK], kvnw_ref[...]).astype(BF16)
    kv = jnp.dot(ckv, wukv_ref[...], preferred_element_type=F32)
    kr = rope_b(proj[:, OFF_BKR:OFF_BKR + LANES])
    for hh in range(B_HEADS):
        sl = slice(hh * LANES, (hh + 1) * LANES)
        qb_ref[0, :, sl] = (rope_b(qb[:, sl]) * b_scale).astype(BF16)
        kb_ref[0, :, sl] = (kv[:, sl] + kr).astype(BF16)
    vb_ref[0] = kv[:, B_HEADS * LANES:].astype(BF16)

    u = _gelu(proj[:, OFF_CU:OFF_CU + C_WIDTH])
    v = _gelu(proj[:, OFF_CV:OFF_CV + C_WIDTH])
    mu = jnp.mean(v, axis=-1, keepdims=True)
    vc = v - mu
    var = jnp.mean(vc * vc, axis=-1, keepdims=True)
    v = (vc * lax.rsqrt(var + NORM_EPS) * lnw_ref[...] + lnb_ref[...]).astype(BF16)

    row = lax.broadcasted_iota(jnp.int32, (BLOCK, BLOCK), 0)
    col = lax.broadcasted_iota(jnp.int32, (BLOCK, BLOCK), 1)
    tri = col <= row
    w_s = [jnp.where(tri, ws_ref[g], 0.0).astype(BF16) for g in range(C_GROUPS)]
    low = _lane_iota((BLOCK, LANES)) < C_GROUP_DIM
    bias = bs_ref[...]
    gwc = gwc_ref[...]
    for c in range(tm // BLOCK):
        rows = slice(c * BLOCK, (c + 1) * BLOCK)
        parts = []
        for p in range(C_WIDTH // LANES):
            vp = v[rows, p * LANES:(p + 1) * LANES]
            r0 = jnp.dot(w_s[2 * p], vp, preferred_element_type=F32)
            r1 = jnp.dot(w_s[2 * p + 1], vp, preferred_element_type=F32)
            parts.append(jnp.where(low, r0, r1))
        mixed = jnp.concatenate(parts, axis=-1) + bias
        yc = u[rows] * mixed
        yc_ref[0, rows, :] = _rms(yc, gwc).astype(BF16)


def _in_proj(x, mod, n1w, win, wuq, wukv, qnw, kvnw, tabs, lnw, lnb, ws, bs_full, gwc, tm):
    b, s, d = x.shape
    tok = lambda w: pl.BlockSpec((1, tm, w), lambda i, j: (i, j, 0))
    full = lambda a: pl.BlockSpec(a.shape, lambda i, j: (0,) * a.ndim)
    ca, sa, cb, sb = tabs
    out_widths = (A_WIDTH, A_KV_WIDTH, A_KV_WIDTH, B_HEADS * B_QK_PAD, B_HEADS * B_QK_PAD, B_WIDTH, C_WIDTH)
    return pl.pallas_call(
        _in_kernel,
        grid=(b, s // tm),
        in_specs=[
            tok(d),
            pl.BlockSpec((1, N_MOD, d), lambda i, j: (i, 0, 0)),
            full(n1w), full(win), full(wuq), full(wukv), full(qnw), full(kvnw),
            tok(LANES), tok(LANES), tok(LANES), tok(LANES),
            full(lnw), full(lnb), full(ws), full(bs_full), full(gwc),
        ],
        out_specs=[tok(w) for w in out_widths],
        out_shape=[jax.ShapeDtypeStruct((b, s, w), BF16) for w in out_widths],
        compiler_params=pltpu.CompilerParams(
            dimension_semantics=("arbitrary", "arbitrary"), vmem_limit_bytes=VMEM_LIMIT),
        name="in_proj",
    )(x, mod, n1w, win, wuq, wukv, qnw, kvnw, ca, sa, cb, sb, lnw, lnb, ws, bs_full, gwc)


def _swa_kernel(q_ref, kc_ref, kp_ref, vc_ref, vp_ref, sink_ref, o_ref):
    tq = q_ref.shape[1]
    first_tile = pl.program_id(1) == 0
    lane = _lane_iota((BLOCK, LANES))
    low = lane < HEAD_DIM
    qi = lax.broadcasted_iota(jnp.int32, (BLOCK, 2 * BLOCK), 0)
    kj = lax.broadcasted_iota(jnp.int32, (BLOCK, 2 * BLOCK), 1)
    rel = qi + BLOCK - kj
    band = (rel >= 0) & (rel < BLOCK)
    for n in range(tq // BLOCK):
        rows = slice(n * BLOCK, (n + 1) * BLOCK)
        if n == 0:
            kcat = jnp.concatenate([kp_ref[0], kc_ref[0, rows, :]], axis=0)
            vcat = jnp.concatenate([vp_ref[0], vc_ref[0, rows, :]], axis=0)
            mask = band & (kj >= jnp.where(first_tile, BLOCK, 0))
        else:
            both = slice((n - 1) * BLOCK, (n + 1) * BLOCK)
            kcat = kc_ref[0, both, :]
            vcat = vc_ref[0, both, :]
            mask = band
        for p in range(A_WIDTH // LANES):
            qp = q_ref[0, rows, p * LANES:(p + 1) * LANES]
            outs = []
            for half in range(2):
                keep = low if half == 0 else jnp.logical_not(low)
                qh = jnp.where(keep, qp, jnp.zeros_like(qp))
                sc = lax.dot_general(qh, kcat, (((1,), (1,)), ((), ())), preferred_element_type=F32)
                sc = jnp.where(mask, sc, NEG_INF)
                sink = sink_ref[2 * p + half:2 * p + half + 1, 0:1]
                m = jnp.maximum(jnp.max(sc, axis=-1, keepdims=True), sink)
                e = jnp.exp(sc - m)
                denom = jnp.sum(e, axis=-1, keepdims=True) + jnp.exp(sink - m)
                pv = jnp.dot(e.astype(BF16), vcat, preferred_element_type=F32)
                outs.append(pv / denom)
            o_ref[0, rows, p * LANES:(p + 1) * LANES] = jnp.where(low, outs[0], outs[1])


def _swa(qa, ka, va, sink_tab, tq):
    b, s, _ = qa.shape
    r = tq // BLOCK
    cur = lambda w: pl.BlockSpec((1, tq, w), lambda i, j: (i, j, 0))
    prev = pl.BlockSpec((1, BLOCK, A_KV_WIDTH), lambda i, j: (i, jnp.maximum(j * r - 1, 0), 0))
    return pl.pallas_call(
        _swa_kernel,
        grid=(b, s // tq),
        in_specs=[cur(A_WIDTH), cur(A_KV_WIDTH), prev, cur(A_KV_WIDTH), prev,
                  pl.BlockSpec(sink_tab.shape, lambda i, j: (0, 0))],
        out_specs=cur(A_WIDTH),
        out_shape=jax.ShapeDtypeStruct((b, s, A_WIDTH), F32),
        compiler_params=pltpu.CompilerParams(
            dimension_semantics=("arbitrary", "arbitrary"), vmem_limit_bytes=VMEM_LIMIT),
        name="swa",
    )(qa, ka, ka, va, va, sink_tab)


def _mla_kernel(q_ref, k_ref, v_ref, o_ref):
    tq = q_ref.shape[1]
    tk = tq
    i = pl.program_id(2)
    low = _lane_iota((tq, LANES)) < B_V
    qi = lax.broadcasted_iota(jnp.int32, (tq, tk), 0)
    kj = lax.broadcasted_iota(jnp.int32, (tq, tk), 1)
    causal = kj <= qi
    outs = []
    for hh in range(2):
        q = q_ref[0, :, hh * LANES:(hh + 1) * LANES]

        def step(j, carry, masked):
            m, l, acc = carry
            start = pl.multiple_of(j * tk, tk)
            k = k_ref[0, pl.ds(start, tk), hh * LANES:(hh + 1) * LANES]
            v = v_ref[0, pl.ds(start, tk), :]
            sc = lax.dot_general(q, k, (((1,), (1,)), ((), ())), preferred_element_type=F32)
            if masked:
                sc = jnp.where(causal, sc, NEG_INF)
            m_new = jnp.maximum(m, jnp.max(sc, axis=-1, keepdims=True))
            alpha = jnp.exp(m - m_new)
            e = jnp.exp(sc - m_new)
            l = alpha * l + jnp.sum(e, axis=-1, keepdims=True)
            acc = alpha * acc + jnp.dot(e.astype(BF16), v, preferred_element_type=F32)
            return m_new, l, acc

        init = (jnp.full((tq, 1), NEG_INF, F32), jnp.zeros((tq, 1), F32), jnp.zeros((tq, LANES), F32))
        carry = lax.fori_loop(0, i, functools.partial(step, masked=False), init)
        m, l, acc = step(i, carry, masked=True)
        outs.append(acc / l)
    o_ref[0] = jnp.where(low, outs[0], outs[1])


def _mla(qb, kb, vb, tq):
    b, s, _ = qb.shape
    pairs = B_HEADS // 2
    return pl.pallas_call(
        _mla_kernel,
        grid=(b, pairs, s // tq),
        in_specs=[
            pl.BlockSpec((1, tq, 2 * B_QK_PAD), lambda i, p, j: (i, j, p)),
            pl.BlockSpec((1, s, 2 * B_QK_PAD), lambda i, p, j: (i, 0, p)),
            pl.BlockSpec((1, s, LANES), lambda i, p, j: (i, 0, p)),
        ],
        out_specs=pl.BlockSpec((1, tq, LANES), lambda i, p, j: (i, j, p)),
        out_shape=jax.ShapeDtypeStruct((b, s, B_WIDTH), F32),
        compiler_params=pltpu.CompilerParams(
            dimension_semantics=("arbitrary", "arbitrary", "arbitrary"), vmem_limit_bytes=VMEM_LIMIT),
        name="mla",
    )(qb, kb, vb)


def _out_kernel(x_ref, ya_ref, yb_ref, yc_ref, mod_ref, gwa_ref, gwb_ref, wout_ref, n2w_ref,
                wgu_ref, wdn_ref, fw_ref, o_ref, *, final):
    x = x_ref[0]
    mod = mod_ref[0]
    y = jnp.concatenate([
        _rms(ya_ref[0], gwa_ref[...]).astype(BF16),
        _rms(yb_ref[0], gwb_ref[...]).astype(BF16),
        yc_ref[0],
    ], axis=-1)
    x1 = x + mod[2:3] * jnp.dot(y, wout_ref[...], preferred_element_type=F32)
    h = (_rms(x1, n2w_ref[...]) * (1.0 + mod[4:5]) + mod[3:4]).astype(BF16)
    n_chunks = wgu_ref.shape[0]
    half = wgu_ref.shape[2] // 2

    def chunk(j, acc):
        gu = jnp.dot(h, wgu_ref[j], preferred_element_type=F32)
        g = gu[:, :half]
        a = (g * jax.nn.sigmoid(g) * gu[:, half:]).astype(BF16)
        return acc + jnp.dot(a, wdn_ref[j], preferred_element_type=F32)

    acc = lax.fori_loop(0, n_chunks, chunk, jnp.zeros(x.shape, F32))
    x2 = x1 + mod[5:6] * acc
    if final:
        x2 = _rms(x2, fw_ref[...])
    o_ref[0] = x2


def _out_ffn(x, ya, yb, yc, mod, gwa, gwb, wout, n2w, wgu, wdn, fw, tm, final):
    b, s, d = x.shape
    tok = lambda w: pl.BlockSpec((1, tm, w), lambda i, j: (i, j, 0))
    full = lambda a: pl.BlockSpec(a.shape, lambda i, j: (0,) * a.ndim, pipeline_mode=pl.Buffered(1))
    return pl.pallas_call(
        functools.partial(_out_kernel, final=final),
        grid=(b, s // tm),
        in_specs=[
            tok(d), tok(A_WIDTH), tok(B_WIDTH), tok(C_WIDTH),
            pl.BlockSpec((1, N_MOD, d), lambda i, j: (i, 0, 0)),
            full(gwa), full(gwb), full(wout), full(n2w), full(wgu), full(wdn), full(fw),
        ],
        out_specs=tok(d),
        out_shape=jax.ShapeDtypeStruct((b, s, d), F32),
        compiler_params=pltpu.CompilerParams(
            dimension_semantics=("arbitrary", "arbitrary"), vmem_limit_bytes=VMEM_LIMIT),
        name="out_ffn",
    )(x, ya, yb, yc, mod, gwa, gwb, wout, n2w, wgu, wdn, fw)


def _in_col_index():
    src_aq, src_ak, src_av = 0, A_WIDTH, A_WIDTH + A_KV_WIDTH
    src_bcq = src_av + A_KV_WIDTH
    src_bckv = src_bcq + B_Q_RANK
    src_bkr = src_bckv + B_KV_RANK
    src_cu = src_bkr + B_ROPE
    src_cv = src_cu + C_WIDTH
    zero = src_cv + C_WIDTH
    idx = []
    for hh in A_HEAD_ORDER:
        idx += list(range(src_aq + hh * HEAD_DIM, src_aq + (hh + 1) * HEAD_DIM))
    idx += list(range(src_ak, src_bkr))
    idx += list(range(src_cu, zero))
    idx += [zero] * B_NOPE + list(range(src_bkr, src_bkr + B_ROPE)) + [zero] * (LANES - B_NOPE - B_ROPE)
    assert len(idx) == IN_COLS_PAD
    return np.asarray(idx, np.int32)


def _rope_tables(positions):
    pos = positions.astype(F32)[..., None]

    def tables(dim):
        inv = 1.0 / (ROPE_THETA ** (jnp.arange(0, dim, 2, dtype=F32) / dim))
        ang = pos * inv
        return jnp.cos(ang), jnp.sin(ang)

    cos_a, sin_a = tables(HEAD_DIM)
    ca = jnp.concatenate([cos_a] * 4, axis=-1)
    sa = jnp.concatenate([-sin_a, sin_a] * 2, axis=-1)
    cos_b, sin_b = tables(B_ROPE)
    ones = jnp.ones(pos.shape[:-1] + (B_NOPE,), F32)
    zeros = jnp.zeros(pos.shape[:-1] + (LANES - B_NOPE - B_ROPE,), F32)
    cb = jnp.concatenate([ones, cos_b, cos_b, zeros], axis=-1)
    sb = jnp.concatenate([0.0 * ones, -sin_b, sin_b, zeros], axis=-1)
    return ca, sa, cb, sb


def kernel(x, c, positions, ada_w, ada_b, norm1_w, w_in, a_sinks, b_q_norm_w, b_w_uq, b_kv_norm_w, b_w_ukv,
           c_ln_w, c_ln_b, c_w_s, c_b_s, out_norm_w, w_out, norm2_w, w_gate_up, w_down, final_norm_w):
    depth = w_in.shape[0]
    b, s, d = x.shape
    ffn = w_down.shape[1]
    tm = min(512, s)
    tq = min(512, s)

    mod = _modulation(c, ada_w, ada_b).reshape(depth, b, N_MOD, d)
    tabs = _rope_tables(positions)

    col_scale = np.ones((IN_COLS_PAD,), np.float32)
    col_scale[OFF_AQ:OFF_AQ + A_WIDTH] = HEAD_DIM ** -0.5
    w_in_p = jnp.concatenate([w_in, jnp.zeros((depth, d, 1), w_in.dtype)], axis=-1)
    w_in_p = (jnp.take(w_in_p, _in_col_index(), axis=-1) * col_scale).astype(BF16)

    w_uq = b_w_uq.reshape(depth, B_Q_RANK, B_HEADS, B_NOPE + B_ROPE)
    w_uq = jnp.pad(w_uq, ((0, 0), (0, 0), (0, 0), (0, B_QK_PAD - B_NOPE - B_ROPE)))
    w_uq = w_uq.reshape(depth, B_Q_RANK, B_HEADS * B_QK_PAD).astype(BF16)
    w_ukv = b_w_ukv.reshape(depth, B_KV_RANK, B_HEADS, B_NOPE + B_V)
    w_uk = jnp.pad(w_ukv[..., :B_NOPE], ((0, 0), (0, 0), (0, 0), (0, B_QK_PAD - B_NOPE)))
    w_ukv = jnp.concatenate([w_uk.reshape(depth, B_KV_RANK, B_HEADS * B_QK_PAD),
                             w_ukv[..., B_NOPE:].reshape(depth, B_KV_RANK, B_WIDTH)], axis=-1).astype(BF16)

    bs_full = jnp.repeat(jnp.swapaxes(c_b_s, 1, 2), C_GROUP_DIM, axis=-1)
    order = np.asarray(A_HEAD_ORDER)
    sink_tab = jnp.broadcast_to(a_sinks[:, order, None], (depth, A_Q_HEADS, LANES))
    sink_tab = jnp.pad(sink_tab, ((0, 0), (0, 8 - A_Q_HEADS), (0, 0)))

    a_perm = (order[:, None] * HEAD_DIM + np.arange(HEAD_DIM)[None, :]).reshape(-1)
    out_perm = np.concatenate([a_perm, np.arange(A_WIDTH, d)])
    gw = jnp.take(out_norm_w, out_perm, axis=-1)
    w_out_p = jnp.take(w_out, out_perm, axis=1).astype(BF16)

    ch = 256
    w_gu = w_gate_up.reshape(depth, d, 2, ffn // ch, ch).transpose(0, 3, 1, 2, 4)
    w_gu = w_gu.reshape(depth, ffn // ch, d, 2 * ch).astype(BF16)
    w_dn = w_down.reshape(depth, ffn // ch, ch, d).astype(BF16)

    row = lambda a: a.reshape(1, -1)
    for l in range(depth):
        qa, ka, va, qb, kb, vb, yc = _in_proj(
            x, mod[l], row(norm1_w[l]), w_in_p[l], w_uq[l], w_ukv[l], row(b_q_norm_w[l]), row(b_kv_norm_w[l]),
            tabs, row(c_ln_w[l]), row(c_ln_b[l]), c_w_s[l], bs_full[l], row(gw[l, A_WIDTH + B_WIDTH:]), tm)
        ya = _swa(qa, ka, va, sink_tab[l], tq)
        yb = _mla(qb, kb, vb, tq)
        x = _out_ffn(x, ya, yb, yc, mod[l], row(gw[l, :A_WIDTH]), row(gw[l, A_WIDTH:A_WIDTH + B_WIDTH]),
                     w_out_p[l], row(norm2_w[l]), w_gu[l], w_dn[l], row(final_norm_w), tm, l == depth - 1)
    return x
```

```python
import functools

import numpy as np
import jax
import jax.numpy as jnp
from jax import lax
from jax.experimental import pallas as pl
from jax.experimental.pallas import tpu as pltpu

F32 = jnp.float32
BF16 = jnp.bfloat16

LANES = 128
HEAD_DIM = 64
ROPE_THETA = 10000.0
NORM_EPS = 1e-6
NEG_INF = -1e30
BLOCK = 128

A_Q_HEADS = 6
A_KV_HEADS = 2
A_WIDTH = A_Q_HEADS * HEAD_DIM
A_KV_WIDTH = A_KV_HEADS * HEAD_DIM
A_HEAD_ORDER = (0, 3, 1, 4, 2, 5)

B_HEADS = 6
B_Q_RANK = 384
B_KV_RANK = 256
B_NOPE = 64
B_ROPE = 32
B_V = 64
B_WIDTH = B_HEADS * B_V
B_QK_PAD = LANES

C_GROUPS = 4
C_GROUP_DIM = 64
C_WIDTH = C_GROUPS * C_GROUP_DIM

N_MOD = 6

OFF_AQ = 0
OFF_AK = OFF_AQ + A_WIDTH
OFF_AV = OFF_AK + A_KV_WIDTH
OFF_BCQ = OFF_AV + A_KV_WIDTH
OFF_BCKV = OFF_BCQ + B_Q_RANK
OFF_CU = OFF_BCKV + B_KV_RANK
OFF_CV = OFF_CU + C_WIDTH
OFF_BKR = OFF_CV + C_WIDTH
IN_COLS_PAD = OFF_BKR + LANES

VMEM_LIMIT = 56 * 1024 * 1024


def _rms(x, w):
    return x * lax.rsqrt(jnp.mean(x * x, axis=-1, keepdims=True) + NORM_EPS) * w


def _gelu(x):
    return 0.5 * x * (1.0 + lax.erf(x * float(np.sqrt(0.5))))


def _lane_iota(shape):
    return lax.broadcasted_iota(jnp.int32, shape, len(shape) - 1)


def _mod_kernel(c_ref, w_ref, b_ref, o_ref):
    c = c_ref[...]
    act = (c * jax.nn.sigmoid(c)).astype(BF16)
    o_ref[0] = jnp.dot(act, w_ref[0].astype(BF16), preferred_element_type=F32) + b_ref[0]


def _modulation(c, ada_w, ada_b):
    depth, d, n = ada_w.shape
    b = c.shape[0]
    tn = 1536
    return pl.pallas_call(
        _mod_kernel,
        grid=(depth, n // tn),
        in_specs=[
            pl.BlockSpec((b, d), lambda l, j: (0, 0)),
            pl.BlockSpec((1, d, tn), lambda l, j: (l, 0, j)),
            pl.BlockSpec((1, 1, tn), lambda l, j: (l, 0, j)),
        ],
        out_specs=pl.BlockSpec((1, b, tn), lambda l, j: (l, 0, j)),
        out_shape=jax.ShapeDtypeStruct((depth, b, n), F32),
        compiler_params=pltpu.CompilerParams(
            dimension_semantics=("arbitrary", "arbitrary"), vmem_limit_bytes=VMEM_LIMIT),
        name="modulation",
    )(c, ada_w, ada_b.reshape(depth, 1, n))


def _in_kernel(x_ref, mod_ref, n1w_ref, win_ref, wuq_ref, wukv_ref, qnw_ref, kvnw_ref,
               ca_ref, sa_ref, cb_ref, sb_ref, lnw_ref, lnb_ref, ws_ref, bs_ref, gwc_ref,
               qa_ref, ka_ref, va_ref, qb_ref, kb_ref, vb_ref, yc_ref):
    tm = x_ref.shape[1]
    x = x_ref[0]
    mod = mod_ref[0]
    h = (_rms(x, n1w_ref[...]) * (1.0 + mod[1:2]) + mod[0:1]).astype(BF16)
    proj = jnp.dot(h, win_ref[...], preferred_element_type=F32)

    lane = _lane_iota((tm, LANES))

    ca = ca_ref[0]
    sa = sa_ref[0]
    a_first = (lane % HEAD_DIM) < (HEAD_DIM // 2)

    def rope_a(t):
        sw = jnp.where(a_first, pltpu.roll(t, LANES - HEAD_DIM // 2, 1), pltpu.roll(t, HEAD_DIM // 2, 1))
        return t * ca + sw * sa

    for p in range(A_WIDTH // LANES):
        sl = slice(OFF_AQ + p * LANES, OFF_AQ + (p + 1) * LANES)
        qa_ref[0, :, p * LANES:(p + 1) * LANES] = rope_a(proj[:, sl]).astype(BF16)
    ka_ref[0] = rope_a(proj[:, OFF_AK:OFF_AK + A_KV_WIDTH]).astype(BF16)
    va_ref[0] = proj[:, OFF_AV:OFF_AV + A_KV_WIDTH].astype(BF16)

    cb = cb_ref[0]
    sb = sb_ref[0]
    b_first = lane < (B_NOPE + B_ROPE // 2)

    def rope_b(t):
        sw = jnp.where(b_first, pltpu.roll(t, LANES - B_ROPE // 2, 1), pltpu.roll(t, B_ROPE // 2, 1))
        return t * cb + sw * sb

    cq = _rms(proj[:, OFF_BCQ:OFF_BCQ + B_Q_RANK], qnw_ref[...]).astype(BF16)
    qb = jnp.dot(cq, wuq_ref[...], preferred_element_type=F32)
    b_scale = float((B_NOPE + B_ROPE) ** -0.5 * np.log2(np.e))
    ckv = _rms(proj[:, OFF_BCKV:OFF_BCKV + B_KV_RANK], kvnw_ref[...]).astype(BF16)
    kv = jnp.dot(ckv, wukv_ref[...], preferred_element_type=F32)
    kr = rope_b(proj[:, OFF_BKR:OFF_BKR + LANES])
    for hh in range(B_HEADS):
        sl = slice(hh * LANES, (hh + 1) * LANES)
        qb_ref[0, :, sl] = (rope_b(qb[:, sl]) * b_scale).astype(BF16)
        kb_ref[0, :, sl] = (kv[:, sl] + kr).astype(BF16)
    vb_ref[0] = kv[:, B_HEADS * LANES:].astype(BF16)

    u = _gelu(proj[:, OFF_CU:OFF_CU + C_WIDTH])
    v = _gelu(proj[:, OFF_CV:OFF_CV + C_WIDTH])
    mu = jnp.mean(v, axis=-1, keepdims=True)
    vc = v - mu
    var = jnp.mean(vc * vc, axis=-1, keepdims=True)
    v = (vc * lax.rsqrt(var + NORM_EPS) * lnw_ref[...] + lnb_ref[...]).astype(BF16)

    row = lax.broadcasted_iota(jnp.int32, (BLOCK, BLOCK), 0)
    col = lax.broadcasted_iota(jnp.int32, (BLOCK, BLOCK), 1)
    tri = col <= row
    w_s = [jnp.where(tri, ws_ref[g], 0.0).astype(BF16) for g in range(C_GROUPS)]
    low = _lane_iota((BLOCK, LANES)) < C_GROUP_DIM
    bias = bs_ref[...]
    gwc = gwc_ref[...]
    for c in range(tm // BLOCK):
        rows = slice(c * BLOCK, (c + 1) * BLOCK)
        parts = []
        for p in range(C_WIDTH // LANES):
            vp = v[rows, p * LANES:(p + 1) * LANES]
            r0 = jnp.dot(w_s[2 * p], vp, preferred_element_type=F32)
            r1 = jnp.dot(w_s[2 * p + 1], vp, preferred_element_type=F32)
            parts.append(jnp.where(low, r0, r1))
        mixed = jnp.concatenate(parts, axis=-1) + bias
        yc = u[rows] * mixed
        yc_ref[0, rows, :] = _rms(yc, gwc).astype(BF16)


def _in_proj(x, mod, n1w, win, wuq, wukv, qnw, kvnw, tabs, lnw, lnb, ws, bs_full, gwc, tm):
    b, s, d = x.shape
    tok = lambda w: pl.BlockSpec((1, tm, w), lambda i, j: (i, j, 0))
    full = lambda a: pl.BlockSpec(a.shape, lambda i, j: (0,) * a.ndim)
    ca, sa, cb, sb = tabs
    out_widths = (A_WIDTH, A_KV_WIDTH, A_KV_WIDTH, B_HEADS * B_QK_PAD, B_HEADS * B_QK_PAD, B_WIDTH, C_WIDTH)
    return pl.pallas_call(
        _in_kernel,
        grid=(b, s // tm),
        in_specs=[
            tok(d),
            pl.BlockSpec((1, N_MOD, d), lambda i, j: (i, 0, 0)),
            full(n1w), full(win), full(wuq), full(wukv), full(qnw), full(kvnw),
            tok(LANES), tok(LANES), tok(LANES), tok(LANES),
            full(lnw), full(lnb), full(ws), full(bs_full), full(gwc),
        ],
        out_specs=[tok(w) for w in out_widths],
        out_shape=[jax.ShapeDtypeStruct((b, s, w), BF16) for w in out_widths],
        compiler_params=pltpu.CompilerParams(
            dimension_semantics=("arbitrary", "arbitrary"), vmem_limit_bytes=VMEM_LIMIT),
        name="in_proj",
    )(x, mod, n1w, win, wuq, wukv, qnw, kvnw, ca, sa, cb, sb, lnw, lnb, ws, bs_full, gwc)


def _swa_kernel(q_ref, kc_ref, kp_ref, vc_ref, vp_ref, sink_ref, o_ref):
    tq = q_ref.shape[1]
    pairs = A_WIDTH // LANES
    first_tile = pl.program_id(1) == 0
    low = _lane_iota((BLOCK, LANES)) < HEAD_DIM
    qi = lax.broadcasted_iota(jnp.int32, (pairs * BLOCK, 2 * BLOCK), 0) % BLOCK
    kj = lax.broadcasted_iota(jnp.int32, (pairs * BLOCK, 2 * BLOCK), 1)
    rel = qi + BLOCK - kj
    band = (rel >= 0) & (rel < BLOCK)
    sinks = [jnp.concatenate([jnp.broadcast_to(sink_ref[2 * p + half:2 * p + half + 1, 0:1], (BLOCK, 1))
                              for p in range(pairs)], axis=0) for half in range(2)]
    for n in range(tq // BLOCK):
        rows = slice(n * BLOCK, (n + 1) * BLOCK)
        if n == 0:
            kcat = jnp.concatenate([kp_ref[0], kc_ref[0, rows, :]], axis=0)
            vcat = jnp.concatenate([vp_ref[0], vc_ref[0, rows, :]], axis=0)
            mask = band & (kj >= jnp.where(first_tile, BLOCK, 0))
        else:
            both = slice((n - 1) * BLOCK, (n + 1) * BLOCK)
            kcat = kc_ref[0, both, :]
            vcat = vc_ref[0, both, :]
            mask = band
        qps = [q_ref[0, rows, p * LANES:(p + 1) * LANES] for p in range(pairs)]
        outs = []
        for half in range(2):
            keep = low if half == 0 else jnp.logical_not(low)
            qs = jnp.concatenate([jnp.where(keep, qp, jnp.zeros_like(qp)) for qp in qps], axis=0)
            sc = lax.dot_general(qs, kcat, (((1,), (1,)), ((), ())), preferred_element_type=F32)
            sc = jnp.where(mask, sc, NEG_INF)
            sink = sinks[half]
            m = jnp.maximum(jnp.max(sc, axis=-1, keepdims=True), sink)
            e = jnp.exp(sc - m)
            denom = jnp.sum(e, axis=-1, keepdims=True) + jnp.exp(sink - m)
            pv = jnp.dot(e.astype(BF16), vcat, preferred_element_type=F32)
            outs.append(pv / denom)
        for p in range(pairs):
            blk = slice(p * BLOCK, (p + 1) * BLOCK)
            o_ref[0, rows, p * LANES:(p + 1) * LANES] = jnp.where(low, outs[0][blk], outs[1][blk])


def _swa(qa, ka, va, sink_tab, tq):
    b, s, _ = qa.shape
    r = tq // BLOCK
    cur = lambda w: pl.BlockSpec((1, tq, w), lambda i, j: (i, j, 0))
    prev = pl.BlockSpec((1, BLOCK, A_KV_WIDTH), lambda i, j: (i, jnp.maximum(j * r - 1, 0), 0))
    return pl.pallas_call(
        _swa_kernel,
        grid=(b, s // tq),
        in_specs=[cur(A_WIDTH), cur(A_KV_WIDTH), prev, cur(A_KV_WIDTH), prev,
                  pl.BlockSpec(sink_tab.shape, lambda i, j: (0, 0))],
        out_specs=cur(A_WIDTH),
        out_shape=jax.ShapeDtypeStruct((b, s, A_WIDTH), F32),
        compiler_params=pltpu.CompilerParams(
            dimension_semantics=("arbitrary", "arbitrary"), vmem_limit_bytes=VMEM_LIMIT),
        name="swa",
    )(qa, ka, ka, va, va, sink_tab)


def _mla_tile(q_ref, k_ref, v_ref, o_ref, n_full):
    tq = q_ref.shape[1]
    tk = tq
    qi = lax.broadcasted_iota(jnp.int32, (tq, tk), 0)
    kj = lax.broadcasted_iota(jnp.int32, (tq, tk), 1)
    causal = kj <= qi
    state = [None, None]
    for j in range(n_full + 1):
        v = v_ref[0, j * tk:(j + 1) * tk, :]
        for hh in range(2):
            q = q_ref[0, :, hh * LANES:(hh + 1) * LANES]
            k = k_ref[0, j * tk:(j + 1) * tk, hh * LANES:(hh + 1) * LANES]
            sc = lax.dot_general(q, k, (((1,), (1,)), ((), ())), preferred_element_type=F32)
            if j == n_full:
                sc = jnp.where(causal, sc, NEG_INF)
            m_blk = jnp.max(sc, axis=-1, keepdims=True)
            if state[hh] is None:
                m_new = m_blk
                e = jnp.exp2(sc - m_new)
                l = jnp.sum(e, axis=-1, keepdims=True)
                acc = jnp.dot(e.astype(BF16), v, preferred_element_type=F32)
            else:
                m, l, acc = state[hh]
                m_new = jnp.maximum(m, m_blk)
                alpha = jnp.exp2(m - m_new)
                e = jnp.exp2(sc - m_new)
                l = alpha * l + jnp.sum(e, axis=-1, keepdims=True)
                acc = alpha * acc + jnp.dot(e.astype(BF16), v, preferred_element_type=F32)
            state[hh] = (m_new, l, acc)
    low = _lane_iota((tq, LANES)) < B_V
    outs = [acc / l for (_, l, acc) in state]
    o_ref[0] = jnp.where(low, outs[0], outs[1])


def _mla_kernel(q_ref, k_ref, v_ref, o_ref):
    i = pl.program_id(2)
    for n_full in range(k_ref.shape[1] // q_ref.shape[1]):
        pl.when(i == n_full)(functools.partial(_mla_tile, q_ref, k_ref, v_ref, o_ref, n_full))


def _mla(qb, kb, vb, tq):
    b, s, _ = qb.shape
    pairs = B_HEADS // 2
    return pl.pallas_call(
        _mla_kernel,
        grid=(b, pairs, s // tq),
        in_specs=[
            pl.BlockSpec((1, tq, 2 * B_QK_PAD), lambda i, p, j: (i, j, p)),
            pl.BlockSpec((1, s, 2 * B_QK_PAD), lambda i, p, j: (i, 0, p)),
            pl.BlockSpec((1, s, LANES), lambda i, p, j: (i, 0, p)),
        ],
        out_specs=pl.BlockSpec((1, tq, LANES), lambda i, p, j: (i, j, p)),
        out_shape=jax.ShapeDtypeStruct((b, s, B_WIDTH), F32),
        compiler_params=pltpu.CompilerParams(
            dimension_semantics=("arbitrary", "arbitrary", "arbitrary"), vmem_limit_bytes=VMEM_LIMIT),
        name="mla",
    )(qb, kb, vb)


def _out_kernel(x_ref, ya_ref, yb_ref, yc_ref, mod_ref, gwa_ref, gwb_ref, wout_ref, n2w_ref,
                wgu_ref, wdn_ref, fw_ref, o_ref, *, final):
    x = x_ref[0]
    mod = mod_ref[0]
    y = jnp.concatenate([
        _rms(ya_ref[0], gwa_ref[...]).astype(BF16),
        _rms(yb_ref[0], gwb_ref[...]).astype(BF16),
        yc_ref[0],
    ], axis=-1)
    x1 = x + mod[2:3] * jnp.dot(y, wout_ref[...], preferred_element_type=F32)
    h = (_rms(x1, n2w_ref[...]) * (1.0 + mod[4:5]) + mod[3:4]).astype(BF16)
    n_chunks = wgu_ref.shape[0]
    half = wgu_ref.shape[2] // 2

    def chunk(j, acc):
        gu = jnp.dot(h, wgu_ref[j], preferred_element_type=F32)
        g = gu[:, :half]
        a = (g * jax.nn.sigmoid(g) * gu[:, half:]).astype(BF16)
        return acc + jnp.dot(a, wdn_ref[j], preferred_element_type=F32)

    acc = lax.fori_loop(0, n_chunks, chunk, jnp.zeros(x.shape, F32), unroll=True)
    x2 = x1 + mod[5:6] * acc
    if final:
        x2 = _rms(x2, fw_ref[...])
    o_ref[0] = x2


def _out_ffn(x, ya, yb, yc, mod, gwa, gwb, wout, n2w, wgu, wdn, fw, tm, final):
    b, s, d = x.shape
    tok = lambda w: pl.BlockSpec((1, tm, w), lambda i, j: (i, j, 0))
    full = lambda a: pl.BlockSpec(a.shape, lambda i, j: (0,) * a.ndim, pipeline_mode=pl.Buffered(1))
    return pl.pallas_call(
        functools.partial(_out_kernel, final=final),
        grid=(b, s // tm),
        in_specs=[
            tok(d), tok(A_WIDTH), tok(B_WIDTH), tok(C_WIDTH),
            pl.BlockSpec((1, N_MOD, d), lambda i, j: (i, 0, 0)),
            full(gwa), full(gwb), full(wout), full(n2w), full(wgu), full(wdn), full(fw),
        ],
        out_specs=tok(d),
        out_shape=jax.ShapeDtypeStruct((b, s, d), F32),
        compiler_params=pltpu.CompilerParams(
            dimension_semantics=("arbitrary", "arbitrary"), vmem_limit_bytes=VMEM_LIMIT),
        name="out_ffn",
    )(x, ya, yb, yc, mod, gwa, gwb, wout, n2w, wgu, wdn, fw)


def _in_col_index():
    src_aq, src_ak, src_av = 0, A_WIDTH, A_WIDTH + A_KV_WIDTH
    src_bcq = src_av + A_KV_WIDTH
    src_bckv = src_bcq + B_Q_RANK
    src_bkr = src_bckv + B_KV_RANK
    src_cu = src_bkr + B_ROPE
    src_cv = src_cu + C_WIDTH
    zero = src_cv + C_WIDTH
    idx = []
    for hh in A_HEAD_ORDER:
        idx += list(range(src_aq + hh * HEAD_DIM, src_aq + (hh + 1) * HEAD_DIM))
    idx += list(range(src_ak, src_bkr))
    idx += list(range(src_cu, zero))
    idx += [zero] * B_NOPE + list(range(src_bkr, src_bkr + B_ROPE)) + [zero] * (LANES - B_NOPE - B_ROPE)
    assert len(idx) == IN_COLS_PAD
    return np.asarray(idx, np.int32)


def _rope_tables(positions):
    pos = positions.astype(F32)[..., None]

    def tables(dim):
        inv = 1.0 / (ROPE_THETA ** (jnp.arange(0, dim, 2, dtype=F32) / dim))
        ang = pos * inv
        return jnp.cos(ang), jnp.sin(ang)

    cos_a, sin_a = tables(HEAD_DIM)
    ca = jnp.concatenate([cos_a] * 4, axis=-1)
    sa = jnp.concatenate([-sin_a, sin_a] * 2, axis=-1)
    cos_b, sin_b = tables(B_ROPE)
    ones = jnp.ones(pos.shape[:-1] + (B_NOPE,), F32)
    zeros = jnp.zeros(pos.shape[:-1] + (LANES - B_NOPE - B_ROPE,), F32)
    cb = jnp.concatenate([ones, cos_b, cos_b, zeros], axis=-1)
    sb = jnp.concatenate([0.0 * ones, -sin_b, sin_b, zeros], axis=-1)
    return ca, sa, cb, sb


def kernel(x, c, positions, ada_w, ada_b, norm1_w, w_in, a_sinks, b_q_norm_w, b_w_uq, b_kv_norm_w, b_w_ukv,
           c_ln_w, c_ln_b, c_w_s, c_b_s, out_norm_w, w_out, norm2_w, w_gate_up, w_down, final_norm_w):
    depth = w_in.shape[0]
    b, s, d = x.shape
    ffn = w_down.shape[1]
    tm = min(512, s)
    tq = min(512, s)

    mod = _modulation(c, ada_w, ada_b).reshape(depth, b, N_MOD, d)
    tabs = _rope_tables(positions)

    col_scale = np.ones((IN_COLS_PAD,), np.float32)
    col_scale[OFF_AQ:OFF_AQ + A_WIDTH] = HEAD_DIM ** -0.5
    w_in_p = jnp.concatenate([w_in, jnp.zeros((depth, d, 1), w_in.dtype)], axis=-1)
    w_in_p = (jnp.take(w_in_p, _in_col_index(), axis=-1) * col_scale).astype(BF16)

    w_uq = b_w_uq.reshape(depth, B_Q_RANK, B_HEADS, B_NOPE + B_ROPE)
    w_uq = jnp.pad(w_uq, ((0, 0), (0, 0), (0, 0), (0, B_QK_PAD - B_NOPE - B_ROPE)))
    w_uq = w_uq.reshape(depth, B_Q_RANK, B_HEADS * B_QK_PAD).astype(BF16)
    w_ukv = b_w_ukv.reshape(depth, B_KV_RANK, B_HEADS, B_NOPE + B_V)
    w_uk = jnp.pad(w_ukv[..., :B_NOPE], ((0, 0), (0, 0), (0, 0), (0, B_QK_PAD - B_NOPE)))
    w_ukv = jnp.concatenate([w_uk.reshape(depth, B_KV_RANK, B_HEADS * B_QK_PAD),
                             w_ukv[..., B_NOPE:].reshape(depth, B_KV_RANK, B_WIDTH)], axis=-1).astype(BF16)

    bs_full = jnp.repeat(jnp.swapaxes(c_b_s, 1, 2), C_GROUP_DIM, axis=-1)
    order = np.asarray(A_HEAD_ORDER)
    sink_tab = jnp.broadcast_to(a_sinks[:, order, None], (depth, A_Q_HEADS, LANES))
    sink_tab = jnp.pad(sink_tab, ((0, 0), (0, 8 - A_Q_HEADS), (0, 0)))

    a_perm = (order[:, None] * HEAD_DIM + np.arange(HEAD_DIM)[None, :]).reshape(-1)
    out_perm = np.concatenate([a_perm, np.arange(A_WIDTH, d)])
    gw = jnp.take(out_norm_w, out_perm, axis=-1)
    w_out_p = jnp.take(w_out, out_perm, axis=1).astype(BF16)

    ch = 256
    w_gu = w_gate_up.reshape(depth, d, 2, ffn // ch, ch).transpose(0, 3, 1, 2, 4)
    w_gu = w_gu.reshape(depth, ffn // ch, d, 2 * ch).astype(BF16)
    w_dn = w_down.reshape(depth, ffn // ch, ch, d).astype(BF16)

    row = lambda a: a.reshape(1, -1)
    for l in range(depth):
        qa, ka, va, qb, kb, vb, yc = _in_proj(
            x, mod[l], row(norm1_w[l]), w_in_p[l], w_uq[l], w_ukv[l], row(b_q_norm_w[l]), row(b_kv_norm_w[l]),
            tabs, row(c_ln_w[l]), row(c_ln_b[l]), c_w_s[l], bs_full[l], row(gw[l, A_WIDTH + B_WIDTH:]), tm)
        ya = _swa(qa, ka, va, sink_tab[l], tq)
        yb = _mla(qb, kb, vb, tq)
        x = _out_ffn(x, ya, yb, yc, mod[l], row(gw[l, :A_WIDTH]), row(gw[l, A_WIDTH:A_WIDTH + B_WIDTH]),
                     w_out_p[l], row(norm2_w[l]), w_gu[l], w_dn[l], row(final_norm_w), tm, l == depth - 1)
    return x
```

```python
import functools

import numpy as np
import jax
import jax.numpy as jnp
from jax import lax
from jax.experimental import pallas as pl
from jax.experimental.pallas import tpu as pltpu

F32 = jnp.float32
BF16 = jnp.bfloat16

LANES = 128
HEAD_DIM = 64
ROPE_THETA = 10000.0
NORM_EPS = 1e-6
NEG_INF = -1e30
BLOCK = 128

A_Q_HEADS = 6
A_KV_HEADS = 2
A_WIDTH = A_Q_HEADS * HEAD_DIM
A_KV_WIDTH = A_KV_HEADS * HEAD_DIM
A_HEAD_ORDER = (0, 3, 1, 4, 2, 5)

B_HEADS = 6
B_Q_RANK = 384
B_KV_RANK = 256
B_NOPE = 64
B_ROPE = 32
B_V = 64
B_WIDTH = B_HEADS * B_V
B_QK_PAD = LANES

C_GROUPS = 4
C_GROUP_DIM = 64
C_WIDTH = C_GROUPS * C_GROUP_DIM

N_MOD = 6

OFF_AQ = 0
OFF_AK = OFF_AQ + A_WIDTH
OFF_AV = OFF_AK + A_KV_WIDTH
OFF_BCQ = OFF_AV + A_KV_WIDTH
OFF_BCKV = OFF_BCQ + B_Q_RANK
OFF_CU = OFF_BCKV + B_KV_RANK
OFF_CV = OFF_CU + C_WIDTH
OFF_BKR = OFF_CV + C_WIDTH
IN_COLS_PAD = OFF_BKR + LANES

VMEM_LIMIT = 56 * 1024 * 1024
MLA_DIAG_CHUNK = 512
FFN_CHUNK = 256


def _rms(x, w):
    return x * lax.rsqrt(jnp.mean(x * x, axis=-1, keepdims=True) + NORM_EPS) * w


def _gelu(x):
    return 0.5 * x * (1.0 + lax.erf(x * float(np.sqrt(0.5))))


def _lane_iota(shape):
    return lax.broadcasted_iota(jnp.int32, shape, len(shape) - 1)


def _mod_kernel(c_ref, w_ref, b_ref, o_ref):
    c = c_ref[...]
    act = (c * jax.nn.sigmoid(c)).astype(BF16)
    o_ref[0] = jnp.dot(act, w_ref[0].astype(BF16), preferred_element_type=F32) + b_ref[0]


def _modulation(c, ada_w, ada_b):
    depth, d, n = ada_w.shape
    b = c.shape[0]
    tn = 1536
    return pl.pallas_call(
        _mod_kernel,
        grid=(depth, n // tn),
        in_specs=[
            pl.BlockSpec((b, d), lambda l, j: (0, 0)),
            pl.BlockSpec((1, d, tn), lambda l, j: (l, 0, j)),
            pl.BlockSpec((1, 1, tn), lambda l, j: (l, 0, j)),
        ],
        out_specs=pl.BlockSpec((1, b, tn), lambda l, j: (l, 0, j)),
        out_shape=jax.ShapeDtypeStruct((depth, b, n), F32),
        compiler_params=pltpu.CompilerParams(
            dimension_semantics=("arbitrary", "arbitrary"), vmem_limit_bytes=VMEM_LIMIT),
        name="modulation",
    )(c, ada_w, ada_b.reshape(depth, 1, n))


def _in_kernel(x_ref, mod_ref, n1w_ref, win_ref, wuq_ref, wukv_ref, qnw_ref, kvnw_ref,
               ca_ref, sa_ref, cb_ref, sb_ref, lnw_ref, lnb_ref, ws_ref, bs_ref, gwc_ref,
               qa_ref, ka_ref, va_ref, qb_ref, kb_ref, vb_ref, yc_ref):
    tm = x_ref.shape[1]
    x = x_ref[0]
    mod = mod_ref[0]
    h = (_rms(x, n1w_ref[...]) * (1.0 + mod[1:2]) + mod[0:1]).astype(BF16)
    proj = jnp.dot(h, win_ref[...], preferred_element_type=F32)

    lane = _lane_iota((tm, LANES))

    ca = ca_ref[0]
    sa = sa_ref[0]
    a_first = (lane % HEAD_DIM) < (HEAD_DIM // 2)

    def rope_a(t):
        sw = jnp.where(a_first, pltpu.roll(t, LANES - HEAD_DIM // 2, 1), pltpu.roll(t, HEAD_DIM // 2, 1))
        return t * ca + sw * sa

    for p in range(A_WIDTH // LANES):
        sl = slice(OFF_AQ + p * LANES, OFF_AQ + (p + 1) * LANES)
        qa_ref[0, :, p * LANES:(p + 1) * LANES] = rope_a(proj[:, sl]).astype(BF16)
    ka_ref[0] = rope_a(proj[:, OFF_AK:OFF_AK + A_KV_WIDTH]).astype(BF16)
    va_ref[0] = proj[:, OFF_AV:OFF_AV + A_KV_WIDTH].astype(BF16)

    cb = cb_ref[0]
    sb = sb_ref[0]
    b_first = lane < (B_NOPE + B_ROPE // 2)

    def rope_b(t):
        sw = jnp.where(b_first, pltpu.roll(t, LANES - B_ROPE // 2, 1), pltpu.roll(t, B_ROPE // 2, 1))
        return t * cb + sw * sb

    cq = _rms(proj[:, OFF_BCQ:OFF_BCQ + B_Q_RANK], qnw_ref[...]).astype(BF16)
    qb = jnp.dot(cq, wuq_ref[...], preferred_element_type=F32)
    b_scale = float((B_NOPE + B_ROPE) ** -0.5 * np.log2(np.e))
    ckv = _rms(proj[:, OFF_BCKV:OFF_BCKV + B_KV_RANK], kvnw_ref[...]).astype(BF16)
    kv = jnp.dot(ckv, wukv_ref[...], preferred_element_type=F32)
    kr = rope_b(proj[:, OFF_BKR:OFF_BKR + LANES])
    for hh in range(B_HEADS):
        sl = slice(hh * LANES, (hh + 1) * LANES)
        qb_ref[0, :, sl] = (rope_b(qb[:, sl]) * b_scale).astype(BF16)
        kb_ref[0, :, sl] = (kv[:, sl] + kr).astype(BF16)
    vb_ref[0] = kv[:, B_HEADS * LANES:].astype(BF16)

    u = _gelu(proj[:, OFF_CU:OFF_CU + C_WIDTH])
    v = _gelu(proj[:, OFF_CV:OFF_CV + C_WIDTH])
    mu = jnp.mean(v, axis=-1, keepdims=True)
    vc = v - mu
    var = jnp.mean(vc * vc, axis=-1, keepdims=True)
    v = (vc * lax.rsqrt(var + NORM_EPS) * lnw_ref[...] + lnb_ref[...]).astype(BF16)

    row = lax.broadcasted_iota(jnp.int32, (BLOCK, BLOCK), 0)
    col = lax.broadcasted_iota(jnp.int32, (BLOCK, BLOCK), 1)
    tri = col <= row
    w_s = [jnp.where(tri, ws_ref[g], 0.0).astype(BF16) for g in range(C_GROUPS)]
    low = _lane_iota((BLOCK, LANES)) < C_GROUP_DIM
    bias = bs_ref[...]
    gwc = gwc_ref[...]
    for c in range(tm // BLOCK):
        rows = slice(c * BLOCK, (c + 1) * BLOCK)
        parts = []
        for p in range(C_WIDTH // LANES):
            vp = v[rows, p * LANES:(p + 1) * LANES]
            r0 = jnp.dot(w_s[2 * p], vp, preferred_element_type=F32)
            r1 = jnp.dot(w_s[2 * p + 1], vp, preferred_element_type=F32)
            parts.append(jnp.where(low, r0, r1))
        mixed = jnp.concatenate(parts, axis=-1) + bias
        yc = u[rows] * mixed
        yc_ref[0, rows, :] = _rms(yc, gwc).astype(BF16)


def _in_proj(x, mod, n1w, win, wuq, wukv, qnw, kvnw, tabs, lnw, lnb, ws, bs_full, gwc, tm, layer):
    b, s, d = x.shape
    tok = lambda w: pl.BlockSpec((1, tm, w), lambda i, j: (i, j, 0))
    full = lambda a: pl.BlockSpec(a.shape, lambda i, j: (0,) * a.ndim)
    stacked = lambda a: pl.BlockSpec((None,) + a.shape[1:], lambda i, j: (layer,) + (0,) * (a.ndim - 1))
    ca, sa, cb, sb = tabs
    out_widths = (A_WIDTH, A_KV_WIDTH, A_KV_WIDTH, B_HEADS * B_QK_PAD, B_HEADS * B_QK_PAD, B_WIDTH, C_WIDTH)
    out_specs = [tok(w) for w in out_widths]
    out_shape = [jax.ShapeDtypeStruct((b, s, w), BF16) for w in out_widths]
    return pl.pallas_call(
        _in_kernel,
        grid=(b, s // tm),
        in_specs=[
            tok(d),
            pl.BlockSpec((1, N_MOD, d), lambda i, j: (i, 0, 0)),
            full(n1w), stacked(win), stacked(wuq), stacked(wukv), full(qnw), full(kvnw),
            tok(LANES), tok(LANES), tok(LANES), tok(LANES),
            full(lnw), full(lnb), stacked(ws), stacked(bs_full), full(gwc),
        ],
        out_specs=out_specs,
        out_shape=out_shape,
        compiler_params=pltpu.CompilerParams(
            dimension_semantics=("arbitrary", "arbitrary"), vmem_limit_bytes=VMEM_LIMIT),
        name="in_proj",
    )(x, mod, n1w, win, wuq, wukv, qnw, kvnw, ca, sa, cb, sb, lnw, lnb, ws, bs_full, gwc)


def _swa_kernel(q_ref, kc_ref, kp_ref, vc_ref, vp_ref, sink_ref, o_ref):
    tq = q_ref.shape[1]
    pairs = A_WIDTH // LANES
    first_tile = pl.program_id(1) == 0
    low = _lane_iota((BLOCK, LANES)) < HEAD_DIM
    qi = lax.broadcasted_iota(jnp.int32, (pairs * BLOCK, 2 * BLOCK), 0) % BLOCK
    kj = lax.broadcasted_iota(jnp.int32, (pairs * BLOCK, 2 * BLOCK), 1)
    rel = qi + BLOCK - kj
    band = (rel >= 0) & (rel < BLOCK)
    sinks = [jnp.concatenate([jnp.broadcast_to(sink_ref[2 * p + half:2 * p + half + 1, 0:1], (BLOCK, 1))
                              for p in range(pairs)], axis=0) for half in range(2)]
    for n in range(tq // BLOCK):
        rows = slice(n * BLOCK, (n + 1) * BLOCK)
        if n == 0:
            kcat = jnp.concatenate([kp_ref[0], kc_ref[0, rows, :]], axis=0)
            vcat = jnp.concatenate([vp_ref[0], vc_ref[0, rows, :]], axis=0)
            mask = band & (kj >= jnp.where(first_tile, BLOCK, 0))
        else:
            both = slice((n - 1) * BLOCK, (n + 1) * BLOCK)
            kcat = kc_ref[0, both, :]
            vcat = vc_ref[0, both, :]
            mask = band
        qps = [q_ref[0, rows, p * LANES:(p + 1) * LANES] for p in range(pairs)]
        outs = []
        for half in range(2):
            keep = low if half == 0 else jnp.logical_not(low)
            qs = jnp.concatenate([jnp.where(keep, qp, jnp.zeros_like(qp)) for qp in qps], axis=0)
            sc = lax.dot_general(qs, kcat, (((1,), (1,)), ((), ())), preferred_element_type=F32)
            sc = jnp.where(mask, sc, NEG_INF)
            sink = sinks[half]
            m = jnp.maximum(jnp.max(sc, axis=-1, keepdims=True), sink)
            e = jnp.exp(sc - m)
            denom = jnp.sum(e, axis=-1, keepdims=True) + jnp.exp(sink - m)
            pv = jnp.dot(e.astype(BF16), vcat, preferred_element_type=F32)
            outs.append(pv / denom)
        for p in range(pairs):
            blk = slice(p * BLOCK, (p + 1) * BLOCK)
            o_ref[0, rows, p * LANES:(p + 1) * LANES] = jnp.where(low, outs[0][blk], outs[1][blk])


def _swa(qa, ka, va, sink_tab, tq):
    b, s, _ = qa.shape
    r = tq // BLOCK
    cur = lambda w: pl.BlockSpec((1, tq, w), lambda i, j: (i, j, 0))
    prev = pl.BlockSpec((1, BLOCK, A_KV_WIDTH), lambda i, j: (i, jnp.maximum(j * r - 1, 0), 0))
    return pl.pallas_call(
        _swa_kernel,
        grid=(b, s // tq),
        in_specs=[cur(A_WIDTH), cur(A_KV_WIDTH), prev, cur(A_KV_WIDTH), prev,
                  pl.BlockSpec(sink_tab.shape, lambda i, j: (0, 0))],
        out_specs=cur(A_WIDTH),
        out_shape=jax.ShapeDtypeStruct((b, s, A_WIDTH), F32),
        compiler_params=pltpu.CompilerParams(
            dimension_semantics=("arbitrary", "arbitrary"), vmem_limit_bytes=VMEM_LIMIT),
        name="swa",
    )(qa, ka, ka, va, va, sink_tab)


def _mla_tile(q_ref, k_ref, v_ref, o_ref, n_full):
    tq = q_ref.shape[1]
    tk = tq
    tc = MLA_DIAG_CHUNK
    n_chunks = tq // tc

    def update(state, sc, v):
        m_blk = jnp.max(sc, axis=-1, keepdims=True)
        if state is None:
            e = jnp.exp2(sc - m_blk)
            return m_blk, jnp.sum(e, axis=-1, keepdims=True), jnp.dot(e.astype(BF16), v, preferred_element_type=F32)
        m, l, acc = state
        m_new = jnp.maximum(m, m_blk)
        alpha = jnp.exp2(m - m_new)
        e = jnp.exp2(sc - m_new)
        l = alpha * l + jnp.sum(e, axis=-1, keepdims=True)
        return m_new, l, alpha * acc + jnp.dot(e.astype(BF16), v, preferred_element_type=F32)

    nt = (((1,), (1,)), ((), ()))
    state = [None, None]
    for j in range(n_full):
        v = v_ref[0, j * tk:(j + 1) * tk, :]
        for hh in range(2):
            q = q_ref[0, :, hh * LANES:(hh + 1) * LANES]
            k = k_ref[0, j * tk:(j + 1) * tk, hh * LANES:(hh + 1) * LANES]
            state[hh] = update(state[hh], lax.dot_general(q, k, nt, preferred_element_type=F32), v)

    low = _lane_iota((tc, LANES)) < B_V
    base = n_full * tk
    for c in range(n_chunks):
        rows = slice(c * tc, (c + 1) * tc)
        nk = (c + 1) * tc
        v = v_ref[0, base:base + nk, :]
        qi = lax.broadcasted_iota(jnp.int32, (tc, nk), 0) + c * tc
        kj = lax.broadcasted_iota(jnp.int32, (tc, nk), 1)
        outs = []
        for hh in range(2):
            q = q_ref[0, rows, hh * LANES:(hh + 1) * LANES]
            k = k_ref[0, base:base + nk, hh * LANES:(hh + 1) * LANES]
            sc = jnp.where(kj <= qi, lax.dot_general(q, k, nt, preferred_element_type=F32), NEG_INF)
            prev = None if state[hh] is None else tuple(t[rows] for t in state[hh])
            _, l, acc = update(prev, sc, v)
            outs.append(acc / l)
        o_ref[0, rows, :] = jnp.where(low, outs[0], outs[1])


def _mla_kernel(q_ref, k_ref, v_ref, o_ref):
    i = pl.program_id(2)
    for n_full in range(k_ref.shape[1] // q_ref.shape[1]):
        pl.when(i == n_full)(functools.partial(_mla_tile, q_ref, k_ref, v_ref, o_ref, n_full))


def _mla(qb, kb, vb, tq):
    b, s, _ = qb.shape
    pairs = B_HEADS // 2
    return pl.pallas_call(
        _mla_kernel,
        grid=(b, pairs, s // tq),
        in_specs=[
            pl.BlockSpec((1, tq, 2 * B_QK_PAD), lambda i, p, j: (i, j, p)),
            pl.BlockSpec((1, s, 2 * B_QK_PAD), lambda i, p, j: (i, 0, p)),
            pl.BlockSpec((1, s, LANES), lambda i, p, j: (i, 0, p)),
        ],
        out_specs=pl.BlockSpec((1, tq, LANES), lambda i, p, j: (i, j, p)),
        out_shape=jax.ShapeDtypeStruct((b, s, B_WIDTH), F32),
        compiler_params=pltpu.CompilerParams(
            dimension_semantics=("arbitrary", "arbitrary", "arbitrary"), vmem_limit_bytes=VMEM_LIMIT),
        name="mla",
    )(qb, kb, vb)


def _out_kernel(x_ref, ya_ref, yb_ref, yc_ref, mod_ref, gwa_ref, gwb_ref, wout_ref, n2w_ref,
                wgu_ref, wdn_ref, fw_ref, o_ref, *, final):
    x = x_ref[0]
    mod = mod_ref[0]
    y = jnp.concatenate([
        _rms(ya_ref[0], gwa_ref[...]).astype(BF16),
        _rms(yb_ref[0], gwb_ref[...]).astype(BF16),
        yc_ref[0],
    ], axis=-1)
    x1 = x + mod[2:3] * jnp.dot(y, wout_ref[...], preferred_element_type=F32)
    h = (_rms(x1, n2w_ref[...]) * (1.0 + mod[4:5]) + mod[3:4]).astype(BF16)
    ffn = wdn_ref.shape[0]
    acc = None
    for lo in range(0, ffn, FFN_CHUNK):
        g = jnp.dot(h, wgu_ref[:, lo:lo + FFN_CHUNK], preferred_element_type=F32)
        u = jnp.dot(h, wgu_ref[:, ffn + lo:ffn + lo + FFN_CHUNK], preferred_element_type=F32)
        a = (g * jax.nn.sigmoid(g) * u).astype(BF16)
        part = jnp.dot(a, wdn_ref[lo:lo + FFN_CHUNK, :], preferred_element_type=F32)
        acc = part if acc is None else acc + part
    x2 = x1 + mod[5:6] * acc
    if final:
        x2 = _rms(x2, fw_ref[...])
    o_ref[0] = x2


def _out_ffn(x, ya, yb, yc, mod, gwa, gwb, wout, n2w, wgu, wdn, fw, tm, layer, final):
    b, s, d = x.shape
    tok = lambda w: pl.BlockSpec((1, tm, w), lambda i, j: (i, j, 0))
    full = lambda a: pl.BlockSpec(a.shape, lambda i, j: (0,) * a.ndim, pipeline_mode=pl.Buffered(1))
    stacked = lambda a: pl.BlockSpec((None,) + a.shape[1:], lambda i, j: (layer,) + (0,) * (a.ndim - 1),
                                     pipeline_mode=pl.Buffered(1))
    return pl.pallas_call(
        functools.partial(_out_kernel, final=final),
        grid=(b, s // tm),
        in_specs=[
            tok(d), tok(A_WIDTH), tok(B_WIDTH), tok(C_WIDTH),
            pl.BlockSpec((1, N_MOD, d), lambda i, j: (i, 0, 0)),
            full(gwa), full(gwb), stacked(wout), full(n2w), stacked(wgu), stacked(wdn), full(fw),
        ],
        out_specs=tok(d),
        out_shape=jax.ShapeDtypeStruct((b, s, d), F32),
        compiler_params=pltpu.CompilerParams(
            dimension_semantics=("arbitrary", "arbitrary"), vmem_limit_bytes=VMEM_LIMIT),
        name="out_ffn",
    )(x, ya, yb, yc, mod, gwa, gwb, wout, n2w, wgu, wdn, fw)


def _in_col_index():
    src_aq, src_ak, src_av = 0, A_WIDTH, A_WIDTH + A_KV_WIDTH
    src_bcq = src_av + A_KV_WIDTH
    src_bckv = src_bcq + B_Q_RANK
    src_bkr = src_bckv + B_KV_RANK
    src_cu = src_bkr + B_ROPE
    src_cv = src_cu + C_WIDTH
    zero = src_cv + C_WIDTH
    idx = []
    for hh in A_HEAD_ORDER:
        idx += list(range(src_aq + hh * HEAD_DIM, src_aq + (hh + 1) * HEAD_DIM))
    idx += list(range(src_ak, src_bkr))
    idx += list(range(src_cu, zero))
    idx += [zero] * B_NOPE + list(range(src_bkr, src_bkr + B_ROPE)) + [zero] * (LANES - B_NOPE - B_ROPE)
    assert len(idx) == IN_COLS_PAD
    return np.asarray(idx, np.int32)


def _take_static(a, idx, axis):
    n = a.shape[axis]
    idx = [int(t) for t in idx]
    parts, start = [], 0
    for pos in range(1, len(idx) + 1):
        if pos < len(idx):
            prev, cur = idx[pos - 1], idx[pos]
            same_run = (prev == n and cur == n) or (prev != n and cur != n and cur == prev + 1)
        else:
            same_run = False
        if not same_run:
            first, count = idx[start], pos - start
            if first == n:
                shape = a.shape[:axis] + (count,) + a.shape[axis + 1:]
                parts.append(jnp.zeros(shape, a.dtype))
            else:
                parts.append(lax.slice_in_dim(a, first, first + count, axis=axis))
            start = pos
    return jnp.concatenate(parts, axis=axis)


def _rope_tables(positions):
    pos = positions.astype(F32)[..., None]

    def tables(dim):
        inv = 1.0 / (ROPE_THETA ** (jnp.arange(0, dim, 2, dtype=F32) / dim))
        ang = pos * inv
        return jnp.cos(ang), jnp.sin(ang)

    cos_a, sin_a = tables(HEAD_DIM)
    ca = jnp.concatenate([cos_a] * 4, axis=-1)
    sa = jnp.concatenate([-sin_a, sin_a] * 2, axis=-1)
    cos_b, sin_b = tables(B_ROPE)
    ones = jnp.ones(pos.shape[:-1] + (B_NOPE,), F32)
    zeros = jnp.zeros(pos.shape[:-1] + (LANES - B_NOPE - B_ROPE,), F32)
    cb = jnp.concatenate([ones, cos_b, cos_b, zeros], axis=-1)
    sb = jnp.concatenate([0.0 * ones, -sin_b, sin_b, zeros], axis=-1)
    return ca, sa, cb, sb


def kernel(x, c, positions, ada_w, ada_b, norm1_w, w_in, a_sinks, b_q_norm_w, b_w_uq, b_kv_norm_w, b_w_ukv,
           c_ln_w, c_ln_b, c_w_s, c_b_s, out_norm_w, w_out, norm2_w, w_gate_up, w_down, final_norm_w):
    depth = w_in.shape[0]
    b, s, d = x.shape
    ffn = w_down.shape[1]
    tm = min(512, s)
    tq = min(512, s)

    mod = _modulation(c, ada_w, ada_b).reshape(depth, b, N_MOD, d)
    tabs = _rope_tables(positions)

    col_scale = np.ones((IN_COLS_PAD,), np.float32)
    col_scale[OFF_AQ:OFF_AQ + A_WIDTH] = HEAD_DIM ** -0.5
    w_in_p = (_take_static(w_in, _in_col_index(), 2) * col_scale).astype(BF16)

    w_uq = b_w_uq.reshape(depth, B_Q_RANK, B_HEADS, B_NOPE + B_ROPE)
    w_uq = jnp.pad(w_uq, ((0, 0), (0, 0), (0, 0), (0, B_QK_PAD - B_NOPE - B_ROPE)))
    w_uq = w_uq.reshape(depth, B_Q_RANK, B_HEADS * B_QK_PAD).astype(BF16)
    w_ukv = b_w_ukv.reshape(depth, B_KV_RANK, B_HEADS, B_NOPE + B_V)
    w_uk = jnp.pad(w_ukv[..., :B_NOPE], ((0, 0), (0, 0), (0, 0), (0, B_QK_PAD - B_NOPE)))
    w_ukv = jnp.concatenate([w_uk.reshape(depth, B_KV_RANK, B_HEADS * B_QK_PAD),
                             w_ukv[..., B_NOPE:].reshape(depth, B_KV_RANK, B_WIDTH)], axis=-1).astype(BF16)

    bs_full = jnp.repeat(jnp.swapaxes(c_b_s, 1, 2), C_GROUP_DIM, axis=-1)
    order = np.asarray(A_HEAD_ORDER)
    sink_tab = jnp.broadcast_to(a_sinks[:, order, None], (depth, A_Q_HEADS, LANES))
    sink_tab = jnp.pad(sink_tab, ((0, 0), (0, 8 - A_Q_HEADS), (0, 0)))

    a_perm = (order[:, None] * HEAD_DIM + np.arange(HEAD_DIM)[None, :]).reshape(-1)
    out_perm = np.concatenate([a_perm, np.arange(A_WIDTH, d)])
    gw = _take_static(out_norm_w, out_perm, 1)
    w_out_p = _take_static(w_out, out_perm, 1).astype(BF16)

    assert ffn % FFN_CHUNK == 0
    w_gu = w_gate_up.astype(BF16)
    w_dn = w_down.astype(BF16)

    row = lambda a: a.reshape(1, -1)
    for l in range(depth):
        qa, ka, va, qb, kb, vb, yc = _in_proj(
            x, mod[l], row(norm1_w[l]), w_in_p, w_uq, w_ukv, row(b_q_norm_w[l]), row(b_kv_norm_w[l]),
            tabs, row(c_ln_w[l]), row(c_ln_b[l]), c_w_s, bs_full, row(gw[l, A_WIDTH + B_WIDTH:]), tm, l)
        ya = _swa(qa, ka, va, sink_tab[l], tq)
        yb = _mla(qb, kb, vb, tq)
        x = _out_ffn(x, ya, yb, yc, mod[l], row(gw[l, :A_WIDTH]), row(gw[l, A_WIDTH:A_WIDTH + B_WIDTH]),
                     w_out_p, row(norm2_w[l]), w_gu, w_dn, row(final_norm_w), tm, l, l == depth - 1)
    return x
```

```python
import functools

import numpy as np
import jax
import jax.numpy as jnp
from jax import lax
from jax.experimental import pallas as pl
from jax.experimental.pallas import tpu as pltpu

F32 = jnp.float32
BF16 = jnp.bfloat16

LANES = 128
HEAD_DIM = 64
ROPE_THETA = 10000.0
NORM_EPS = 1e-6
NEG_INF = -1e30
BLOCK = 128

A_Q_HEADS = 6
A_KV_HEADS = 2
A_WIDTH = A_Q_HEADS * HEAD_DIM
A_KV_WIDTH = A_KV_HEADS * HEAD_DIM
A_HEAD_ORDER = (0, 3, 1, 4, 2, 5)

B_HEADS = 6
B_Q_RANK = 384
B_KV_RANK = 256
B_NOPE = 64
B_ROPE = 32
B_V = 64
B_WIDTH = B_HEADS * B_V
B_QK_PAD = LANES

C_GROUPS = 4
C_GROUP_DIM = 64
C_WIDTH = C_GROUPS * C_GROUP_DIM

N_MOD = 6

OFF_AQ = 0
OFF_AK = OFF_AQ + A_WIDTH
OFF_AV = OFF_AK + A_KV_WIDTH
OFF_BCQ = OFF_AV + A_KV_WIDTH
OFF_BCKV = OFF_BCQ + B_Q_RANK
OFF_CU = OFF_BCKV + B_KV_RANK
OFF_CV = OFF_CU + C_WIDTH
OFF_BKR = OFF_CV + C_WIDTH
IN_COLS_PAD = OFF_BKR + LANES

VMEM_LIMIT = 56 * 1024 * 1024
MLA_KV_SEG = 2048
IN_ROWS = 256
IN_TILE = 1024
OUT_TILE = 512
ATTN_TILE = 512
FFN_CHUNK = 256


def _rms(x, w):
    return x * lax.rsqrt(jnp.mean(x * x, axis=-1, keepdims=True) + NORM_EPS) * w


def _gelu(x):
    return 0.5 * x * (1.0 + lax.erf(x * float(np.sqrt(0.5))))


def _lane_iota(shape):
    return lax.broadcasted_iota(jnp.int32, shape, len(shape) - 1)


def _mod_kernel(c_ref, w_ref, b_ref, o_ref):
    c = c_ref[...]
    act = (c * jax.nn.sigmoid(c)).astype(BF16)
    o_ref[0] = jnp.dot(act, w_ref[0].astype(BF16), preferred_element_type=F32) + b_ref[0]


def _modulation(c, ada_w, ada_b):
    depth, d, n = ada_w.shape
    b = c.shape[0]
    tn = 1536
    return pl.pallas_call(
        _mod_kernel,
        grid=(depth, n // tn),
        in_specs=[
            pl.BlockSpec((b, d), lambda l, j: (0, 0)),
            pl.BlockSpec((1, d, tn), lambda l, j: (l, 0, j)),
            pl.BlockSpec((1, 1, tn), lambda l, j: (l, 0, j)),
        ],
        out_specs=pl.BlockSpec((1, b, tn), lambda l, j: (l, 0, j)),
        out_shape=jax.ShapeDtypeStruct((depth, b, n), F32),
        compiler_params=pltpu.CompilerParams(
            dimension_semantics=("arbitrary", "arbitrary"), vmem_limit_bytes=VMEM_LIMIT),
        name="modulation",
    )(c, ada_w, ada_b.reshape(depth, 1, n))


def _in_kernel(x_ref, mod_ref, n1w_ref, win_ref, wuq_ref, wukv_ref, qnw_ref, kvnw_ref,
               ca_ref, sa_ref, cb_ref, sb_ref, lnw_ref, lnb_ref, ws_ref, bs_ref, gwc_ref,
               qa_ref, ka_ref, va_ref, qb_ref, kb_ref, vb_ref, yc_ref):
    tm = x_ref.shape[1]
    mod = mod_ref[0]
    n1w = n1w_ref[...]
    lane = _lane_iota((IN_ROWS, LANES))
    a_first = (lane % HEAD_DIM) < (HEAD_DIM // 2)
    b_first = lane < (B_NOPE + B_ROPE // 2)
    b_scale = float((B_NOPE + B_ROPE) ** -0.5 * np.log2(np.e))
    row = lax.broadcasted_iota(jnp.int32, (BLOCK, BLOCK), 0)
    col = lax.broadcasted_iota(jnp.int32, (BLOCK, BLOCK), 1)
    w_s = [jnp.where(col <= row, ws_ref[g], 0.0).astype(BF16) for g in range(C_GROUPS)]
    low = _lane_iota((BLOCK, LANES)) < C_GROUP_DIM
    bias = bs_ref[...]
    gwc = gwc_ref[...]

    def project(r0):
        x = x_ref[0, r0:r0 + IN_ROWS, :]
        h = (_rms(x, n1w) * (1.0 + mod[1:2]) + mod[0:1]).astype(BF16)
        return jnp.dot(h, win_ref[...], preferred_element_type=F32)

    starts = list(range(0, tm, IN_ROWS))
    projs = {starts[0]: project(starts[0])}
    for idx, r0 in enumerate(starts):
        if idx + 1 < len(starts):
            projs[starts[idx + 1]] = project(starts[idx + 1])
        proj = projs.pop(r0)
        rs = slice(r0, r0 + IN_ROWS)

        ca = ca_ref[0, rs, :]
        sa = sa_ref[0, rs, :]

        def rope_a(t):
            sw = jnp.where(a_first, pltpu.roll(t, LANES - HEAD_DIM // 2, 1), pltpu.roll(t, HEAD_DIM // 2, 1))
            return t * ca + sw * sa

        for p in range(A_WIDTH // LANES):
            sl = slice(OFF_AQ + p * LANES, OFF_AQ + (p + 1) * LANES)
            qa_ref[0, rs, p * LANES:(p + 1) * LANES] = rope_a(proj[:, sl]).astype(BF16)
        ka_ref[0, rs, :] = rope_a(proj[:, OFF_AK:OFF_AK + A_KV_WIDTH]).astype(BF16)
        va_ref[0, rs, :] = proj[:, OFF_AV:OFF_AV + A_KV_WIDTH].astype(BF16)

        cb = cb_ref[0, rs, :]
        sb = sb_ref[0, rs, :]

        def rope_b(t):
            sw = jnp.where(b_first, pltpu.roll(t, LANES - B_ROPE // 2, 1), pltpu.roll(t, B_ROPE // 2, 1))
            return t * cb + sw * sb

        cq = _rms(proj[:, OFF_BCQ:OFF_BCQ + B_Q_RANK], qnw_ref[...]).astype(BF16)
        qb = jnp.dot(cq, wuq_ref[...], preferred_element_type=F32)
        ckv = _rms(proj[:, OFF_BCKV:OFF_BCKV + B_KV_RANK], kvnw_ref[...]).astype(BF16)
        kv = jnp.dot(ckv, wukv_ref[...], preferred_element_type=F32)
        kr = rope_b(proj[:, OFF_BKR:OFF_BKR + LANES])
        for hh in range(B_HEADS):
            sl = slice(hh * LANES, (hh + 1) * LANES)
            qb_ref[0, rs, sl] = (rope_b(qb[:, sl]) * b_scale).astype(BF16)
            kb_ref[0, rs, sl] = (kv[:, sl] + kr).astype(BF16)
        vb_ref[0, rs, :] = kv[:, B_HEADS * LANES:].astype(BF16)

        u = _gelu(proj[:, OFF_CU:OFF_CU + C_WIDTH])
        v = _gelu(proj[:, OFF_CV:OFF_CV + C_WIDTH])
        mu = jnp.mean(v, axis=-1, keepdims=True)
        vc = v - mu
        var = jnp.mean(vc * vc, axis=-1, keepdims=True)
        v = (vc * lax.rsqrt(var + NORM_EPS) * lnw_ref[...] + lnb_ref[...]).astype(BF16)
        for c in range(IN_ROWS // BLOCK):
            rows = slice(c * BLOCK, (c + 1) * BLOCK)
            parts = []
            for p in range(C_WIDTH // LANES):
                vp = v[rows, p * LANES:(p + 1) * LANES]
                r_lo = jnp.dot(w_s[2 * p], vp, preferred_element_type=F32)
                r_hi = jnp.dot(w_s[2 * p + 1], vp, preferred_element_type=F32)
                parts.append(jnp.where(low, r_lo, r_hi))
            mixed = jnp.concatenate(parts, axis=-1) + bias
            yc = u[rows] * mixed
            yc_ref[0, r0 + c * BLOCK:r0 + (c + 1) * BLOCK, :] = _rms(yc, gwc).astype(BF16)


def _in_proj(x, mod, n1w, win, wuq, wukv, qnw, kvnw, tabs, lnw, lnb, ws, bs_full, gwc, tm, layer):
    b, s, d = x.shape
    tok = lambda w: pl.BlockSpec((1, tm, w), lambda i, j: (i, j, 0))
    full = lambda a: pl.BlockSpec(a.shape, lambda i, j: (0,) * a.ndim)
    stacked = lambda a: pl.BlockSpec((None,) + a.shape[1:], lambda i, j: (layer,) + (0,) * (a.ndim - 1))
    ca, sa, cb, sb = tabs
    out_widths = (A_WIDTH, A_KV_WIDTH, A_KV_WIDTH, B_HEADS * B_QK_PAD, B_HEADS * B_QK_PAD, B_WIDTH, C_WIDTH)
    out_specs = [tok(w) for w in out_widths]
    out_shape = [jax.ShapeDtypeStruct((b, s, w), BF16) for w in out_widths]
    return pl.pallas_call(
        _in_kernel,
        grid=(b, s // tm),
        in_specs=[
            tok(d),
            pl.BlockSpec((1, N_MOD, d), lambda i, j: (i, 0, 0)),
            full(n1w), stacked(win), stacked(wuq), stacked(wukv), full(qnw), full(kvnw),
            tok(LANES), tok(LANES), tok(LANES), tok(LANES),
            full(lnw), full(lnb), stacked(ws), stacked(bs_full), full(gwc),
        ],
        out_specs=out_specs,
        out_shape=out_shape,
        compiler_params=pltpu.CompilerParams(
            dimension_semantics=("arbitrary", "arbitrary"), vmem_limit_bytes=VMEM_LIMIT),
        name="in_proj",
    )(x, mod, n1w, win, wuq, wukv, qnw, kvnw, ca, sa, cb, sb, lnw, lnb, ws, bs_full, gwc)


def _swa_kernel(q_ref, kc_ref, kp_ref, vc_ref, vp_ref, sink_ref, o_ref):
    tq = q_ref.shape[1]
    pairs = A_WIDTH // LANES
    first_tile = pl.program_id(1) == 0
    low = _lane_iota((BLOCK, LANES)) < HEAD_DIM
    qi = lax.broadcasted_iota(jnp.int32, (pairs * BLOCK, 2 * BLOCK), 0) % BLOCK
    kj = lax.broadcasted_iota(jnp.int32, (pairs * BLOCK, 2 * BLOCK), 1)
    rel = qi + BLOCK - kj
    band = (rel >= 0) & (rel < BLOCK)
    sinks = [jnp.concatenate([jnp.broadcast_to(sink_ref[2 * p + half:2 * p + half + 1, 0:1], (BLOCK, 1))
                              for p in range(pairs)], axis=0) for half in range(2)]
    n_blocks = tq // BLOCK
    chains = [(n, half) for n in range(n_blocks) for half in range(2)]
    kcat, vcat, mask = {}, {}, {}
    for n in range(n_blocks):
        if n == 0:
            kcat[n] = jnp.concatenate([kp_ref[0], kc_ref[0, 0:BLOCK, :]], axis=0)
            vcat[n] = jnp.concatenate([vp_ref[0], vc_ref[0, 0:BLOCK, :]], axis=0)
            mask[n] = band & (kj >= jnp.where(first_tile, BLOCK, 0))
        else:
            both = slice((n - 1) * BLOCK, (n + 1) * BLOCK)
            kcat[n] = kc_ref[0, both, :]
            vcat[n] = vc_ref[0, both, :]
            mask[n] = band
    sc, m, pv, denom = {}, {}, {}, {}
    for n, half in chains:
        keep = low if half == 0 else jnp.logical_not(low)
        qs = jnp.concatenate([jnp.where(keep, qp, jnp.zeros_like(qp)) for qp in
                              (q_ref[0, n * BLOCK:(n + 1) * BLOCK, p * LANES:(p + 1) * LANES] for p in range(pairs))],
                             axis=0)
        sc[n, half] = lax.dot_general(qs, kcat[n], (((1,), (1,)), ((), ())), preferred_element_type=F32)
    for n, half in chains:
        sc[n, half] = jnp.where(mask[n], sc[n, half], NEG_INF)
        m[n, half] = jnp.maximum(jnp.max(sc[n, half], axis=-1, keepdims=True), sinks[half])
    ones = jnp.ones((2 * BLOCK, LANES), BF16)
    for n, half in chains:
        e = jnp.exp(sc[n, half] - m[n, half]).astype(BF16)
        pv[n, half] = jnp.dot(e, jnp.concatenate([vcat[n], ones], axis=1), preferred_element_type=F32)
    for n, half in chains:
        denom[n, half] = pv[n, half][:, LANES:] + jnp.exp(sinks[half] - m[n, half])
    for n in range(n_blocks):
        outs = [pv[n, half][:, :LANES] / denom[n, half] for half in range(2)]
        for p in range(pairs):
            blk = slice(p * BLOCK, (p + 1) * BLOCK)
            o_ref[0, n * BLOCK:(n + 1) * BLOCK, p * LANES:(p + 1) * LANES] = jnp.where(low, outs[0][blk], outs[1][blk])


def _swa(qa, ka, va, sink_tab, tq):
    b, s, _ = qa.shape
    r = tq // BLOCK
    cur = lambda w: pl.BlockSpec((1, tq, w), lambda i, j: (i, j, 0))
    prev = pl.BlockSpec((1, BLOCK, A_KV_WIDTH), lambda i, j: (i, jnp.maximum(j * r - 1, 0), 0))
    return pl.pallas_call(
        _swa_kernel,
        grid=(b, s // tq),
        in_specs=[cur(A_WIDTH), cur(A_KV_WIDTH), prev, cur(A_KV_WIDTH), prev,
                  pl.BlockSpec(sink_tab.shape, lambda i, j: (0, 0))],
        out_specs=cur(A_WIDTH),
        out_shape=jax.ShapeDtypeStruct((b, s, A_WIDTH), F32),
        compiler_params=pltpu.CompilerParams(
            dimension_semantics=("arbitrary", "arbitrary"), vmem_limit_bytes=VMEM_LIMIT),
        name="swa",
    )(qa, ka, ka, va, va, sink_tab)


def _mla_tile(q_ref, k_ref, v_ref, o_ref, n_full):
    tq = q_ref.shape[1]
    full_keys = n_full * tq
    segments = [(lo, min(MLA_KV_SEG, full_keys - lo), False) for lo in range(0, full_keys, MLA_KV_SEG)]
    segments.append((full_keys, tq, True))
    qi = lax.broadcasted_iota(jnp.int32, (tq, tq), 0)
    kj = lax.broadcasted_iota(jnp.int32, (tq, tq), 1)
    causal = kj <= qi

    nt = (((1,), (1,)), ((), ()))
    state = [None, None]
    for lo, width, masked in segments:
        v = v_ref[0, lo:lo + width, :]
        low_v = _lane_iota(v.shape) < B_V
        one = jnp.ones_like(v)
        v_aug = (jnp.where(low_v, v, one), jnp.where(low_v, one, v))
        sc = []
        for hh in range(2):
            q = q_ref[0, :, hh * LANES:(hh + 1) * LANES]
            k = k_ref[0, lo:lo + width, hh * LANES:(hh + 1) * LANES]
            sc.append(lax.dot_general(q, k, nt, preferred_element_type=F32))
        m_new = []
        for hh in range(2):
            if masked:
                sc[hh] = jnp.where(causal, sc[hh], NEG_INF)
            m_blk = jnp.max(sc[hh], axis=-1, keepdims=True)
            m_new.append(m_blk if state[hh] is None else jnp.maximum(state[hh][0], m_blk))
        pv = [jnp.dot(jnp.exp2(sc[hh] - m_new[hh]).astype(BF16), v_aug[hh], preferred_element_type=F32)
              for hh in range(2)]
        for hh in range(2):
            if state[hh] is None:
                acc = pv[hh]
            else:
                m, acc = state[hh]
                acc = jnp.exp2(m - m_new[hh]) * acc + pv[hh]
            state[hh] = (m_new[hh], acc)

    low = _lane_iota((tq, LANES)) < B_V
    num = jnp.where(low, state[0][1], state[1][1])
    den = pltpu.roll(jnp.where(low, state[1][1], state[0][1]), B_V, 1)
    o_ref[0] = num / den


def _mla_kernel(q_ref, k_ref, v_ref, o_ref):
    i = pl.program_id(2)
    for n_full in range(k_ref.shape[1] // q_ref.shape[1]):
        pl.when(i == n_full)(functools.partial(_mla_tile, q_ref, k_ref, v_ref, o_ref, n_full))


def _mla(qb, kb, vb, tq):
    b, s, _ = qb.shape
    pairs = B_HEADS // 2
    return pl.pallas_call(
        _mla_kernel,
        grid=(b, pairs, s // tq),
        in_specs=[
            pl.BlockSpec((1, tq, 2 * B_QK_PAD), lambda i, p, j: (i, j, p)),
            pl.BlockSpec((1, s, 2 * B_QK_PAD), lambda i, p, j: (i, 0, p)),
            pl.BlockSpec((1, s, LANES), lambda i, p, j: (i, 0, p)),
        ],
        out_specs=pl.BlockSpec((1, tq, LANES), lambda i, p, j: (i, j, p)),
        out_shape=jax.ShapeDtypeStruct((b, s, B_WIDTH), F32),
        compiler_params=pltpu.CompilerParams(
            dimension_semantics=("arbitrary", "arbitrary", "arbitrary"), vmem_limit_bytes=VMEM_LIMIT),
        name="mla",
    )(qb, kb, vb)


def _out_kernel(x_ref, ya_ref, yb_ref, yc_ref, mod_ref, gwa_ref, gwb_ref, wout_ref, n2w_ref,
                wgu_ref, wdn_ref, fw_ref, o_ref, *, final):
    x = x_ref[0]
    mod = mod_ref[0]
    y = jnp.concatenate([
        _rms(ya_ref[0], gwa_ref[...]).astype(BF16),
        _rms(yb_ref[0], gwb_ref[...]).astype(BF16),
        yc_ref[0],
    ], axis=-1)
    x1 = x + mod[2:3] * jnp.dot(y, wout_ref[...], preferred_element_type=F32)
    h = (_rms(x1, n2w_ref[...]) * (1.0 + mod[4:5]) + mod[3:4]).astype(BF16)
    ffn = wdn_ref.shape[0]
    acc = None
    for lo in range(0, ffn, FFN_CHUNK):
        g = jnp.dot(h, wgu_ref[:, lo:lo + FFN_CHUNK], preferred_element_type=F32)
        u = jnp.dot(h, wgu_ref[:, ffn + lo:ffn + lo + FFN_CHUNK], preferred_element_type=F32)
        a = (g * jax.nn.sigmoid(g) * u).astype(BF16)
        part = jnp.dot(a, wdn_ref[lo:lo + FFN_CHUNK, :], preferred_element_type=F32)
        acc = part if acc is None else acc + part
    x2 = x1 + mod[5:6] * acc
    if final:
        x2 = _rms(x2, fw_ref[...])
    o_ref[0] = x2


def _out_ffn(x, ya, yb, yc, mod, gwa, gwb, wout, n2w, wgu, wdn, fw, tm, layer, final):
    b, s, d = x.shape
    tok = lambda w: pl.BlockSpec((1, tm, w), lambda i, j: (i, j, 0))
    full = lambda a: pl.BlockSpec(a.shape, lambda i, j: (0,) * a.ndim, pipeline_mode=pl.Buffered(1))
    stacked = lambda a: pl.BlockSpec((None,) + a.shape[1:], lambda i, j: (layer,) + (0,) * (a.ndim - 1),
                                     pipeline_mode=pl.Buffered(1))
    return pl.pallas_call(
        functools.partial(_out_kernel, final=final),
        grid=(b, s // tm),
        in_specs=[
            tok(d), tok(A_WIDTH), tok(B_WIDTH), tok(C_WIDTH),
            pl.BlockSpec((1, N_MOD, d), lambda i, j: (i, 0, 0)),
            full(gwa), full(gwb), stacked(wout), full(n2w), stacked(wgu), stacked(wdn), full(fw),
        ],
        out_specs=tok(d),
        out_shape=jax.ShapeDtypeStruct((b, s, d), F32),
        compiler_params=pltpu.CompilerParams(
            dimension_semantics=("arbitrary", "arbitrary"), vmem_limit_bytes=VMEM_LIMIT),
        name="out_ffn",
    )(x, ya, yb, yc, mod, gwa, gwb, wout, n2w, wgu, wdn, fw)


def _in_col_index():
    src_aq, src_ak, src_av = 0, A_WIDTH, A_WIDTH + A_KV_WIDTH
    src_bcq = src_av + A_KV_WIDTH
    src_bckv = src_bcq + B_Q_RANK
    src_bkr = src_bckv + B_KV_RANK
    src_cu = src_bkr + B_ROPE
    src_cv = src_cu + C_WIDTH
    zero = src_cv + C_WIDTH
    idx = []
    for hh in A_HEAD_ORDER:
        idx += list(range(src_aq + hh * HEAD_DIM, src_aq + (hh + 1) * HEAD_DIM))
    idx += list(range(src_ak, src_bkr))
    idx += list(range(src_cu, zero))
    idx += [zero] * B_NOPE + list(range(src_bkr, src_bkr + B_ROPE)) + [zero] * (LANES - B_NOPE - B_ROPE)
    assert len(idx) == IN_COLS_PAD
    return np.asarray(idx, np.int32)


def _take_static(a, idx, axis):
    n = a.shape[axis]
    idx = [int(t) for t in idx]
    parts, start = [], 0
    for pos in range(1, len(idx) + 1):
        if pos < len(idx):
            prev, cur = idx[pos - 1], idx[pos]
            same_run = (prev == n and cur == n) or (prev != n and cur != n and cur == prev + 1)
        else:
            same_run = False
        if not same_run:
            first, count = idx[start], pos - start
            if first == n:
                shape = a.shape[:axis] + (count,) + a.shape[axis + 1:]
                parts.append(jnp.zeros(shape, a.dtype))
            else:
                parts.append(lax.slice_in_dim(a, first, first + count, axis=axis))
            start = pos
    return jnp.concatenate(parts, axis=axis)


def _rope_tables(positions):
    pos = positions.astype(F32)[..., None]

    def tables(dim):
        inv = 1.0 / (ROPE_THETA ** (jnp.arange(0, dim, 2, dtype=F32) / dim))
        ang = pos * inv
        return jnp.cos(ang), jnp.sin(ang)

    cos_a, sin_a = tables(HEAD_DIM)
    ca = jnp.concatenate([cos_a] * 4, axis=-1)
    sa = jnp.concatenate([-sin_a, sin_a] * 2, axis=-1)
    cos_b, sin_b = tables(B_ROPE)
    ones = jnp.ones(pos.shape[:-1] + (B_NOPE,), F32)
    zeros = jnp.zeros(pos.shape[:-1] + (LANES - B_NOPE - B_ROPE,), F32)
    cb = jnp.concatenate([ones, cos_b, cos_b, zeros], axis=-1)
    sb = jnp.concatenate([0.0 * ones, -sin_b, sin_b, zeros], axis=-1)
    return ca, sa, cb, sb


def kernel(x, c, positions, ada_w, ada_b, norm1_w, w_in, a_sinks, b_q_norm_w, b_w_uq, b_kv_norm_w, b_w_ukv,
           c_ln_w, c_ln_b, c_w_s, c_b_s, out_norm_w, w_out, norm2_w, w_gate_up, w_down, final_norm_w):
    depth = w_in.shape[0]
    b, s, d = x.shape
    ffn = w_down.shape[1]
    tm_in, tm_out, tq = min(IN_TILE, s), min(OUT_TILE, s), min(ATTN_TILE, s)

    mod = _modulation(c, ada_w, ada_b).reshape(depth, b, N_MOD, d)
    tabs = _rope_tables(positions)

    col_scale = np.ones((IN_COLS_PAD,), np.float32)
    col_scale[OFF_AQ:OFF_AQ + A_WIDTH] = HEAD_DIM ** -0.5
    w_in_p = (_take_static(w_in, _in_col_index(), 2) * col_scale).astype(BF16)

    w_uq = b_w_uq.reshape(depth, B_Q_RANK, B_HEADS, B_NOPE + B_ROPE)
    w_uq = jnp.pad(w_uq, ((0, 0), (0, 0), (0, 0), (0, B_QK_PAD - B_NOPE - B_ROPE)))
    w_uq = w_uq.reshape(depth, B_Q_RANK, B_HEADS * B_QK_PAD).astype(BF16)
    w_ukv = b_w_ukv.reshape(depth, B_KV_RANK, B_HEADS, B_NOPE + B_V)
    w_uk = jnp.pad(w_ukv[..., :B_NOPE], ((0, 0), (0, 0), (0, 0), (0, B_QK_PAD - B_NOPE)))
    w_ukv = jnp.concatenate([w_uk.reshape(depth, B_KV_RANK, B_HEADS * B_QK_PAD),
                             w_ukv[..., B_NOPE:].reshape(depth, B_KV_RANK, B_WIDTH)], axis=-1).astype(BF16)

    bs_full = jnp.repeat(jnp.swapaxes(c_b_s, 1, 2), C_GROUP_DIM, axis=-1)
    order = np.asarray(A_HEAD_ORDER)
    sink_tab = jnp.broadcast_to(a_sinks[:, order, None], (depth, A_Q_HEADS, LANES))
    sink_tab = jnp.pad(sink_tab, ((0, 0), (0, 8 - A_Q_HEADS), (0, 0)))

    a_perm = (order[:, None] * HEAD_DIM + np.arange(HEAD_DIM)[None, :]).reshape(-1)
    out_perm = np.concatenate([a_perm, np.arange(A_WIDTH, d)])
    gw = _take_static(out_norm_w, out_perm, 1)
    w_out_p = _take_static(w_out, out_perm, 1).astype(BF16)

    assert ffn % FFN_CHUNK == 0
    w_gu = w_gate_up.astype(BF16)
    w_dn = w_down.astype(BF16)

    row = lambda a: a.reshape(1, -1)
    for l in range(depth):
        qa, ka, va, qb, kb, vb, yc = _in_proj(
            x, mod[l], row(norm1_w[l]), w_in_p, w_uq, w_ukv, row(b_q_norm_w[l]), row(b_kv_norm_w[l]),
            tabs, row(c_ln_w[l]), row(c_ln_b[l]), c_w_s, bs_full, row(gw[l, A_WIDTH + B_WIDTH:]), tm_in, l)
        ya = _swa(qa, ka, va, sink_tab[l], tq)
        yb = _mla(qb, kb, vb, tq)
        x = _out_ffn(x, ya, yb, yc, mod[l], row(gw[l, :A_WIDTH]), row(gw[l, A_WIDTH:A_WIDTH + B_WIDTH]),
                     w_out_p, row(norm2_w[l]), w_gu, w_dn, row(final_norm_w), tm_out, l, l == depth - 1)
    return x
```

```python
import functools

import numpy as np
import jax
import jax.numpy as jnp
from jax import lax
from jax.experimental import pallas as pl
from jax.experimental.pallas import tpu as pltpu

F32 = jnp.float32
BF16 = jnp.bfloat16

LANES = 128
HEAD_DIM = 64
ROPE_THETA = 10000.0
NORM_EPS = 1e-6
NEG_INF = -1e30
BLOCK = 128

A_Q_HEADS = 6
A_KV_HEADS = 2
A_WIDTH = A_Q_HEADS * HEAD_DIM
A_KV_WIDTH = A_KV_HEADS * HEAD_DIM
A_HEAD_ORDER = (0, 3, 1, 4, 2, 5)

B_HEADS = 6
B_Q_RANK = 384
B_KV_RANK = 256
B_NOPE = 64
B_ROPE = 32
B_V = 64
B_WIDTH = B_HEADS * B_V
B_QK_PAD = LANES

C_GROUPS = 4
C_GROUP_DIM = 64
C_WIDTH = C_GROUPS * C_GROUP_DIM

N_MOD = 6

OFF_AQ = 0
OFF_AK = OFF_AQ + A_WIDTH
OFF_AV = OFF_AK + A_KV_WIDTH
OFF_BCQ = OFF_AV + A_KV_WIDTH
OFF_BCKV = OFF_BCQ + B_Q_RANK
OFF_CU = OFF_BCKV + B_KV_RANK
OFF_CV = OFF_CU + C_WIDTH
OFF_BKR = OFF_CV + C_WIDTH
IN_COLS_PAD = OFF_BKR + LANES

VMEM_LIMIT = 56 * 1024 * 1024
MLA_KV_SEG = 2048
IN_ROWS = 256
IN_TILE = 1024
OUT_TILE = 512
ATTN_TILE = 512
FFN_CHUNK = 256


def _rms(x, w):
    return x * lax.rsqrt(jnp.mean(x * x, axis=-1, keepdims=True) + NORM_EPS) * w


def _gelu(x):
    return 0.5 * x * (1.0 + lax.erf(x * float(np.sqrt(0.5))))


def _lane_iota(shape):
    return lax.broadcasted_iota(jnp.int32, shape, len(shape) - 1)


def _mod_kernel(c_ref, w_ref, b_ref, o_ref):
    c = c_ref[...]
    act = (c * jax.nn.sigmoid(c)).astype(BF16)
    o_ref[0] = jnp.dot(act, w_ref[0].astype(BF16), preferred_element_type=F32) + b_ref[0]


def _modulation(c, ada_w, ada_b):
    depth, d, n = ada_w.shape
    b = c.shape[0]
    tn = 1536
    return pl.pallas_call(
        _mod_kernel,
        grid=(depth, n // tn),
        in_specs=[
            pl.BlockSpec((b, d), lambda l, j: (0, 0)),
            pl.BlockSpec((1, d, tn), lambda l, j: (l, 0, j)),
            pl.BlockSpec((1, 1, tn), lambda l, j: (l, 0, j)),
        ],
        out_specs=pl.BlockSpec((1, b, tn), lambda l, j: (l, 0, j)),
        out_shape=jax.ShapeDtypeStruct((depth, b, n), F32),
        compiler_params=pltpu.CompilerParams(
            dimension_semantics=("arbitrary", "arbitrary"), vmem_limit_bytes=VMEM_LIMIT),
        name="modulation",
    )(c, ada_w, ada_b.reshape(depth, 1, n))


def _in_kernel(x_ref, mod_ref, n1w_ref, win_ref, wuq_ref, wukv_ref, qnw_ref, kvnw_ref,
               ca_ref, sa_ref, cb_ref, sb_ref, lnw_ref, lnb_ref, ws_ref, bs_ref, gwc_ref,
               qa_ref, ka_ref, va_ref, qb_ref, kb_ref, vb_ref, yc_ref):
    tm = x_ref.shape[1]
    mod = mod_ref[0]
    n1w = n1w_ref[...]
    lane = _lane_iota((IN_ROWS, LANES))
    a_first = (lane % HEAD_DIM) < (HEAD_DIM // 2)
    b_first = lane < (B_NOPE + B_ROPE // 2)
    b_scale = float((B_NOPE + B_ROPE) ** -0.5 * np.log2(np.e))
    row = lax.broadcasted_iota(jnp.int32, (BLOCK, BLOCK), 0)
    col = lax.broadcasted_iota(jnp.int32, (BLOCK, BLOCK), 1)
    w_s = [jnp.where(col <= row, ws_ref[g], 0.0).astype(BF16) for g in range(C_GROUPS)]
    low = _lane_iota((BLOCK, LANES)) < C_GROUP_DIM
    bias = bs_ref[...]
    gwc = gwc_ref[...]

    def project(r0):
        x = x_ref[0, r0:r0 + IN_ROWS, :]
        h = (_rms(x, n1w) * (1.0 + mod[1:2]) + mod[0:1]).astype(BF16)
        return jnp.dot(h, win_ref[...], preferred_element_type=F32)

    starts = list(range(0, tm, IN_ROWS))
    projs = {starts[0]: project(starts[0])}
    for idx, r0 in enumerate(starts):
        if idx + 1 < len(starts):
            projs[starts[idx + 1]] = project(starts[idx + 1])
        proj = projs.pop(r0)
        rs = slice(r0, r0 + IN_ROWS)

        ca = ca_ref[0, rs, :]
        sa = sa_ref[0, rs, :]

        def rope_a(t):
            sw = jnp.where(a_first, pltpu.roll(t, LANES - HEAD_DIM // 2, 1), pltpu.roll(t, HEAD_DIM // 2, 1))
            return t * ca + sw * sa

        for p in range(A_WIDTH // LANES):
            sl = slice(OFF_AQ + p * LANES, OFF_AQ + (p + 1) * LANES)
            qa_ref[0, rs, p * LANES:(p + 1) * LANES] = rope_a(proj[:, sl]).astype(BF16)
        ka_ref[0, rs, :] = rope_a(proj[:, OFF_AK:OFF_AK + A_KV_WIDTH]).astype(BF16)
        va_ref[0, rs, :] = proj[:, OFF_AV:OFF_AV + A_KV_WIDTH].astype(BF16)

        cb = cb_ref[0, rs, :]
        sb = sb_ref[0, rs, :]

        def rope_b(t):
            sw = jnp.where(b_first, pltpu.roll(t, LANES - B_ROPE // 2, 1), pltpu.roll(t, B_ROPE // 2, 1))
            return t * cb + sw * sb

        cq = _rms(proj[:, OFF_BCQ:OFF_BCQ + B_Q_RANK], qnw_ref[...]).astype(BF16)
        qb = jnp.dot(cq, wuq_ref[...], preferred_element_type=F32)
        ckv = _rms(proj[:, OFF_BCKV:OFF_BCKV + B_KV_RANK], kvnw_ref[...]).astype(BF16)
        kv = jnp.dot(ckv, wukv_ref[...], preferred_element_type=F32)
        kr = rope_b(proj[:, OFF_BKR:OFF_BKR + LANES])
        for hh in range(B_HEADS):
            sl = slice(hh * LANES, (hh + 1) * LANES)
            qb_ref[0, rs, sl] = (rope_b(qb[:, sl]) * b_scale).astype(BF16)
            kb_ref[0, rs, sl] = (kv[:, sl] + kr).astype(BF16)
        vb_ref[0, rs, :] = kv[:, B_HEADS * LANES:].astype(BF16)

        u = _gelu(proj[:, OFF_CU:OFF_CU + C_WIDTH])
        v = _gelu(proj[:, OFF_CV:OFF_CV + C_WIDTH])
        mu = jnp.mean(v, axis=-1, keepdims=True)
        vc = v - mu
        var = jnp.mean(vc * vc, axis=-1, keepdims=True)
        v = (vc * lax.rsqrt(var + NORM_EPS) * lnw_ref[...] + lnb_ref[...]).astype(BF16)
        for c in range(IN_ROWS // BLOCK):
            rows = slice(c * BLOCK, (c + 1) * BLOCK)
            parts = []
            for p in range(C_WIDTH // LANES):
                vp = v[rows, p * LANES:(p + 1) * LANES]
                r_lo = jnp.dot(w_s[2 * p], vp, preferred_element_type=F32)
                r_hi = jnp.dot(w_s[2 * p + 1], vp, preferred_element_type=F32)
                parts.append(jnp.where(low, r_lo, r_hi))
            mixed = jnp.concatenate(parts, axis=-1) + bias
            yc = u[rows] * mixed
            yc_ref[0, r0 + c * BLOCK:r0 + (c + 1) * BLOCK, :] = _rms(yc, gwc).astype(BF16)


def _in_proj(x, mod, n1w, win, wuq, wukv, qnw, kvnw, tabs, lnw, lnb, ws, bs_full, gwc, tm, layer):
    b, s, d = x.shape
    tok = lambda w: pl.BlockSpec((1, tm, w), lambda i, j: (i, j, 0))
    full = lambda a: pl.BlockSpec(a.shape, lambda i, j: (0,) * a.ndim)
    stacked = lambda a: pl.BlockSpec((None,) + a.shape[1:], lambda i, j: (layer,) + (0,) * (a.ndim - 1))
    ca, sa, cb, sb = tabs
    out_widths = (A_WIDTH, A_KV_WIDTH, A_KV_WIDTH, B_HEADS * B_QK_PAD, B_HEADS * B_QK_PAD, B_WIDTH, C_WIDTH)
    out_specs = [tok(w) for w in out_widths]
    out_shape = [jax.ShapeDtypeStruct((b, s, w), BF16) for w in out_widths]
    return pl.pallas_call(
        _in_kernel,
        grid=(b, s // tm),
        in_specs=[
            tok(d),
            pl.BlockSpec((1, N_MOD, d), lambda i, j: (i, 0, 0)),
            full(n1w), stacked(win), stacked(wuq), stacked(wukv), full(qnw), full(kvnw),
            tok(LANES), tok(LANES), tok(LANES), tok(LANES),
            full(lnw), full(lnb), stacked(ws), stacked(bs_full), full(gwc),
        ],
        out_specs=out_specs,
        out_shape=out_shape,
        compiler_params=pltpu.CompilerParams(
            dimension_semantics=("arbitrary", "arbitrary"), vmem_limit_bytes=VMEM_LIMIT),
        name="in_proj",
    )(x, mod, n1w, win, wuq, wukv, qnw, kvnw, ca, sa, cb, sb, lnw, lnb, ws, bs_full, gwc)


def _swa_kernel(q_ref, kc_ref, kp_ref, vc_ref, vp_ref, sink_ref, o_ref):
    tq = q_ref.shape[1]
    pairs = A_WIDTH // LANES
    first_tile = pl.program_id(1) == 0
    low = _lane_iota((BLOCK, LANES)) < HEAD_DIM
    qi = lax.broadcasted_iota(jnp.int32, (pairs * BLOCK, 2 * BLOCK), 0) % BLOCK
    kj = lax.broadcasted_iota(jnp.int32, (pairs * BLOCK, 2 * BLOCK), 1)
    rel = qi + BLOCK - kj
    band = (rel >= 0) & (rel < BLOCK)
    sinks = [jnp.concatenate([jnp.broadcast_to(sink_ref[2 * p + half:2 * p + half + 1, :], (BLOCK, LANES))
                              for p in range(pairs)], axis=0) for half in range(2)]
    n_blocks = tq // BLOCK
    chains = [(n, half) for n in range(n_blocks) for half in range(2)]
    kcat, vcat, mask = {}, {}, {}
    for n in range(n_blocks):
        if n == 0:
            kcat[n] = jnp.concatenate([kp_ref[0], kc_ref[0, 0:BLOCK, :]], axis=0)
            vcat[n] = jnp.concatenate([vp_ref[0], vc_ref[0, 0:BLOCK, :]], axis=0)
            mask[n] = band & (kj >= jnp.where(first_tile, BLOCK, 0))
        else:
            both = slice((n - 1) * BLOCK, (n + 1) * BLOCK)
            kcat[n] = kc_ref[0, both, :]
            vcat[n] = vc_ref[0, both, :]
            mask[n] = band
    sc, m, pv, denom = {}, {}, {}, {}
    for n, half in chains:
        keep = low if half == 0 else jnp.logical_not(low)
        qs = jnp.concatenate([jnp.where(keep, qp, jnp.zeros_like(qp)) for qp in
                              (q_ref[0, n * BLOCK:(n + 1) * BLOCK, p * LANES:(p + 1) * LANES] for p in range(pairs))],
                             axis=0)
        sc[n, half] = lax.dot_general(qs, kcat[n], (((1,), (1,)), ((), ())), preferred_element_type=F32)
    for n, half in chains:
        sc[n, half] = jnp.where(mask[n], sc[n, half], NEG_INF)
        m[n, half] = jnp.maximum(jnp.max(sc[n, half], axis=-1, keepdims=True), sinks[half])
    ones = jnp.ones((2 * BLOCK, LANES), BF16)
    for n, half in chains:
        e = jnp.concatenate([jnp.exp(sc[n, half][:, t * LANES:(t + 1) * LANES] - m[n, half])
                             for t in range(2 * BLOCK // LANES)], axis=1).astype(BF16)
        pv[n, half] = jnp.dot(e, jnp.concatenate([vcat[n], ones], axis=1), preferred_element_type=F32)
    for n, half in chains:
        denom[n, half] = pv[n, half][:, LANES:] + jnp.exp(sinks[half] - m[n, half])
    for n in range(n_blocks):
        outs = [pv[n, half][:, :LANES] / denom[n, half] for half in range(2)]
        for p in range(pairs):
            blk = slice(p * BLOCK, (p + 1) * BLOCK)
            o_ref[0, n * BLOCK:(n + 1) * BLOCK, p * LANES:(p + 1) * LANES] = jnp.where(low, outs[0][blk], outs[1][blk])


def _swa(qa, ka, va, sink_tab, tq):
    b, s, _ = qa.shape
    r = tq // BLOCK
    cur = lambda w: pl.BlockSpec((1, tq, w), lambda i, j: (i, j, 0))
    prev = pl.BlockSpec((1, BLOCK, A_KV_WIDTH), lambda i, j: (i, jnp.maximum(j * r - 1, 0), 0))
    return pl.pallas_call(
        _swa_kernel,
        grid=(b, s // tq),
        in_specs=[cur(A_WIDTH), cur(A_KV_WIDTH), prev, cur(A_KV_WIDTH), prev,
                  pl.BlockSpec(sink_tab.shape, lambda i, j: (0, 0))],
        out_specs=cur(A_WIDTH),
        out_shape=jax.ShapeDtypeStruct((b, s, A_WIDTH), F32),
        compiler_params=pltpu.CompilerParams(
            dimension_semantics=("arbitrary", "arbitrary"), vmem_limit_bytes=VMEM_LIMIT),
        name="swa",
    )(qa, ka, ka, va, va, sink_tab)


def _mla_tile(q_ref, k_ref, v_ref, o_ref, n_full):
    tq = q_ref.shape[1]
    full_keys = n_full * tq
    segments = [(lo, min(MLA_KV_SEG, full_keys - lo), False) for lo in range(0, full_keys, MLA_KV_SEG)]
    segments.append((full_keys, tq, True))
    qi = lax.broadcasted_iota(jnp.int32, (tq, tq), 0)
    kj = lax.broadcasted_iota(jnp.int32, (tq, tq), 1)
    causal = kj <= qi

    nt = (((1,), (1,)), ((), ()))

    state = [None, None]
    for lo, width, masked in segments:
        v = v_ref[0, lo:lo + width, :]
        low_v = _lane_iota(v.shape) < B_V
        one = jnp.ones_like(v)
        v_aug = (jnp.where(low_v, v, one), jnp.where(low_v, one, v))
        sc = []
        for hh in range(2):
            q = q_ref[0, :, hh * LANES:(hh + 1) * LANES]
            k = k_ref[0, lo:lo + width, hh * LANES:(hh + 1) * LANES]
            sc.append(lax.dot_general(q, k, nt, preferred_element_type=F32))
        m_new = []
        for hh in range(2):
            if masked:
                sc[hh] = jnp.where(causal, sc[hh], NEG_INF)
            m_blk = jnp.max(sc[hh], axis=-1, keepdims=True)
            m_new.append(m_blk if state[hh] is None else jnp.maximum(state[hh][0], m_blk))
        pv = [jnp.dot(jnp.exp2(sc[hh] - m_new[hh]).astype(BF16), v_aug[hh], preferred_element_type=F32)
              for hh in range(2)]
        for hh in range(2):
            if state[hh] is None:
                acc = pv[hh]
            else:
                m, acc = state[hh]
                acc = jnp.exp2(m - m_new[hh]) * acc + pv[hh]
            state[hh] = (m_new[hh], acc)

    low = _lane_iota((tq, LANES)) < B_V
    num = jnp.where(low, state[0][1], state[1][1])
    den = pltpu.roll(jnp.where(low, state[1][1], state[0][1]), B_V, 1)
    o_ref[0] = num / den


def _mla_kernel(q_ref, k_ref, v_ref, o_ref):
    i = pl.program_id(2)
    for n_full in range(k_ref.shape[1] // q_ref.shape[1]):
        pl.when(i == n_full)(functools.partial(_mla_tile, q_ref, k_ref, v_ref, o_ref, n_full))


def _mla(qb, kb, vb, tq):
    b, s, _ = qb.shape
    pairs = B_HEADS // 2
    return pl.pallas_call(
        _mla_kernel,
        grid=(b, pairs, s // tq),
        in_specs=[
            pl.BlockSpec((1, tq, 2 * B_QK_PAD), lambda i, p, j: (i, j, p)),
            pl.BlockSpec((1, s, 2 * B_QK_PAD), lambda i, p, j: (i, 0, p)),
            pl.BlockSpec((1, s, LANES), lambda i, p, j: (i, 0, p)),
        ],
        out_specs=pl.BlockSpec((1, tq, LANES), lambda i, p, j: (i, j, p)),
        out_shape=jax.ShapeDtypeStruct((b, s, B_WIDTH), F32),
        compiler_params=pltpu.CompilerParams(
            dimension_semantics=("arbitrary", "arbitrary", "arbitrary"), vmem_limit_bytes=VMEM_LIMIT),
        name="mla",
    )(qb, kb, vb)


def _out_kernel(x_ref, ya_ref, yb_ref, yc_ref, mod_ref, gwa_ref, gwb_ref, wout_ref, n2w_ref,
                wgu_ref, wdn_ref, fw_ref, o_ref, *, final):
    x = x_ref[0]
    mod = mod_ref[0]
    y = jnp.concatenate([
        _rms(ya_ref[0], gwa_ref[...]).astype(BF16),
        _rms(yb_ref[0], gwb_ref[...]).astype(BF16),
        yc_ref[0],
    ], axis=-1)
    x1 = x + mod[2:3] * jnp.dot(y, wout_ref[...], preferred_element_type=F32)
    h = (_rms(x1, n2w_ref[...]) * (1.0 + mod[4:5]) + mod[3:4]).astype(BF16)
    ffn = wdn_ref.shape[0]
    acc = None
    for lo in range(0, ffn, FFN_CHUNK):
        g = jnp.dot(h, wgu_ref[:, lo:lo + FFN_CHUNK], preferred_element_type=F32)
        u = jnp.dot(h, wgu_ref[:, ffn + lo:ffn + lo + FFN_CHUNK], preferred_element_type=F32)
        a = (g * jax.nn.sigmoid(g) * u).astype(BF16)
        part = jnp.dot(a, wdn_ref[lo:lo + FFN_CHUNK, :], preferred_element_type=F32)
        acc = part if acc is None else acc + part
    x2 = x1 + mod[5:6] * acc
    if final:
        x2 = _rms(x2, fw_ref[...])
    o_ref[0] = x2


def _out_ffn(x, ya, yb, yc, mod, gwa, gwb, wout, n2w, wgu, wdn, fw, tm, layer, final):
    b, s, d = x.shape
    tok = lambda w: pl.BlockSpec((1, tm, w), lambda i, j: (i, j, 0))
    full = lambda a: pl.BlockSpec(a.shape, lambda i, j: (0,) * a.ndim, pipeline_mode=pl.Buffered(1))
    stacked = lambda a: pl.BlockSpec((None,) + a.shape[1:], lambda i, j: (layer,) + (0,) * (a.ndim - 1),
                                     pipeline_mode=pl.Buffered(1))
    return pl.pallas_call(
        functools.partial(_out_kernel, final=final),
        grid=(b, s // tm),
        in_specs=[
            tok(d), tok(A_WIDTH), tok(B_WIDTH), tok(C_WIDTH),
            pl.BlockSpec((1, N_MOD, d), lambda i, j: (i, 0, 0)),
            full(gwa), full(gwb), stacked(wout), full(n2w), stacked(wgu), stacked(wdn), full(fw),
        ],
        out_specs=tok(d),
        out_shape=jax.ShapeDtypeStruct((b, s, d), F32),
        compiler_params=pltpu.CompilerParams(
            dimension_semantics=("arbitrary", "arbitrary"), vmem_limit_bytes=VMEM_LIMIT),
        name="out_ffn",
    )(x, ya, yb, yc, mod, gwa, gwb, wout, n2w, wgu, wdn, fw)


def _in_col_index():
    src_aq, src_ak, src_av = 0, A_WIDTH, A_WIDTH + A_KV_WIDTH
    src_bcq = src_av + A_KV_WIDTH
    src_bckv = src_bcq + B_Q_RANK
    src_bkr = src_bckv + B_KV_RANK
    src_cu = src_bkr + B_ROPE
    src_cv = src_cu + C_WIDTH
    zero = src_cv + C_WIDTH
    idx = []
    for hh in A_HEAD_ORDER:
        idx += list(range(src_aq + hh * HEAD_DIM, src_aq + (hh + 1) * HEAD_DIM))
    idx += list(range(src_ak, src_bkr))
    idx += list(range(src_cu, zero))
    idx += [zero] * B_NOPE + list(range(src_bkr, src_bkr + B_ROPE)) + [zero] * (LANES - B_NOPE - B_ROPE)
    assert len(idx) == IN_COLS_PAD
    return np.asarray(idx, np.int32)


def _take_static(a, idx, axis):
    n = a.shape[axis]
    idx = [int(t) for t in idx]
    parts, start = [], 0
    for pos in range(1, len(idx) + 1):
        if pos < len(idx):
            prev, cur = idx[pos - 1], idx[pos]
            same_run = (prev == n and cur == n) or (prev != n and cur != n and cur == prev + 1)
        else:
            same_run = False
        if not same_run:
            first, count = idx[start], pos - start
            if first == n:
                shape = a.shape[:axis] + (count,) + a.shape[axis + 1:]
                parts.append(jnp.zeros(shape, a.dtype))
            else:
                parts.append(lax.slice_in_dim(a, first, first + count, axis=axis))
            start = pos
    return jnp.concatenate(parts, axis=axis)


def _rope_tables(positions):
    pos = positions.astype(F32)[..., None]

    def tables(dim):
        inv = 1.0 / (ROPE_THETA ** (jnp.arange(0, dim, 2, dtype=F32) / dim))
        ang = pos * inv
        return jnp.cos(ang), jnp.sin(ang)

    cos_a, sin_a = tables(HEAD_DIM)
    ca = jnp.concatenate([cos_a] * 4, axis=-1)
    sa = jnp.concatenate([-sin_a, sin_a] * 2, axis=-1)
    cos_b, sin_b = tables(B_ROPE)
    ones = jnp.ones(pos.shape[:-1] + (B_NOPE,), F32)
    zeros = jnp.zeros(pos.shape[:-1] + (LANES - B_NOPE - B_ROPE,), F32)
    cb = jnp.concatenate([ones, cos_b, cos_b, zeros], axis=-1)
    sb = jnp.concatenate([0.0 * ones, -sin_b, sin_b, zeros], axis=-1)
    return ca, sa, cb, sb


def kernel(x, c, positions, ada_w, ada_b, norm1_w, w_in, a_sinks, b_q_norm_w, b_w_uq, b_kv_norm_w, b_w_ukv,
           c_ln_w, c_ln_b, c_w_s, c_b_s, out_norm_w, w_out, norm2_w, w_gate_up, w_down, final_norm_w):
    depth = w_in.shape[0]
    b, s, d = x.shape
    ffn = w_down.shape[1]
    tm_in, tm_out, tq = min(IN_TILE, s), min(OUT_TILE, s), min(ATTN_TILE, s)

    mod = _modulation(c, ada_w, ada_b).reshape(depth, b, N_MOD, d)
    tabs = _rope_tables(positions)

    col_scale = np.ones((IN_COLS_PAD,), np.float32)
    col_scale[OFF_AQ:OFF_AQ + A_WIDTH] = HEAD_DIM ** -0.5
    w_in_p = (_take_static(w_in, _in_col_index(), 2) * col_scale).astype(BF16)

    w_uq = b_w_uq.reshape(depth, B_Q_RANK, B_HEADS, B_NOPE + B_ROPE)
    w_uq = jnp.pad(w_uq, ((0, 0), (0, 0), (0, 0), (0, B_QK_PAD - B_NOPE - B_ROPE)))
    w_uq = w_uq.reshape(depth, B_Q_RANK, B_HEADS * B_QK_PAD).astype(BF16)
    w_ukv = b_w_ukv.reshape(depth, B_KV_RANK, B_HEADS, B_NOPE + B_V)
    w_uk = jnp.pad(w_ukv[..., :B_NOPE], ((0, 0), (0, 0), (0, 0), (0, B_QK_PAD - B_NOPE)))
    w_ukv = jnp.concatenate([w_uk.reshape(depth, B_KV_RANK, B_HEADS * B_QK_PAD),
                             w_ukv[..., B_NOPE:].reshape(depth, B_KV_RANK, B_WIDTH)], axis=-1).astype(BF16)

    bs_full = jnp.repeat(jnp.swapaxes(c_b_s, 1, 2), C_GROUP_DIM, axis=-1)
    order = np.asarray(A_HEAD_ORDER)
    sink_tab = jnp.broadcast_to(a_sinks[:, order, None], (depth, A_Q_HEADS, LANES))
    sink_tab = jnp.pad(sink_tab, ((0, 0), (0, 8 - A_Q_HEADS), (0, 0)))

    a_perm = (order[:, None] * HEAD_DIM + np.arange(HEAD_DIM)[None, :]).reshape(-1)
    out_perm = np.concatenate([a_perm, np.arange(A_WIDTH, d)])
    gw = _take_static(out_norm_w, out_perm, 1)
    w_out_p = _take_static(w_out, out_perm, 1).astype(BF16)

    assert ffn % FFN_CHUNK == 0
    w_gu = w_gate_up.astype(BF16)
    w_dn = w_down.astype(BF16)

    row = lambda a: a.reshape(1, -1)
    for l in range(depth):
        qa, ka, va, qb, kb, vb, yc = _in_proj(
            x, mod[l], row(norm1_w[l]), w_in_p, w_uq, w_ukv, row(b_q_norm_w[l]), row(b_kv_norm_w[l]),
            tabs, row(c_ln_w[l]), row(c_ln_b[l]), c_w_s, bs_full, row(gw[l, A_WIDTH + B_WIDTH:]), tm_in, l)
        ya = _swa(qa, ka, va, sink_tab[l], tq)
        yb = _mla(qb, kb, vb, tq)
        x = _out_ffn(x, ya, yb, yc, mod[l], row(gw[l, :A_WIDTH]), row(gw[l, A_WIDTH:A_WIDTH + B_WIDTH]),
                     w_out_p, row(norm2_w[l]), w_gu, w_dn, row(final_norm_w), tm_out, l, l == depth - 1)
    return x
```

```python
import functools

import numpy as np
import jax
import jax.numpy as jnp
from jax import lax
from jax.experimental import pallas as pl
from jax.experimental.pallas import tpu as pltpu

F32 = jnp.float32
BF16 = jnp.bfloat16

LANES = 128
HEAD_DIM = 64
ROPE_THETA = 10000.0
NORM_EPS = 1e-6
NEG_INF = -1e30
LOG2E = float(np.log2(np.e))
BLOCK = 128

A_Q_HEADS = 6
A_KV_HEADS = 2
A_WIDTH = A_Q_HEADS * HEAD_DIM
A_KV_WIDTH = A_KV_HEADS * HEAD_DIM
A_HEAD_ORDER = (0, 3, 1, 4, 2, 5)

B_HEADS = 6
B_Q_RANK = 384
B_KV_RANK = 256
B_NOPE = 64
B_ROPE = 32
B_V = 64
B_WIDTH = B_HEADS * B_V
B_QK_PAD = LANES

C_GROUPS = 4
C_GROUP_DIM = 64
C_WIDTH = C_GROUPS * C_GROUP_DIM

N_MOD = 6

OFF_AQ = 0
OFF_AK = OFF_AQ + A_WIDTH
OFF_AV = OFF_AK + A_KV_WIDTH
OFF_BCQ = OFF_AV + A_KV_WIDTH
OFF_BCKV = OFF_BCQ + B_Q_RANK
OFF_CU = OFF_BCKV + B_KV_RANK
OFF_CV = OFF_CU + C_WIDTH
OFF_BKR = OFF_CV + C_WIDTH
IN_COLS_PAD = OFF_BKR + LANES

VMEM_LIMIT = 56 * 1024 * 1024
MLA_KV_SEG = 2048
IN_ROWS = 256
IN_TILE = 1024
OUT_TILE = 1024
OUT_ROWS = 512
ATTN_TILE = 512
FFN_CHUNK = 256


def _rms(x, w):
    return x * lax.rsqrt(jnp.mean(x * x, axis=-1, keepdims=True) + NORM_EPS) * w


def _gelu(x):
    return 0.5 * x * (1.0 + lax.erf(x * float(np.sqrt(0.5))))


def _lane_iota(shape):
    return lax.broadcasted_iota(jnp.int32, shape, len(shape) - 1)


def _mod_kernel(c_ref, w_ref, b_ref, o_ref):
    c = c_ref[...]
    act = (c * jax.nn.sigmoid(c)).astype(BF16)
    o_ref[0] = jnp.dot(act, w_ref[0].astype(BF16), preferred_element_type=F32) + b_ref[0]


def _modulation(c, ada_w, ada_b):
    depth, d, n = ada_w.shape
    b = c.shape[0]
    tn = 1536
    return pl.pallas_call(
        _mod_kernel,
        grid=(depth, n // tn),
        in_specs=[
            pl.BlockSpec((b, d), lambda l, j: (0, 0)),
            pl.BlockSpec((1, d, tn), lambda l, j: (l, 0, j)),
            pl.BlockSpec((1, 1, tn), lambda l, j: (l, 0, j)),
        ],
        out_specs=pl.BlockSpec((1, b, tn), lambda l, j: (l, 0, j)),
        out_shape=jax.ShapeDtypeStruct((depth, b, n), F32),
        compiler_params=pltpu.CompilerParams(
            dimension_semantics=("arbitrary", "arbitrary"), vmem_limit_bytes=VMEM_LIMIT),
        name="modulation",
    )(c, ada_w, ada_b.reshape(depth, 1, n))


def _in_kernel(x_ref, mod_ref, n1w_ref, win_ref, wuq_ref, wukv_ref, qnw_ref, kvnw_ref,
               ca_ref, sa_ref, cb_ref, sb_ref, lnw_ref, lnb_ref, ws_ref, bs_ref, gwc_ref,
               qa_ref, ka_ref, va_ref, qb_ref, kb_ref, vb_ref, yc_ref):
    tm = x_ref.shape[1]
    mod = mod_ref[0]
    n1w = n1w_ref[...]
    lane = _lane_iota((IN_ROWS, LANES))
    a_first = (lane % HEAD_DIM) < (HEAD_DIM // 2)
    b_first = lane < (B_NOPE + B_ROPE // 2)
    b_scale = float((B_NOPE + B_ROPE) ** -0.5) * LOG2E
    row = lax.broadcasted_iota(jnp.int32, (BLOCK, BLOCK), 0)
    col = lax.broadcasted_iota(jnp.int32, (BLOCK, BLOCK), 1)
    w_s = [jnp.where(col <= row, ws_ref[g], 0.0).astype(BF16) for g in range(C_GROUPS)]
    low = _lane_iota((BLOCK, LANES)) < C_GROUP_DIM
    bias = bs_ref[...]
    gwc = gwc_ref[...]

    def project(r0):
        x = x_ref[0, r0:r0 + IN_ROWS, :]
        h = (_rms(x, n1w) * (1.0 + mod[1:2]) + mod[0:1]).astype(BF16)
        return jnp.dot(h, win_ref[...], preferred_element_type=F32)

    starts = list(range(0, tm, IN_ROWS))
    projs = {starts[0]: project(starts[0])}
    for idx, r0 in enumerate(starts):
        if idx + 1 < len(starts):
            projs[starts[idx + 1]] = project(starts[idx + 1])
        proj = projs.pop(r0)
        rs = slice(r0, r0 + IN_ROWS)

        ca = ca_ref[0, rs, :]
        sa = sa_ref[0, rs, :]

        def rope_a(t):
            sw = jnp.where(a_first, pltpu.roll(t, LANES - HEAD_DIM // 2, 1), pltpu.roll(t, HEAD_DIM // 2, 1))
            return t * ca + sw * sa

        for p in range(A_WIDTH // LANES):
            sl = slice(OFF_AQ + p * LANES, OFF_AQ + (p + 1) * LANES)
            qa_ref[0, rs, p * LANES:(p + 1) * LANES] = (rope_a(proj[:, sl]) * LOG2E).astype(BF16)
        ka_ref[0, rs, :] = rope_a(proj[:, OFF_AK:OFF_AK + A_KV_WIDTH]).astype(BF16)
        va_ref[0, rs, :] = proj[:, OFF_AV:OFF_AV + A_KV_WIDTH].astype(BF16)

        cb = cb_ref[0, rs, :]
        sb = sb_ref[0, rs, :]

        def rope_b(t):
            sw = jnp.where(b_first, pltpu.roll(t, LANES - B_ROPE // 2, 1), pltpu.roll(t, B_ROPE // 2, 1))
            return t * cb + sw * sb

        cq = _rms(proj[:, OFF_BCQ:OFF_BCQ + B_Q_RANK], qnw_ref[...]).astype(BF16)
        qb = jnp.dot(cq, wuq_ref[...], preferred_element_type=F32)
        ckv = _rms(proj[:, OFF_BCKV:OFF_BCKV + B_KV_RANK], kvnw_ref[...]).astype(BF16)
        kv = jnp.dot(ckv, wukv_ref[...], preferred_element_type=F32)
        kr = rope_b(proj[:, OFF_BKR:OFF_BKR + LANES])
        for hh in range(B_HEADS):
            sl = slice(hh * LANES, (hh + 1) * LANES)
            qb_ref[0, rs, sl] = (rope_b(qb[:, sl]) * b_scale).astype(BF16)
            kb_ref[0, rs, sl] = (kv[:, sl] + kr).astype(BF16)
        vb_ref[0, rs, :] = kv[:, B_HEADS * LANES:].astype(BF16)

        u = _gelu(proj[:, OFF_CU:OFF_CU + C_WIDTH])
        v = _gelu(proj[:, OFF_CV:OFF_CV + C_WIDTH])
        mu = jnp.mean(v, axis=-1, keepdims=True)
        vc = v - mu
        var = jnp.mean(vc * vc, axis=-1, keepdims=True)
        v = (vc * lax.rsqrt(var + NORM_EPS) * lnw_ref[...] + lnb_ref[...]).astype(BF16)
        for c in range(IN_ROWS // BLOCK):
            rows = slice(c * BLOCK, (c + 1) * BLOCK)
            parts = []
            for p in range(C_WIDTH // LANES):
                vp = v[rows, p * LANES:(p + 1) * LANES]
                r_lo = jnp.dot(w_s[2 * p], vp, preferred_element_type=F32)
                r_hi = jnp.dot(w_s[2 * p + 1], vp, preferred_element_type=F32)
                parts.append(jnp.where(low, r_lo, r_hi))
            mixed = jnp.concatenate(parts, axis=-1) + bias
            yc = u[rows] * mixed
            yc_ref[0, r0 + c * BLOCK:r0 + (c + 1) * BLOCK, :] = _rms(yc, gwc).astype(BF16)


def _in_proj(x, mod, n1w, win, wuq, wukv, qnw, kvnw, tabs, lnw, lnb, ws, bs_full, gwc, tm, layer):
    b, s, d = x.shape
    tok = lambda w: pl.BlockSpec((1, tm, w), lambda i, j: (i, j, 0))
    full = lambda a: pl.BlockSpec(a.shape, lambda i, j: (0,) * a.ndim)
    stacked = lambda a: pl.BlockSpec((None,) + a.shape[1:], lambda i, j: (layer,) + (0,) * (a.ndim - 1))
    ca, sa, cb, sb = tabs
    out_widths = (A_WIDTH, A_KV_WIDTH, A_KV_WIDTH, B_HEADS * B_QK_PAD, B_HEADS * B_QK_PAD, B_WIDTH, C_WIDTH)
    out_specs = [tok(w) for w in out_widths]
    out_shape = [jax.ShapeDtypeStruct((b, s, w), BF16) for w in out_widths]
    return pl.pallas_call(
        _in_kernel,
        grid=(b, s // tm),
        in_specs=[
            tok(d),
            pl.BlockSpec((1, N_MOD, d), lambda i, j: (i, 0, 0)),
            full(n1w), stacked(win), stacked(wuq), stacked(wukv), full(qnw), full(kvnw),
            tok(LANES), tok(LANES), tok(LANES), tok(LANES),
            full(lnw), full(lnb), stacked(ws), stacked(bs_full), full(gwc),
        ],
        out_specs=out_specs,
        out_shape=out_shape,
        compiler_params=pltpu.CompilerParams(
            dimension_semantics=("arbitrary", "arbitrary"), vmem_limit_bytes=VMEM_LIMIT),
        name="in_proj",
    )(x, mod, n1w, win, wuq, wukv, qnw, kvnw, ca, sa, cb, sb, lnw, lnb, ws, bs_full, gwc)


def _swa_kernel(q_ref, kc_ref, kp_ref, vc_ref, vp_ref, sink_ref, o_ref, bias_ref):
    tq = q_ref.shape[1]
    pairs = A_WIDTH // LANES
    first_tile = pl.program_id(1) == 0
    low = _lane_iota((BLOCK, LANES)) < HEAD_DIM
    qi = lax.broadcasted_iota(jnp.int32, (pairs * BLOCK, 2 * BLOCK), 0) % BLOCK
    kj = lax.broadcasted_iota(jnp.int32, (pairs * BLOCK, 2 * BLOCK), 1)
    rel = qi + BLOCK - kj
    band = (rel >= 0) & (rel < BLOCK)
    bias_ref[...] = jnp.where(band, 0.0, NEG_INF)
    sinks = [jnp.concatenate([jnp.broadcast_to(sink_ref[2 * p + half:2 * p + half + 1, :], (BLOCK, LANES))
                              for p in range(pairs)], axis=0) for half in range(2)]
    n_blocks = tq // BLOCK
    chains = [(n, half) for n in range(n_blocks) for half in range(2)]
    kcat, vcat = {}, {}
    for n in range(n_blocks):
        if n == 0:
            kcat[n] = jnp.concatenate([kp_ref[0], kc_ref[0, 0:BLOCK, :]], axis=0)
            vcat[n] = jnp.concatenate([vp_ref[0], vc_ref[0, 0:BLOCK, :]], axis=0)
        else:
            both = slice((n - 1) * BLOCK, (n + 1) * BLOCK)
            kcat[n] = kc_ref[0, both, :]
            vcat[n] = vc_ref[0, both, :]
    sc, m, pv, denom = {}, {}, {}, {}
    for n, half in chains:
        keep = low if half == 0 else jnp.logical_not(low)
        qs = jnp.concatenate([jnp.where(keep, qp, jnp.zeros_like(qp)) for qp in
                              (q_ref[0, n * BLOCK:(n + 1) * BLOCK, p * LANES:(p + 1) * LANES] for p in range(pairs))],
                             axis=0)
        sc[n, half] = lax.dot_general(qs, kcat[n], (((1,), (1,)), ((), ())), preferred_element_type=F32)
    for n, half in chains:
        if n == 0:
            sc[n, half] = jnp.where(band & (kj >= jnp.where(first_tile, BLOCK, 0)), sc[n, half], NEG_INF)
        else:
            sc[n, half] = sc[n, half] + bias_ref[...]
        m[n, half] = jnp.maximum(jnp.max(sc[n, half], axis=-1, keepdims=True), sinks[half])
    ones = jnp.ones((2 * BLOCK, LANES), BF16)
    for n, half in chains:
        e = jnp.concatenate([jnp.exp2(sc[n, half][:, t * LANES:(t + 1) * LANES] - m[n, half])
                             for t in range(2 * BLOCK // LANES)], axis=1).astype(BF16)
        pv[n, half] = jnp.dot(e, jnp.concatenate([vcat[n], ones], axis=1), preferred_element_type=F32)
    for n, half in chains:
        denom[n, half] = pv[n, half][:, LANES:] + jnp.exp2(sinks[half] - m[n, half])
    for n in range(n_blocks):
        outs = [pv[n, half][:, :LANES] / denom[n, half] for half in range(2)]
        for p in range(pairs):
            blk = slice(p * BLOCK, (p + 1) * BLOCK)
            o_ref[0, n * BLOCK:(n + 1) * BLOCK, p * LANES:(p + 1) * LANES] = jnp.where(low, outs[0][blk], outs[1][blk])


def _swa(qa, ka, va, sink_tab, tq):
    b, s, _ = qa.shape
    r = tq // BLOCK
    cur = lambda w: pl.BlockSpec((1, tq, w), lambda i, j: (i, j, 0))
    prev = pl.BlockSpec((1, BLOCK, A_KV_WIDTH), lambda i, j: (i, jnp.maximum(j * r - 1, 0), 0))
    return pl.pallas_call(
        _swa_kernel,
        grid=(b, s // tq),
        in_specs=[cur(A_WIDTH), cur(A_KV_WIDTH), prev, cur(A_KV_WIDTH), prev,
                  pl.BlockSpec(sink_tab.shape, lambda i, j: (0, 0))],
        out_specs=cur(A_WIDTH),
        out_shape=jax.ShapeDtypeStruct((b, s, A_WIDTH), F32),
        scratch_shapes=[pltpu.VMEM((A_WIDTH // LANES * BLOCK, 2 * BLOCK), F32)],
        compiler_params=pltpu.CompilerParams(
            dimension_semantics=("arbitrary", "arbitrary"), vmem_limit_bytes=VMEM_LIMIT),
        name="swa",
    )(qa, ka, ka, va, va, sink_tab)


def _mla_tile(q_ref, k_ref, v_ref, o_ref, n_full):
    tq = q_ref.shape[1]
    full_keys = n_full * tq
    segments = [(lo, min(MLA_KV_SEG, full_keys - lo), False) for lo in range(0, full_keys, MLA_KV_SEG)]
    segments.append((full_keys, tq, True))
    qi = lax.broadcasted_iota(jnp.int32, (tq, tq), 0)
    kj = lax.broadcasted_iota(jnp.int32, (tq, tq), 1)
    causal = kj <= qi

    nt = (((1,), (1,)), ((), ()))

    state = [None, None]
    for lo, width, masked in segments:
        v = v_ref[0, lo:lo + width, :]
        low_v = _lane_iota(v.shape) < B_V
        one = jnp.ones_like(v)
        v_aug = (jnp.where(low_v, v, one), jnp.where(low_v, one, v))
        sc = []
        for hh in range(2):
            q = q_ref[0, :, hh * LANES:(hh + 1) * LANES]
            k = k_ref[0, lo:lo + width, hh * LANES:(hh + 1) * LANES]
            sc.append(lax.dot_general(q, k, nt, preferred_element_type=F32))
        m_new = []
        for hh in range(2):
            if masked:
                sc[hh] = jnp.where(causal, sc[hh], NEG_INF)
            m_blk = jnp.max(sc[hh], axis=-1, keepdims=True)
            m_new.append(m_blk if state[hh] is None else jnp.maximum(state[hh][0], m_blk))
        pv = [jnp.dot(jnp.exp2(sc[hh] - m_new[hh]).astype(BF16), v_aug[hh], preferred_element_type=F32)
              for hh in range(2)]
        for hh in range(2):
            if state[hh] is None:
                acc = pv[hh]
            else:
                m, acc = state[hh]
                acc = jnp.exp2(m - m_new[hh]) * acc + pv[hh]
            state[hh] = (m_new[hh], acc)

    low = _lane_iota((tq, LANES)) < B_V
    num = jnp.where(low, state[0][1], state[1][1])
    den = pltpu.roll(jnp.where(low, state[1][1], state[0][1]), B_V, 1)
    o_ref[0] = num / den


def _mla_kernel(q_ref, k_ref, v_ref, o_ref):
    i = pl.program_id(2)
    for n_full in range(k_ref.shape[1] // q_ref.shape[1]):
        pl.when(i == n_full)(functools.partial(_mla_tile, q_ref, k_ref, v_ref, o_ref, n_full))


def _mla(qb, kb, vb, tq):
    b, s, _ = qb.shape
    pairs = B_HEADS // 2
    return pl.pallas_call(
        _mla_kernel,
        grid=(b, pairs, s // tq),
        in_specs=[
            pl.BlockSpec((1, tq, 2 * B_QK_PAD), lambda i, p, j: (i, j, p)),
            pl.BlockSpec((1, s, 2 * B_QK_PAD), lambda i, p, j: (i, 0, p)),
            pl.BlockSpec((1, s, LANES), lambda i, p, j: (i, 0, p)),
        ],
        out_specs=pl.BlockSpec((1, tq, LANES), lambda i, p, j: (i, j, p)),
        out_shape=jax.ShapeDtypeStruct((b, s, B_WIDTH), F32),
        compiler_params=pltpu.CompilerParams(
            dimension_semantics=("arbitrary", "arbitrary", "arbitrary"), vmem_limit_bytes=VMEM_LIMIT),
        name="mla",
    )(qb, kb, vb)


def _out_kernel(x_ref, ya_ref, yb_ref, yc_ref, mod_ref, gwa_ref, gwb_ref, wout_ref, n2w_ref,
                wgu_ref, wdn_ref, fw_ref, o_ref, *, final):
    tm = x_ref.shape[1]
    mod = mod_ref[0]
    ffn = wdn_ref.shape[0]

    def mix(r0):
        rs = slice(r0, r0 + OUT_ROWS)
        y = jnp.concatenate([
            _rms(ya_ref[0, rs, :], gwa_ref[...]).astype(BF16),
            _rms(yb_ref[0, rs, :], gwb_ref[...]).astype(BF16),
            yc_ref[0, rs, :],
        ], axis=-1)
        x1 = x_ref[0, rs, :] + mod[2:3] * jnp.dot(y, wout_ref[...], preferred_element_type=F32)
        h = (_rms(x1, n2w_ref[...]) * (1.0 + mod[4:5]) + mod[3:4]).astype(BF16)
        return x1, h

    starts = list(range(0, tm, OUT_ROWS))
    mixed = {starts[0]: mix(starts[0])}
    for idx, r0 in enumerate(starts):
        if idx + 1 < len(starts):
            mixed[starts[idx + 1]] = mix(starts[idx + 1])
        x1, h = mixed.pop(r0)
        acc = None
        for lo in range(0, ffn, FFN_CHUNK):
            g = jnp.dot(h, wgu_ref[:, lo:lo + FFN_CHUNK], preferred_element_type=F32)
            u = jnp.dot(h, wgu_ref[:, ffn + lo:ffn + lo + FFN_CHUNK], preferred_element_type=F32)
            a = (g * jax.nn.sigmoid(g) * u).astype(BF16)
            part = jnp.dot(a, wdn_ref[lo:lo + FFN_CHUNK, :], preferred_element_type=F32)
            acc = part if acc is None else acc + part
        x2 = x1 + mod[5:6] * acc
        if final:
            x2 = _rms(x2, fw_ref[...])
        o_ref[0, r0:r0 + OUT_ROWS, :] = x2


def _out_ffn(x, ya, yb, yc, mod, gwa, gwb, wout, n2w, wgu, wdn, fw, tm, layer, final):
    b, s, d = x.shape
    tok = lambda w: pl.BlockSpec((1, tm, w), lambda i, j: (i, j, 0))
    full = lambda a: pl.BlockSpec(a.shape, lambda i, j: (0,) * a.ndim, pipeline_mode=pl.Buffered(1))
    stacked = lambda a: pl.BlockSpec((None,) + a.shape[1:], lambda i, j: (layer,) + (0,) * (a.ndim - 1),
                                     pipeline_mode=pl.Buffered(1))
    return pl.pallas_call(
        functools.partial(_out_kernel, final=final),
        grid=(b, s // tm),
        in_specs=[
            tok(d), tok(A_WIDTH), tok(B_WIDTH), tok(C_WIDTH),
            pl.BlockSpec((1, N_MOD, d), lambda i, j: (i, 0, 0)),
            full(gwa), full(gwb), stacked(wout), full(n2w), stacked(wgu), stacked(wdn), full(fw),
        ],
        out_specs=tok(d),
        out_shape=jax.ShapeDtypeStruct((b, s, d), F32),
        compiler_params=pltpu.CompilerParams(
            dimension_semantics=("arbitrary", "arbitrary"), vmem_limit_bytes=VMEM_LIMIT),
        name="out_ffn",
    )(x, ya, yb, yc, mod, gwa, gwb, wout, n2w, wgu, wdn, fw)


def _in_col_index():
    src_aq, src_ak, src_av = 0, A_WIDTH, A_WIDTH + A_KV_WIDTH
    src_bcq = src_av + A_KV_WIDTH
    src_bckv = src_bcq + B_Q_RANK
    src_bkr = src_bckv + B_KV_RANK
    src_cu = src_bkr + B_ROPE
    src_cv = src_cu + C_WIDTH
    zero = src_cv + C_WIDTH
    idx = []
    for hh in A_HEAD_ORDER:
        idx += list(range(src_aq + hh * HEAD_DIM, src_aq + (hh + 1) * HEAD_DIM))
    idx += list(range(src_ak, src_bkr))
    idx += list(range(src_cu, zero))
    idx += [zero] * B_NOPE + list(range(src_bkr, src_bkr + B_ROPE)) + [zero] * (LANES - B_NOPE - B_ROPE)
    assert len(idx) == IN_COLS_PAD
    return np.asarray(idx, np.int32)


def _take_static(a, idx, axis):
    n = a.shape[axis]
    idx = [int(t) for t in idx]
    parts, start = [], 0
    for pos in range(1, len(idx) + 1):
        if pos < len(idx):
            prev, cur = idx[pos - 1], idx[pos]
            same_run = (prev == n and cur == n) or (prev != n and cur != n and cur == prev + 1)
        else:
            same_run = False
        if not same_run:
            first, count = idx[start], pos - start
            if first == n:
                shape = a.shape[:axis] + (count,) + a.shape[axis + 1:]
                parts.append(jnp.zeros(shape, a.dtype))
            else:
                parts.append(lax.slice_in_dim(a, first, first + count, axis=axis))
            start = pos
    return jnp.concatenate(parts, axis=axis)


def _rope_tables(positions):
    pos = positions.astype(F32)[..., None]

    def tables(dim):
        inv = 1.0 / (ROPE_THETA ** (jnp.arange(0, dim, 2, dtype=F32) / dim))
        ang = pos * inv
        return jnp.cos(ang), jnp.sin(ang)

    cos_a, sin_a = tables(HEAD_DIM)
    ca = jnp.concatenate([cos_a] * 4, axis=-1)
    sa = jnp.concatenate([-sin_a, sin_a] * 2, axis=-1)
    cos_b, sin_b = tables(B_ROPE)
    ones = jnp.ones(pos.shape[:-1] + (B_NOPE,), F32)
    zeros = jnp.zeros(pos.shape[:-1] + (LANES - B_NOPE - B_ROPE,), F32)
    cb = jnp.concatenate([ones, cos_b, cos_b, zeros], axis=-1)
    sb = jnp.concatenate([0.0 * ones, -sin_b, sin_b, zeros], axis=-1)
    return ca, sa, cb, sb


def kernel(x, c, positions, ada_w, ada_b, norm1_w, w_in, a_sinks, b_q_norm_w, b_w_uq, b_kv_norm_w, b_w_ukv,
           c_ln_w, c_ln_b, c_w_s, c_b_s, out_norm_w, w_out, norm2_w, w_gate_up, w_down, final_norm_w):
    depth = w_in.shape[0]
    b, s, d = x.shape
    ffn = w_down.shape[1]
    tm_in, tm_out, tq = min(IN_TILE, s), min(OUT_TILE, s), min(ATTN_TILE, s)

    mod = _modulation(c, ada_w, ada_b).reshape(depth, b, N_MOD, d)
    tabs = _rope_tables(positions)

    col_scale = np.ones((IN_COLS_PAD,), np.float32)
    col_scale[OFF_AQ:OFF_AQ + A_WIDTH] = HEAD_DIM ** -0.5
    w_in_p = (_take_static(w_in, _in_col_index(), 2) * col_scale).astype(BF16)

    w_uq = b_w_uq.reshape(depth, B_Q_RANK, B_HEADS, B_NOPE + B_ROPE)
    w_uq = jnp.pad(w_uq, ((0, 0), (0, 0), (0, 0), (0, B_QK_PAD - B_NOPE - B_ROPE)))
    w_uq = w_uq.reshape(depth, B_Q_RANK, B_HEADS * B_QK_PAD).astype(BF16)
    w_ukv = b_w_ukv.reshape(depth, B_KV_RANK, B_HEADS, B_NOPE + B_V)
    w_uk = jnp.pad(w_ukv[..., :B_NOPE], ((0, 0), (0, 0), (0, 0), (0, B_QK_PAD - B_NOPE)))
    w_ukv = jnp.concatenate([w_uk.reshape(depth, B_KV_RANK, B_HEADS * B_QK_PAD),
                             w_ukv[..., B_NOPE:].reshape(depth, B_KV_RANK, B_WIDTH)], axis=-1).astype(BF16)

    bs_full = jnp.repeat(jnp.swapaxes(c_b_s, 1, 2), C_GROUP_DIM, axis=-1)
    order = np.asarray(A_HEAD_ORDER)
    sink_tab = jnp.broadcast_to((a_sinks * LOG2E)[:, order, None], (depth, A_Q_HEADS, LANES))
    sink_tab = jnp.pad(sink_tab, ((0, 0), (0, 8 - A_Q_HEADS), (0, 0)))

    a_perm = (order[:, None] * HEAD_DIM + np.arange(HEAD_DIM)[None, :]).reshape(-1)
    out_perm = np.concatenate([a_perm, np.arange(A_WIDTH, d)])
    gw = _take_static(out_norm_w, out_perm, 1)
    w_out_p = _take_static(w_out, out_perm, 1).astype(BF16)

    assert ffn % FFN_CHUNK == 0
    w_gu = w_gate_up.astype(BF16)
    w_dn = w_down.astype(BF16)

    row = lambda a: a.reshape(1, -1)
    for l in range(depth):
        qa, ka, va, qb, kb, vb, yc = _in_proj(
            x, mod[l], row(norm1_w[l]), w_in_p, w_uq, w_ukv, row(b_q_norm_w[l]), row(b_kv_norm_w[l]),
            tabs, row(c_ln_w[l]), row(c_ln_b[l]), c_w_s, bs_full, row(gw[l, A_WIDTH + B_WIDTH:]), tm_in, l)
        ya = _swa(qa, ka, va, sink_tab[l], tq)
        yb = _mla(qb, kb, vb, tq)
        x = _out_ffn(x, ya, yb, yc, mod[l], row(gw[l, :A_WIDTH]), row(gw[l, A_WIDTH:A_WIDTH + B_WIDTH]),
                     w_out_p, row(norm2_w[l]), w_gu, w_dn, row(final_norm_w), tm_out, l, l == depth - 1)
    return x
```

```python
import functools

import numpy as np
import jax
import jax.numpy as jnp
from jax import lax
from jax.experimental import pallas as pl
from jax.experimental.pallas import tpu as pltpu

F32 = jnp.float32
BF16 = jnp.bfloat16

LANES = 128
HEAD_DIM = 64
ROPE_THETA = 10000.0
NORM_EPS = 1e-6
NEG_INF = -1e30
LOG2E = float(np.log2(np.e))
BLOCK = 128

A_Q_HEADS = 6
A_KV_HEADS = 2
A_WIDTH = A_Q_HEADS * HEAD_DIM
A_KV_WIDTH = A_KV_HEADS * HEAD_DIM
A_HEAD_ORDER = (0, 3, 1, 4, 2, 5)

B_HEADS = 6
B_Q_RANK = 384
B_KV_RANK = 256
B_NOPE = 64
B_ROPE = 32
B_V = 64
B_WIDTH = B_HEADS * B_V
B_QK_PAD = LANES

C_GROUPS = 4
C_GROUP_DIM = 64
C_WIDTH = C_GROUPS * C_GROUP_DIM

N_MOD = 6

OFF_AQ = 0
OFF_AK = OFF_AQ + A_WIDTH
OFF_AV = OFF_AK + A_KV_WIDTH
OFF_BCQ = OFF_AV + A_KV_WIDTH
OFF_BCKV = OFF_BCQ + B_Q_RANK
OFF_CU = OFF_BCKV + B_KV_RANK
OFF_CV = OFF_CU + C_WIDTH
OFF_BKR = OFF_CV + C_WIDTH
IN_COLS_PAD = OFF_BKR + LANES

VMEM_LIMIT = 56 * 1024 * 1024
MLA_KV_SEG = 2048
IN_ROWS = 256
IN_TILE = 1024
OUT_TILE = 1024
OUT_ROWS = 512
ATTN_TILE = 512
FFN_CHUNK = 256


def _rms(x, w):
    return x * lax.rsqrt(jnp.mean(x * x, axis=-1, keepdims=True) + NORM_EPS) * w


def _gelu(x):
    return 0.5 * x * (1.0 + lax.erf(x * float(np.sqrt(0.5))))


def _lane_iota(shape):
    return lax.broadcasted_iota(jnp.int32, shape, len(shape) - 1)


def _mod_kernel(c_ref, w_ref, b_ref, o_ref):
    c = c_ref[...]
    act = (c * jax.nn.sigmoid(c)).astype(BF16)
    o_ref[0] = jnp.dot(act, w_ref[0].astype(BF16), preferred_element_type=F32) + b_ref[0]


def _modulation(c, ada_w, ada_b):
    depth, d, n = ada_w.shape
    b = c.shape[0]
    tn = 1536
    return pl.pallas_call(
        _mod_kernel,
        grid=(depth, n // tn),
        in_specs=[
            pl.BlockSpec((b, d), lambda l, j: (0, 0)),
            pl.BlockSpec((1, d, tn), lambda l, j: (l, 0, j)),
            pl.BlockSpec((1, 1, tn), lambda l, j: (l, 0, j)),
        ],
        out_specs=pl.BlockSpec((1, b, tn), lambda l, j: (l, 0, j)),
        out_shape=jax.ShapeDtypeStruct((depth, b, n), F32),
        compiler_params=pltpu.CompilerParams(
            dimension_semantics=("arbitrary", "arbitrary"), vmem_limit_bytes=VMEM_LIMIT),
        name="modulation",
    )(c, ada_w, ada_b.reshape(depth, 1, n))


def _swa_scores(q_pairs, kcat):
    low = _lane_iota((BLOCK, LANES)) < HEAD_DIM
    out = []
    for half in range(2):
        keep = low if half == 0 else jnp.logical_not(low)
        qs = jnp.concatenate([jnp.where(keep, qp, jnp.zeros_like(qp)) for qp in q_pairs], axis=0)
        out.append(lax.dot_general(qs, kcat, (((1,), (1,)), ((), ())), preferred_element_type=F32))
    return out


def _in_kernel(x_ref, mod_ref, n1w_ref, win_ref, wuq_ref, wukv_ref, qnw_ref, kvnw_ref,
               ca_ref, sa_ref, cb_ref, sb_ref, lnw_ref, lnb_ref, ws_ref, bs_ref, gwc_ref, sink_ref,
               ya_ref, qb_ref, kb_ref, vb_ref, yc_ref, kprev_ref, vprev_ref, band_ref):
    tm = x_ref.shape[1]
    mod = mod_ref[0]
    n1w = n1w_ref[...]
    lane = _lane_iota((IN_ROWS, LANES))
    a_first = (lane % HEAD_DIM) < (HEAD_DIM // 2)
    b_first = lane < (B_NOPE + B_ROPE // 2)

    first_tile = pl.program_id(1) == 0

    @pl.when(first_tile)
    def _():
        kprev_ref[...] = jnp.zeros_like(kprev_ref)
        vprev_ref[...] = jnp.zeros_like(vprev_ref)

    pairs = A_WIDTH // LANES
    qi = lax.broadcasted_iota(jnp.int32, (pairs * BLOCK, 2 * BLOCK), 0) % BLOCK
    kj = lax.broadcasted_iota(jnp.int32, (pairs * BLOCK, 2 * BLOCK), 1)
    rel = qi + BLOCK - kj
    band = (rel >= 0) & (rel < BLOCK)
    band_ref[...] = jnp.where(band, 0.0, NEG_INF)
    sinks = [jnp.concatenate([jnp.broadcast_to(sink_ref[2 * p + half:2 * p + half + 1, :], (BLOCK, LANES))
                              for p in range(pairs)], axis=0) for half in range(2)]
    ones_blk = jnp.ones((2 * BLOCK, LANES), BF16)
    low_a = _lane_iota((BLOCK, LANES)) < HEAD_DIM
    k_last = kprev_ref[...]
    v_last = vprev_ref[...]
    b_scale = float((B_NOPE + B_ROPE) ** -0.5) * LOG2E
    row = lax.broadcasted_iota(jnp.int32, (BLOCK, BLOCK), 0)
    col = lax.broadcasted_iota(jnp.int32, (BLOCK, BLOCK), 1)
    w_s = [jnp.where(col <= row, ws_ref[g], 0.0).astype(BF16) for g in range(C_GROUPS)]
    low = _lane_iota((BLOCK, LANES)) < C_GROUP_DIM
    bias = bs_ref[...]
    gwc = gwc_ref[...]

    def project(r0):
        x = x_ref[0, r0:r0 + IN_ROWS, :]
        h = (_rms(x, n1w) * (1.0 + mod[1:2]) + mod[0:1]).astype(BF16)
        return jnp.dot(h, win_ref[...], preferred_element_type=F32)

    starts = list(range(0, tm, IN_ROWS))
    projs = {starts[0]: project(starts[0])}
    for idx, r0 in enumerate(starts):
        if idx + 1 < len(starts):
            projs[starts[idx + 1]] = project(starts[idx + 1])
        proj = projs.pop(r0)
        rs = slice(r0, r0 + IN_ROWS)

        ca = ca_ref[0, rs, :]
        sa = sa_ref[0, rs, :]

        def rope_a(t):
            sw = jnp.where(a_first, pltpu.roll(t, LANES - HEAD_DIM // 2, 1), pltpu.roll(t, HEAD_DIM // 2, 1))
            return t * ca + sw * sa

        q_pairs = [(rope_a(proj[:, OFF_AQ + p * LANES:OFF_AQ + (p + 1) * LANES]) * LOG2E).astype(BF16)
                   for p in range(pairs)]
        ka = rope_a(proj[:, OFF_AK:OFF_AK + A_KV_WIDTH]).astype(BF16)
        va = proj[:, OFF_AV:OFF_AV + A_KV_WIDTH].astype(BF16)

        n_blk = IN_ROWS // BLOCK
        kcats, vcats, scs = [], [], []
        for nb in range(n_blk):
            if nb == 0:
                kcats.append(jnp.concatenate([k_last, ka[0:BLOCK]], axis=0))
                vcats.append(jnp.concatenate([v_last, va[0:BLOCK]], axis=0))
            else:
                kcats.append(ka[(nb - 1) * BLOCK:(nb + 1) * BLOCK])
                vcats.append(va[(nb - 1) * BLOCK:(nb + 1) * BLOCK])
            scs.append(_swa_scores([qp[nb * BLOCK:(nb + 1) * BLOCK] for qp in q_pairs], kcats[nb]))
        k_last = ka[IN_ROWS - BLOCK:]
        v_last = va[IN_ROWS - BLOCK:]

        cb = cb_ref[0, rs, :]
        sb = sb_ref[0, rs, :]

        def rope_b(t):
            sw = jnp.where(b_first, pltpu.roll(t, LANES - B_ROPE // 2, 1), pltpu.roll(t, B_ROPE // 2, 1))
            return t * cb + sw * sb

        cq = _rms(proj[:, OFF_BCQ:OFF_BCQ + B_Q_RANK], qnw_ref[...]).astype(BF16)
        qb = jnp.dot(cq, wuq_ref[...], preferred_element_type=F32)
        ckv = _rms(proj[:, OFF_BCKV:OFF_BCKV + B_KV_RANK], kvnw_ref[...]).astype(BF16)
        kv = jnp.dot(ckv, wukv_ref[...], preferred_element_type=F32)

        ms = []
        for nb in range(n_blk):
            for half in range(2):
                if idx == 0 and nb == 0:
                    seen = band & (kj >= jnp.where(first_tile, BLOCK, 0))
                    scs[nb][half] = jnp.where(seen, scs[nb][half], NEG_INF)
                else:
                    scs[nb][half] = scs[nb][half] + band_ref[...]
            ms.append([jnp.maximum(jnp.max(scs[nb][half], axis=-1, keepdims=True), sinks[half])
                       for half in range(2)])

        kr = rope_b(proj[:, OFF_BKR:OFF_BKR + LANES])
        for hh in range(B_HEADS):
            sl = slice(hh * LANES, (hh + 1) * LANES)
            qb_ref[0, rs, sl] = (rope_b(qb[:, sl]) * b_scale).astype(BF16)
            kb_ref[0, rs, sl] = (kv[:, sl] + kr).astype(BF16)
        vb_ref[0, rs, :] = kv[:, B_HEADS * LANES:].astype(BF16)

        pvs = []
        for nb in range(n_blk):
            pvs.append([])
            for half in range(2):
                e = jnp.concatenate([jnp.exp2(scs[nb][half][:, t * LANES:(t + 1) * LANES] - ms[nb][half])
                                     for t in range(2 * BLOCK // LANES)], axis=1).astype(BF16)
                pvs[nb].append(jnp.dot(e, jnp.concatenate([vcats[nb], ones_blk], axis=1),
                                       preferred_element_type=F32))

        u = _gelu(proj[:, OFF_CU:OFF_CU + C_WIDTH])
        v = _gelu(proj[:, OFF_CV:OFF_CV + C_WIDTH])
        mu = jnp.mean(v, axis=-1, keepdims=True)
        vc = v - mu
        var = jnp.mean(vc * vc, axis=-1, keepdims=True)
        v = (vc * lax.rsqrt(var + NORM_EPS) * lnw_ref[...] + lnb_ref[...]).astype(BF16)
        for c in range(IN_ROWS // BLOCK):
            rows = slice(c * BLOCK, (c + 1) * BLOCK)
            parts = []
            for p in range(C_WIDTH // LANES):
                vp = v[rows, p * LANES:(p + 1) * LANES]
                r_lo = jnp.dot(w_s[2 * p], vp, preferred_element_type=F32)
                r_hi = jnp.dot(w_s[2 * p + 1], vp, preferred_element_type=F32)
                parts.append(jnp.where(low, r_lo, r_hi))
            mixed = jnp.concatenate(parts, axis=-1) + bias
            yc = u[rows] * mixed
            yc_ref[0, r0 + c * BLOCK:r0 + (c + 1) * BLOCK, :] = _rms(yc, gwc).astype(BF16)

        for nb in range(n_blk):
            outs = [pvs[nb][half][:, :LANES] / (pvs[nb][half][:, LANES:] + jnp.exp2(sinks[half] - ms[nb][half]))
                    for half in range(2)]
            for p in range(pairs):
                blk = slice(p * BLOCK, (p + 1) * BLOCK)
                ya_ref[0, r0 + nb * BLOCK:r0 + (nb + 1) * BLOCK, p * LANES:(p + 1) * LANES] = jnp.where(
                    low_a, outs[0][blk], outs[1][blk])

    kprev_ref[...] = k_last
    vprev_ref[...] = v_last


def _in_proj(x, mod, n1w, win, wuq, wukv, qnw, kvnw, tabs, lnw, lnb, ws, bs_full, gwc, sink_tab, tm, layer):
    b, s, d = x.shape
    tok = lambda w: pl.BlockSpec((1, tm, w), lambda i, j: (i, j, 0))
    full = lambda a: pl.BlockSpec(a.shape, lambda i, j: (0,) * a.ndim)
    stacked = lambda a: pl.BlockSpec((None,) + a.shape[1:], lambda i, j: (layer,) + (0,) * (a.ndim - 1))
    ca, sa, cb, sb = tabs
    out_widths = (A_WIDTH, B_HEADS * B_QK_PAD, B_HEADS * B_QK_PAD, B_WIDTH, C_WIDTH)
    out_specs = [tok(w) for w in out_widths]
    out_shape = [jax.ShapeDtypeStruct((b, s, w), F32 if n == 0 else BF16) for n, w in enumerate(out_widths)]
    return pl.pallas_call(
        _in_kernel,
        grid=(b, s // tm),
        in_specs=[
            tok(d),
            pl.BlockSpec((1, N_MOD, d), lambda i, j: (i, 0, 0)),
            full(n1w), stacked(win), stacked(wuq), stacked(wukv), full(qnw), full(kvnw),
            tok(LANES), tok(LANES), tok(LANES), tok(LANES),
            full(lnw), full(lnb), stacked(ws), stacked(bs_full), full(gwc), stacked(sink_tab),
        ],
        out_specs=out_specs,
        out_shape=out_shape,
        scratch_shapes=[pltpu.VMEM((BLOCK, A_KV_WIDTH), BF16), pltpu.VMEM((BLOCK, A_KV_WIDTH), BF16),
                        pltpu.VMEM((A_WIDTH // LANES * BLOCK, 2 * BLOCK), F32)],
        compiler_params=pltpu.CompilerParams(
            dimension_semantics=("arbitrary", "arbitrary"), vmem_limit_bytes=VMEM_LIMIT),
        name="in_proj",
    )(x, mod, n1w, win, wuq, wukv, qnw, kvnw, ca, sa, cb, sb, lnw, lnb, ws, bs_full, gwc, sink_tab)


def _mla_tile(q_ref, k_ref, v_ref, o_ref, n_full):
    tq = q_ref.shape[1]
    full_keys = n_full * tq
    segments = [(lo, min(MLA_KV_SEG, full_keys - lo), False) for lo in range(0, full_keys, MLA_KV_SEG)]
    segments.append((full_keys, tq, True))
    qi = lax.broadcasted_iota(jnp.int32, (tq, tq), 0)
    kj = lax.broadcasted_iota(jnp.int32, (tq, tq), 1)
    causal = kj <= qi

    nt = (((1,), (1,)), ((), ()))

    state = [None, None]
    for lo, width, masked in segments:
        v = v_ref[0, lo:lo + width, :]
        low_v = _lane_iota(v.shape) < B_V
        one = jnp.ones_like(v)
        v_aug = (jnp.where(low_v, v, one), jnp.where(low_v, one, v))
        sc = []
        for hh in range(2):
            q = q_ref[0, :, hh * LANES:(hh + 1) * LANES]
            k = k_ref[0, lo:lo + width, hh * LANES:(hh + 1) * LANES]
            sc.append(lax.dot_general(q, k, nt, preferred_element_type=F32))
        m_new = []
        for hh in range(2):
            if masked:
                sc[hh] = jnp.where(causal, sc[hh], NEG_INF)
            m_blk = jnp.max(sc[hh], axis=-1, keepdims=True)
            m_new.append(m_blk if state[hh] is None else jnp.maximum(state[hh][0], m_blk))
        pv = [jnp.dot(jnp.exp2(sc[hh] - m_new[hh]).astype(BF16), v_aug[hh], preferred_element_type=F32)
              for hh in range(2)]
        for hh in range(2):
            if state[hh] is None:
                acc = pv[hh]
            else:
                m, acc = state[hh]
                acc = jnp.exp2(m - m_new[hh]) * acc + pv[hh]
            state[hh] = (m_new[hh], acc)

    low = _lane_iota((tq, LANES)) < B_V
    num = jnp.where(low, state[0][1], state[1][1])
    den = pltpu.roll(jnp.where(low, state[1][1], state[0][1]), B_V, 1)
    o_ref[0] = num / den


def _mla_kernel(q_ref, k_ref, v_ref, o_ref):
    i = pl.program_id(2)
    for n_full in range(k_ref.shape[1] // q_ref.shape[1]):
        pl.when(i == n_full)(functools.partial(_mla_tile, q_ref, k_ref, v_ref, o_ref, n_full))


def _mla(qb, kb, vb, tq):
    b, s, _ = qb.shape
    pairs = B_HEADS // 2
    return pl.pallas_call(
        _mla_kernel,
        grid=(b, pairs, s // tq),
        in_specs=[
            pl.BlockSpec((1, tq, 2 * B_QK_PAD), lambda i, p, j: (i, j, p)),
            pl.BlockSpec((1, s, 2 * B_QK_PAD), lambda i, p, j: (i, 0, p)),
            pl.BlockSpec((1, s, LANES), lambda i, p, j: (i, 0, p)),
        ],
        out_specs=pl.BlockSpec((1, tq, LANES), lambda i, p, j: (i, j, p)),
        out_shape=jax.ShapeDtypeStruct((b, s, B_WIDTH), F32),
        compiler_params=pltpu.CompilerParams(
            dimension_semantics=("arbitrary", "arbitrary", "arbitrary"), vmem_limit_bytes=VMEM_LIMIT),
        name="mla",
    )(qb, kb, vb)


def _out_kernel(x_ref, ya_ref, yb_ref, yc_ref, mod_ref, gwa_ref, gwb_ref, wout_ref, n2w_ref,
                wgu_ref, wdn_ref, fw_ref, o_ref, *, final):
    tm = x_ref.shape[1]
    mod = mod_ref[0]
    ffn = wdn_ref.shape[0]

    def mix(r0):
        rs = slice(r0, r0 + OUT_ROWS)
        y = jnp.concatenate([
            _rms(ya_ref[0, rs, :], gwa_ref[...]).astype(BF16),
            _rms(yb_ref[0, rs, :], gwb_ref[...]).astype(BF16),
            yc_ref[0, rs, :],
        ], axis=-1)
        x1 = x_ref[0, rs, :] + mod[2:3] * jnp.dot(y, wout_ref[...], preferred_element_type=F32)
        h = (_rms(x1, n2w_ref[...]) * (1.0 + mod[4:5]) + mod[3:4]).astype(BF16)
        return x1, h

    starts = list(range(0, tm, OUT_ROWS))
    mixed = {starts[0]: mix(starts[0])}
    for idx, r0 in enumerate(starts):
        if idx + 1 < len(starts):
            mixed[starts[idx + 1]] = mix(starts[idx + 1])
        x1, h = mixed.pop(r0)
        acc = None
        for lo in range(0, ffn, FFN_CHUNK):
            g = jnp.dot(h, wgu_ref[:, lo:lo + FFN_CHUNK], preferred_element_type=F32)
            u = jnp.dot(h, wgu_ref[:, ffn + lo:ffn + lo + FFN_CHUNK], preferred_element_type=F32)
            a = (g * jax.nn.sigmoid(g) * u).astype(BF16)
            part = jnp.dot(a, wdn_ref[lo:lo + FFN_CHUNK, :], preferred_element_type=F32)
            acc = part if acc is None else acc + part
        x2 = x1 + mod[5:6] * acc
        if final:
            x2 = _rms(x2, fw_ref[...])
        o_ref[0, r0:r0 + OUT_ROWS, :] = x2


def _out_ffn(x, ya, yb, yc, mod, gwa, gwb, wout, n2w, wgu, wdn, fw, tm, layer, final):
    b, s, d = x.shape
    tok = lambda w: pl.BlockSpec((1, tm, w), lambda i, j: (i, j, 0))
    full = lambda a: pl.BlockSpec(a.shape, lambda i, j: (0,) * a.ndim, pipeline_mode=pl.Buffered(1))
    stacked = lambda a: pl.BlockSpec((None,) + a.shape[1:], lambda i, j: (layer,) + (0,) * (a.ndim - 1),
                                     pipeline_mode=pl.Buffered(1))
    return pl.pallas_call(
        functools.partial(_out_kernel, final=final),
        grid=(b, s // tm),
        in_specs=[
            tok(d), tok(A_WIDTH), tok(B_WIDTH), tok(C_WIDTH),
            pl.BlockSpec((1, N_MOD, d), lambda i, j: (i, 0, 0)),
            full(gwa), full(gwb), stacked(wout), full(n2w), stacked(wgu), stacked(wdn), full(fw),
        ],
        out_specs=tok(d),
        out_shape=jax.ShapeDtypeStruct((b, s, d), F32),
        compiler_params=pltpu.CompilerParams(
            dimension_semantics=("arbitrary", "arbitrary"), vmem_limit_bytes=VMEM_LIMIT),
        name="out_ffn",
    )(x, ya, yb, yc, mod, gwa, gwb, wout, n2w, wgu, wdn, fw)


def _in_col_index():
    src_aq, src_ak, src_av = 0, A_WIDTH, A_WIDTH + A_KV_WIDTH
    src_bcq = src_av + A_KV_WIDTH
    src_bckv = src_bcq + B_Q_RANK
    src_bkr = src_bckv + B_KV_RANK
    src_cu = src_bkr + B_ROPE
    src_cv = src_cu + C_WIDTH
    zero = src_cv + C_WIDTH
    idx = []
    for hh in A_HEAD_ORDER:
        idx += list(range(src_aq + hh * HEAD_DIM, src_aq + (hh + 1) * HEAD_DIM))
    idx += list(range(src_ak, src_bkr))
    idx += list(range(src_cu, zero))
    idx += [zero] * B_NOPE + list(range(src_bkr, src_bkr + B_ROPE)) + [zero] * (LANES - B_NOPE - B_ROPE)
    assert len(idx) == IN_COLS_PAD
    return np.asarray(idx, np.int32)


def _take_static(a, idx, axis):
    n = a.shape[axis]
    idx = [int(t) for t in idx]
    parts, start = [], 0
    for pos in range(1, len(idx) + 1):
        if pos < len(idx):
            prev, cur = idx[pos - 1], idx[pos]
            same_run = (prev == n and cur == n) or (prev != n and cur != n and cur == prev + 1)
        else:
            same_run = False
        if not same_run:
            first, count = idx[start], pos - start
            if first == n:
                shape = a.shape[:axis] + (count,) + a.shape[axis + 1:]
                parts.append(jnp.zeros(shape, a.dtype))
            else:
                parts.append(lax.slice_in_dim(a, first, first + count, axis=axis))
            start = pos
    return jnp.concatenate(parts, axis=axis)


def _rope_tables(positions):
    pos = positions.astype(F32)[..., None]

    def tables(dim):
        inv = 1.0 / (ROPE_THETA ** (jnp.arange(0, dim, 2, dtype=F32) / dim))
        ang = pos * inv
        return jnp.cos(ang), jnp.sin(ang)

    cos_a, sin_a = tables(HEAD_DIM)
    ca = jnp.concatenate([cos_a] * 4, axis=-1)
    sa = jnp.concatenate([-sin_a, sin_a] * 2, axis=-1)
    cos_b, sin_b = tables(B_ROPE)
    ones = jnp.ones(pos.shape[:-1] + (B_NOPE,), F32)
    zeros = jnp.zeros(pos.shape[:-1] + (LANES - B_NOPE - B_ROPE,), F32)
    cb = jnp.concatenate([ones, cos_b, cos_b, zeros], axis=-1)
    sb = jnp.concatenate([0.0 * ones, -sin_b, sin_b, zeros], axis=-1)
    return ca, sa, cb, sb


def kernel(x, c, positions, ada_w, ada_b, norm1_w, w_in, a_sinks, b_q_norm_w, b_w_uq, b_kv_norm_w, b_w_ukv,
           c_ln_w, c_ln_b, c_w_s, c_b_s, out_norm_w, w_out, norm2_w, w_gate_up, w_down, final_norm_w):
    depth = w_in.shape[0]
    b, s, d = x.shape
    ffn = w_down.shape[1]
    tm_in, tm_out, tq = min(IN_TILE, s), min(OUT_TILE, s), min(ATTN_TILE, s)

    mod = _modulation(c, ada_w, ada_b).reshape(depth, b, N_MOD, d)
    tabs = _rope_tables(positions)

    col_scale = np.ones((IN_COLS_PAD,), np.float32)
    col_scale[OFF_AQ:OFF_AQ + A_WIDTH] = HEAD_DIM ** -0.5
    w_in_p = (_take_static(w_in, _in_col_index(), 2) * col_scale).astype(BF16)

    w_uq = b_w_uq.reshape(depth, B_Q_RANK, B_HEADS, B_NOPE + B_ROPE)
    w_uq = jnp.pad(w_uq, ((0, 0), (0, 0), (0, 0), (0, B_QK_PAD - B_NOPE - B_ROPE)))
    w_uq = w_uq.reshape(depth, B_Q_RANK, B_HEADS * B_QK_PAD).astype(BF16)
    w_ukv = b_w_ukv.reshape(depth, B_KV_RANK, B_HEADS, B_NOPE + B_V)
    w_uk = jnp.pad(w_ukv[..., :B_NOPE], ((0, 0), (0, 0), (0, 0), (0, B_QK_PAD - B_NOPE)))
    w_ukv = jnp.concatenate([w_uk.reshape(depth, B_KV_RANK, B_HEADS * B_QK_PAD),
                             w_ukv[..., B_NOPE:].reshape(depth, B_KV_RANK, B_WIDTH)], axis=-1).astype(BF16)

    bs_full = jnp.repeat(jnp.swapaxes(c_b_s, 1, 2), C_GROUP_DIM, axis=-1)
    order = np.asarray(A_HEAD_ORDER)
    sink_tab = jnp.broadcast_to((a_sinks * LOG2E)[:, order, None], (depth, A_Q_HEADS, LANES))
    sink_tab = jnp.pad(sink_tab, ((0, 0), (0, 8 - A_Q_HEADS), (0, 0)))

    a_perm = (order[:, None] * HEAD_DIM + np.arange(HEAD_DIM)[None, :]).reshape(-1)
    out_perm = np.concatenate([a_perm, np.arange(A_WIDTH, d)])
    gw = _take_static(out_norm_w, out_perm, 1)
    w_out_p = _take_static(w_out, out_perm, 1).astype(BF16)

    assert ffn % FFN_CHUNK == 0
    w_gu = w_gate_up.astype(BF16)
    w_dn = w_down.astype(BF16)

    row = lambda a: a.reshape(1, -1)
    for l in range(depth):
        ya, qb, kb, vb, yc = _in_proj(
            x, mod[l], row(norm1_w[l]), w_in_p, w_uq, w_ukv, row(b_q_norm_w[l]), row(b_kv_norm_w[l]),
            tabs, row(c_ln_w[l]), row(c_ln_b[l]), c_w_s, bs_full, row(gw[l, A_WIDTH + B_WIDTH:]), sink_tab,
            tm_in, l)
        yb = _mla(qb, kb, vb, tq)
        x = _out_ffn(x, ya, yb, yc, mod[l], row(gw[l, :A_WIDTH]), row(gw[l, A_WIDTH:A_WIDTH + B_WIDTH]),
                     w_out_p, row(norm2_w[l]), w_gu, w_dn, row(final_norm_w), tm_out, l, l == depth - 1)
    return x
```

```python
import functools

import numpy as np
import jax
import jax.numpy as jnp
from jax import lax
from jax.experimental import pallas as pl
from jax.experimental.pallas import tpu as pltpu

F32 = jnp.float32
BF16 = jnp.bfloat16

LANES = 128
HEAD_DIM = 64
ROPE_THETA = 10000.0
NORM_EPS = 1e-6
NEG_INF = -1e30
LOG2E = float(np.log2(np.e))
BLOCK = 128

A_Q_HEADS = 6
A_KV_HEADS = 2
A_WIDTH = A_Q_HEADS * HEAD_DIM
A_KV_WIDTH = A_KV_HEADS * HEAD_DIM
A_HEAD_ORDER = (0, 3, 1, 4, 2, 5)

B_HEADS = 6
B_Q_RANK = 384
B_KV_RANK = 256
B_NOPE = 64
B_ROPE = 32
B_V = 64
B_WIDTH = B_HEADS * B_V
B_QK_PAD = LANES

C_GROUPS = 4
C_GROUP_DIM = 64
C_WIDTH = C_GROUPS * C_GROUP_DIM

N_MOD = 6

OFF_AQ = 0
OFF_AK = OFF_AQ + A_WIDTH
OFF_AV = OFF_AK + A_KV_WIDTH
OFF_BCQ = OFF_AV + A_KV_WIDTH
OFF_BCKV = OFF_BCQ + B_Q_RANK
OFF_CU = OFF_BCKV + B_KV_RANK
OFF_CV = OFF_CU + C_WIDTH
OFF_BKR = OFF_CV + C_WIDTH
IN_COLS_PAD = OFF_BKR + LANES

VMEM_LIMIT = 56 * 1024 * 1024
MLA_KV_SEG = 2048
IN_ROWS = 256
IN_TILE = 1024
OUT_TILE = 1024
OUT_ROWS = 512
ATTN_TILE = 2048
MLA_ROWS = 512
FFN_CHUNK = 256


def _rms(x, w):
    return x * lax.rsqrt(jnp.mean(x * x, axis=-1, keepdims=True) + NORM_EPS) * w


def _gelu(x):
    return 0.5 * x * (1.0 + lax.erf(x * float(np.sqrt(0.5))))


def _lane_iota(shape):
    return lax.broadcasted_iota(jnp.int32, shape, len(shape) - 1)


def _mod_kernel(c_ref, w_ref, b_ref, o_ref):
    c = c_ref[...]
    act = (c * jax.nn.sigmoid(c)).astype(BF16)
    o_ref[0] = jnp.dot(act, w_ref[0].astype(BF16), preferred_element_type=F32) + b_ref[0]


def _modulation(c, ada_w, ada_b):
    depth, d, n = ada_w.shape
    b = c.shape[0]
    tn = 1536
    return pl.pallas_call(
        _mod_kernel,
        grid=(depth, n // tn),
        in_specs=[
            pl.BlockSpec((b, d), lambda l, j: (0, 0)),
            pl.BlockSpec((1, d, tn), lambda l, j: (l, 0, j)),
            pl.BlockSpec((1, 1, tn), lambda l, j: (l, 0, j)),
        ],
        out_specs=pl.BlockSpec((1, b, tn), lambda l, j: (l, 0, j)),
        out_shape=jax.ShapeDtypeStruct((depth, b, n), F32),
        compiler_params=pltpu.CompilerParams(
            dimension_semantics=("arbitrary", "arbitrary"), vmem_limit_bytes=VMEM_LIMIT),
        name="modulation",
    )(c, ada_w, ada_b.reshape(depth, 1, n))


def _swa_scores(q_pairs, kcat):
    low = _lane_iota((BLOCK, LANES)) < HEAD_DIM
    out = []
    for half in range(2):
        keep = low if half == 0 else jnp.logical_not(low)
        qs = jnp.concatenate([jnp.where(keep, qp, jnp.zeros_like(qp)) for qp in q_pairs], axis=0)
        out.append(lax.dot_general(qs, kcat, (((1,), (1,)), ((), ())), preferred_element_type=F32))
    return out


def _in_kernel(x_ref, mod_ref, n1w_ref, win_ref, wuq_ref, wukv_ref, qnw_ref, kvnw_ref,
               ca_ref, sa_ref, cb_ref, sb_ref, lnw_ref, lnb_ref, ws_ref, bs_ref, gwc_ref, sink_ref,
               ya_ref, qb_ref, kb_ref, vb_ref, yc_ref, kprev_ref, vprev_ref, band_ref):
    tm = x_ref.shape[1]
    mod = mod_ref[0]
    n1w = n1w_ref[...]
    lane = _lane_iota((IN_ROWS, LANES))
    a_first = (lane % HEAD_DIM) < (HEAD_DIM // 2)
    b_first = lane < (B_NOPE + B_ROPE // 2)

    first_tile = pl.program_id(1) == 0

    @pl.when(first_tile)
    def _():
        kprev_ref[...] = jnp.zeros_like(kprev_ref)
        vprev_ref[...] = jnp.zeros_like(vprev_ref)

    pairs = A_WIDTH // LANES
    qi = lax.broadcasted_iota(jnp.int32, (pairs * BLOCK, 2 * BLOCK), 0) % BLOCK
    kj = lax.broadcasted_iota(jnp.int32, (pairs * BLOCK, 2 * BLOCK), 1)
    rel = qi + BLOCK - kj
    band = (rel >= 0) & (rel < BLOCK)
    band_ref[...] = jnp.where(band, 0.0, NEG_INF)
    sinks = [jnp.concatenate([jnp.broadcast_to(sink_ref[2 * p + half:2 * p + half + 1, :], (BLOCK, LANES))
                              for p in range(pairs)], axis=0) for half in range(2)]
    ones_blk = jnp.ones((2 * BLOCK, LANES), BF16)
    low_a = _lane_iota((BLOCK, LANES)) < HEAD_DIM
    k_last = kprev_ref[...]
    v_last = vprev_ref[...]
    b_scale = float((B_NOPE + B_ROPE) ** -0.5) * LOG2E
    row = lax.broadcasted_iota(jnp.int32, (BLOCK, BLOCK), 0)
    col = lax.broadcasted_iota(jnp.int32, (BLOCK, BLOCK), 1)
    w_s = [jnp.where(col <= row, ws_ref[g], 0.0).astype(BF16) for g in range(C_GROUPS)]
    low = _lane_iota((BLOCK, LANES)) < C_GROUP_DIM
    bias = bs_ref[...]
    gwc = gwc_ref[...]

    def project(r0):
        x = x_ref[0, r0:r0 + IN_ROWS, :]
        h = (_rms(x, n1w) * (1.0 + mod[1:2]) + mod[0:1]).astype(BF16)
        return jnp.dot(h, win_ref[...], preferred_element_type=F32)

    starts = list(range(0, tm, IN_ROWS))
    projs = {starts[0]: project(starts[0])}
    for idx, r0 in enumerate(starts):
        if idx + 1 < len(starts):
            projs[starts[idx + 1]] = project(starts[idx + 1])
        proj = projs.pop(r0)
        rs = slice(r0, r0 + IN_ROWS)

        ca = ca_ref[0, rs, :]
        sa = sa_ref[0, rs, :]

        def rope_a(t):
            sw = jnp.where(a_first, pltpu.roll(t, LANES - HEAD_DIM // 2, 1), pltpu.roll(t, HEAD_DIM // 2, 1))
            return t * ca + sw * sa

        q_pairs = [(rope_a(proj[:, OFF_AQ + p * LANES:OFF_AQ + (p + 1) * LANES]) * LOG2E).astype(BF16)
                   for p in range(pairs)]
        ka = rope_a(proj[:, OFF_AK:OFF_AK + A_KV_WIDTH]).astype(BF16)
        va = proj[:, OFF_AV:OFF_AV + A_KV_WIDTH].astype(BF16)

        n_blk = IN_ROWS // BLOCK
        kcats, vcats, scs = [], [], []
        for nb in range(n_blk):
            if nb == 0:
                kcats.append(jnp.concatenate([k_last, ka[0:BLOCK]], axis=0))
                vcats.append(jnp.concatenate([v_last, va[0:BLOCK]], axis=0))
            else:
                kcats.append(ka[(nb - 1) * BLOCK:(nb + 1) * BLOCK])
                vcats.append(va[(nb - 1) * BLOCK:(nb + 1) * BLOCK])
            scs.append(_swa_scores([qp[nb * BLOCK:(nb + 1) * BLOCK] for qp in q_pairs], kcats[nb]))
        k_last = ka[IN_ROWS - BLOCK:]
        v_last = va[IN_ROWS - BLOCK:]

        cb = cb_ref[0, rs, :]
        sb = sb_ref[0, rs, :]

        def rope_b(t):
            sw = jnp.where(b_first, pltpu.roll(t, LANES - B_ROPE // 2, 1), pltpu.roll(t, B_ROPE // 2, 1))
            return t * cb + sw * sb

        cq = _rms(proj[:, OFF_BCQ:OFF_BCQ + B_Q_RANK], qnw_ref[...]).astype(BF16)
        qb = jnp.dot(cq, wuq_ref[...], preferred_element_type=F32)
        ckv = _rms(proj[:, OFF_BCKV:OFF_BCKV + B_KV_RANK], kvnw_ref[...]).astype(BF16)
        kv = jnp.dot(ckv, wukv_ref[...], preferred_element_type=F32)

        ms = []
        for nb in range(n_blk):
            for half in range(2):
                if idx == 0 and nb == 0:
                    seen = band & (kj >= jnp.where(first_tile, BLOCK, 0))
                    scs[nb][half] = jnp.where(seen, scs[nb][half], NEG_INF)
                else:
                    scs[nb][half] = scs[nb][half] + band_ref[...]
            ms.append([jnp.maximum(jnp.max(scs[nb][half], axis=-1, keepdims=True), sinks[half])
                       for half in range(2)])

        kr = rope_b(proj[:, OFF_BKR:OFF_BKR + LANES])
        for hh in range(B_HEADS):
            sl = slice(hh * LANES, (hh + 1) * LANES)
            qb_ref[0, rs, sl] = (rope_b(qb[:, sl]) * b_scale).astype(BF16)
            kb_ref[0, rs, sl] = (kv[:, sl] + kr).astype(BF16)
        vb_ref[0, rs, :] = kv[:, B_HEADS * LANES:].astype(BF16)

        pvs = []
        for nb in range(n_blk):
            pvs.append([])
            for half in range(2):
                e = jnp.concatenate([jnp.exp2(scs[nb][half][:, t * LANES:(t + 1) * LANES] - ms[nb][half])
                                     for t in range(2 * BLOCK // LANES)], axis=1).astype(BF16)
                pvs[nb].append(jnp.dot(e, jnp.concatenate([vcats[nb], ones_blk], axis=1),
                                       preferred_element_type=F32))

        u = _gelu(proj[:, OFF_CU:OFF_CU + C_WIDTH])
        v = _gelu(proj[:, OFF_CV:OFF_CV + C_WIDTH])
        mu = jnp.mean(v, axis=-1, keepdims=True)
        vc = v - mu
        var = jnp.mean(vc * vc, axis=-1, keepdims=True)
        v = (vc * lax.rsqrt(var + NORM_EPS) * lnw_ref[...] + lnb_ref[...]).astype(BF16)
        for c in range(IN_ROWS // BLOCK):
            rows = slice(c * BLOCK, (c + 1) * BLOCK)
            parts = []
            for p in range(C_WIDTH // LANES):
                vp = v[rows, p * LANES:(p + 1) * LANES]
                r_lo = jnp.dot(w_s[2 * p], vp, preferred_element_type=F32)
                r_hi = jnp.dot(w_s[2 * p + 1], vp, preferred_element_type=F32)
                parts.append(jnp.where(low, r_lo, r_hi))
            mixed = jnp.concatenate(parts, axis=-1) + bias
            yc = u[rows] * mixed
            yc_ref[0, r0 + c * BLOCK:r0 + (c + 1) * BLOCK, :] = _rms(yc, gwc).astype(BF16)

        for nb in range(n_blk):
            outs = [pvs[nb][half][:, :LANES] / (pvs[nb][half][:, LANES:] + jnp.exp2(sinks[half] - ms[nb][half]))
                    for half in range(2)]
            for p in range(pairs):
                blk = slice(p * BLOCK, (p + 1) * BLOCK)
                ya_ref[0, r0 + nb * BLOCK:r0 + (nb + 1) * BLOCK, p * LANES:(p + 1) * LANES] = jnp.where(
                    low_a, outs[0][blk], outs[1][blk])

    kprev_ref[...] = k_last
    vprev_ref[...] = v_last


def _in_proj(x, mod, n1w, win, wuq, wukv, qnw, kvnw, tabs, lnw, lnb, ws, bs_full, gwc, sink_tab, tm, layer):
    b, s, d = x.shape
    tok = lambda w: pl.BlockSpec((1, tm, w), lambda i, j: (i, j, 0))
    full = lambda a: pl.BlockSpec(a.shape, lambda i, j: (0,) * a.ndim)
    stacked = lambda a: pl.BlockSpec((None,) + a.shape[1:], lambda i, j: (layer,) + (0,) * (a.ndim - 1))
    ca, sa, cb, sb = tabs
    out_widths = (A_WIDTH, B_HEADS * B_QK_PAD, B_HEADS * B_QK_PAD, B_WIDTH, C_WIDTH)
    out_specs = [tok(w) for w in out_widths]
    out_shape = [jax.ShapeDtypeStruct((b, s, w), F32 if n == 0 else BF16) for n, w in enumerate(out_widths)]
    return pl.pallas_call(
        _in_kernel,
        grid=(b, s // tm),
        in_specs=[
            tok(d),
            pl.BlockSpec((1, N_MOD, d), lambda i, j: (i, 0, 0)),
            full(n1w), stacked(win), stacked(wuq), stacked(wukv), full(qnw), full(kvnw),
            tok(LANES), tok(LANES), tok(LANES), tok(LANES),
            full(lnw), full(lnb), stacked(ws), stacked(bs_full), full(gwc), stacked(sink_tab),
        ],
        out_specs=out_specs,
        out_shape=out_shape,
        scratch_shapes=[pltpu.VMEM((BLOCK, A_KV_WIDTH), BF16), pltpu.VMEM((BLOCK, A_KV_WIDTH), BF16),
                        pltpu.VMEM((A_WIDTH // LANES * BLOCK, 2 * BLOCK), F32)],
        compiler_params=pltpu.CompilerParams(
            dimension_semantics=("arbitrary", "arbitrary"), vmem_limit_bytes=VMEM_LIMIT),
        name="in_proj",
    )(x, mod, n1w, win, wuq, wukv, qnw, kvnw, ca, sa, cb, sb, lnw, lnb, ws, bs_full, gwc, sink_tab)


def _mla_tile(q_ref, k_ref, v_ref, o_ref, tile):
    tq = q_ref.shape[1]
    rows = MLA_ROWS
    qi = lax.broadcasted_iota(jnp.int32, (rows, rows), 0)
    kj = lax.broadcasted_iota(jnp.int32, (rows, rows), 1)
    causal = kj <= qi
    nt = (((1,), (1,)), ((), ()))

    subs = list(range(tq // rows))
    segments = {}
    for sub in subs:
        first_row = tile * tq + sub * rows
        segments[sub] = [(lo, min(MLA_KV_SEG, first_row - lo), False) for lo in range(0, first_row, MLA_KV_SEG)]
        segments[sub].append((first_row, rows, True))
    chains = [(sub, hh) for sub in subs for hh in range(2)]
    state = {ch: None for ch in chains}

    for rnd in range(max(len(s) for s in segments.values())):
        live = [ch for ch in chains if rnd < len(segments[ch[0]])]
        v_aug = {}
        for sub in sorted({ch[0] for ch in live}):
            lo, width, _ = segments[sub][rnd]
            v = v_ref[0, lo:lo + width, :]
            low_v = _lane_iota(v.shape) < B_V
            one = jnp.ones_like(v)
            v_aug[sub] = (jnp.where(low_v, v, one), jnp.where(low_v, one, v))
        sc, m_new = {}, {}
        for sub, hh in live:
            lo, width, _ = segments[sub][rnd]
            q = q_ref[0, sub * rows:(sub + 1) * rows, hh * LANES:(hh + 1) * LANES]
            k = k_ref[0, lo:lo + width, hh * LANES:(hh + 1) * LANES]
            sc[sub, hh] = lax.dot_general(q, k, nt, preferred_element_type=F32)
        for ch in live:
            if segments[ch[0]][rnd][2]:
                sc[ch] = jnp.where(causal, sc[ch], NEG_INF)
            m_blk = jnp.max(sc[ch], axis=-1, keepdims=True)
            m_new[ch] = m_blk if state[ch] is None else jnp.maximum(state[ch][0], m_blk)
        pv = {ch: jnp.dot(jnp.exp2(sc[ch] - m_new[ch]).astype(BF16), v_aug[ch[0]][ch[1]],
                          preferred_element_type=F32) for ch in live}
        for ch in live:
            if state[ch] is None:
                acc = pv[ch]
            else:
                m, acc = state[ch]
                acc = jnp.exp2(m - m_new[ch]) * acc + pv[ch]
            state[ch] = (m_new[ch], acc)

    low = _lane_iota((rows, LANES)) < B_V
    for sub in subs:
        acc0, acc1 = state[sub, 0][1], state[sub, 1][1]
        num = jnp.where(low, acc0, acc1)
        den = pltpu.roll(jnp.where(low, acc1, acc0), B_V, 1)
        o_ref[0, sub * rows:(sub + 1) * rows, :] = num / den


def _mla_kernel(q_ref, k_ref, v_ref, o_ref):
    i = pl.program_id(2)
    for tile in range(k_ref.shape[1] // q_ref.shape[1]):
        pl.when(i == tile)(functools.partial(_mla_tile, q_ref, k_ref, v_ref, o_ref, tile))


def _mla(qb, kb, vb, tq):
    b, s, _ = qb.shape
    pairs = B_HEADS // 2
    return pl.pallas_call(
        _mla_kernel,
        grid=(b, pairs, s // tq),
        in_specs=[
            pl.BlockSpec((1, tq, 2 * B_QK_PAD), lambda i, p, j: (i, j, p)),
            pl.BlockSpec((1, s, 2 * B_QK_PAD), lambda i, p, j: (i, 0, p)),
            pl.BlockSpec((1, s, LANES), lambda i, p, j: (i, 0, p)),
        ],
        out_specs=pl.BlockSpec((1, tq, LANES), lambda i, p, j: (i, j, p)),
        out_shape=jax.ShapeDtypeStruct((b, s, B_WIDTH), F32),
        compiler_params=pltpu.CompilerParams(
            dimension_semantics=("arbitrary", "arbitrary", "arbitrary"), vmem_limit_bytes=VMEM_LIMIT),
        name="mla",
    )(qb, kb, vb)


def _out_kernel(x_ref, ya_ref, yb_ref, yc_ref, mod_ref, gwa_ref, gwb_ref, wout_ref, n2w_ref,
                wgu_ref, wdn_ref, fw_ref, o_ref, *, final):
    tm = x_ref.shape[1]
    mod = mod_ref[0]
    ffn = wdn_ref.shape[0]

    def mix(r0):
        rs = slice(r0, r0 + OUT_ROWS)
        y = jnp.concatenate([
            _rms(ya_ref[0, rs, :], gwa_ref[...]).astype(BF16),
            _rms(yb_ref[0, rs, :], gwb_ref[...]).astype(BF16),
            yc_ref[0, rs, :],
        ], axis=-1)
        x1 = x_ref[0, rs, :] + mod[2:3] * jnp.dot(y, wout_ref[...], preferred_element_type=F32)
        h = (_rms(x1, n2w_ref[...]) * (1.0 + mod[4:5]) + mod[3:4]).astype(BF16)
        return x1, h

    starts = list(range(0, tm, OUT_ROWS))
    mixed = {starts[0]: mix(starts[0])}
    for idx, r0 in enumerate(starts):
        if idx + 1 < len(starts):
            mixed[starts[idx + 1]] = mix(starts[idx + 1])
        x1, h = mixed.pop(r0)
        acc = None
        for lo in range(0, ffn, FFN_CHUNK):
            g = jnp.dot(h, wgu_ref[:, lo:lo + FFN_CHUNK], preferred_element_type=F32)
            u = jnp.dot(h, wgu_ref[:, ffn + lo:ffn + lo + FFN_CHUNK], preferred_element_type=F32)
            a = (g * jax.nn.sigmoid(g) * u).astype(BF16)
            part = jnp.dot(a, wdn_ref[lo:lo + FFN_CHUNK, :], preferred_element_type=F32)
            acc = part if acc is None else acc + part
        x2 = x1 + mod[5:6] * acc
        if final:
            x2 = _rms(x2, fw_ref[...])
        o_ref[0, r0:r0 + OUT_ROWS, :] = x2


def _out_ffn(x, ya, yb, yc, mod, gwa, gwb, wout, n2w, wgu, wdn, fw, tm, layer, final):
    b, s, d = x.shape
    tok = lambda w: pl.BlockSpec((1, tm, w), lambda i, j: (i, j, 0))
    full = lambda a: pl.BlockSpec(a.shape, lambda i, j: (0,) * a.ndim, pipeline_mode=pl.Buffered(1))
    stacked = lambda a: pl.BlockSpec((None,) + a.shape[1:], lambda i, j: (layer,) + (0,) * (a.ndim - 1),
                                     pipeline_mode=pl.Buffered(1))
    return pl.pallas_call(
        functools.partial(_out_kernel, final=final),
        grid=(b, s // tm),
        in_specs=[
            tok(d), tok(A_WIDTH), tok(B_WIDTH), tok(C_WIDTH),
            pl.BlockSpec((1, N_MOD, d), lambda i, j: (i, 0, 0)),
            full(gwa), full(gwb), stacked(wout), full(n2w), stacked(wgu), stacked(wdn), full(fw),
        ],
        out_specs=tok(d),
        out_shape=jax.ShapeDtypeStruct((b, s, d), F32),
        compiler_params=pltpu.CompilerParams(
            dimension_semantics=("arbitrary", "arbitrary"), vmem_limit_bytes=VMEM_LIMIT),
        name="out_ffn",
    )(x, ya, yb, yc, mod, gwa, gwb, wout, n2w, wgu, wdn, fw)


def _in_col_index():
    src_aq, src_ak, src_av = 0, A_WIDTH, A_WIDTH + A_KV_WIDTH
    src_bcq = src_av + A_KV_WIDTH
    src_bckv = src_bcq + B_Q_RANK
    src_bkr = src_bckv + B_KV_RANK
    src_cu = src_bkr + B_ROPE
    src_cv = src_cu + C_WIDTH
    zero = src_cv + C_WIDTH
    idx = []
    for hh in A_HEAD_ORDER:
        idx += list(range(src_aq + hh * HEAD_DIM, src_aq + (hh + 1) * HEAD_DIM))
    idx += list(range(src_ak, src_bkr))
    idx += list(range(src_cu, zero))
    idx += [zero] * B_NOPE + list(range(src_bkr, src_bkr + B_ROPE)) + [zero] * (LANES - B_NOPE - B_ROPE)
    assert len(idx) == IN_COLS_PAD
    return np.asarray(idx, np.int32)


def _take_static(a, idx, axis):
    n = a.shape[axis]
    idx = [int(t) for t in idx]
    parts, start = [], 0
    for pos in range(1, len(idx) + 1):
        if pos < len(idx):
            prev, cur = idx[pos - 1], idx[pos]
            same_run = (prev == n and cur == n) or (prev != n and cur != n and cur == prev + 1)
        else:
            same_run = False
        if not same_run:
            first, count = idx[start], pos - start
            if first == n:
                shape = a.shape[:axis] + (count,) + a.shape[axis + 1:]
                parts.append(jnp.zeros(shape, a.dtype))
            else:
                parts.append(lax.slice_in_dim(a, first, first + count, axis=axis))
            start = pos
    return jnp.concatenate(parts, axis=axis)


def _rope_tables(positions):
    pos = positions.astype(F32)[..., None]

    def tables(dim):
        inv = 1.0 / (ROPE_THETA ** (jnp.arange(0, dim, 2, dtype=F32) / dim))
        ang = pos * inv
        return jnp.cos(ang), jnp.sin(ang)

    cos_a, sin_a = tables(HEAD_DIM)
    ca = jnp.concatenate([cos_a] * 4, axis=-1)
    sa = jnp.concatenate([-sin_a, sin_a] * 2, axis=-1)
    cos_b, sin_b = tables(B_ROPE)
    ones = jnp.ones(pos.shape[:-1] + (B_NOPE,), F32)
    zeros = jnp.zeros(pos.shape[:-1] + (LANES - B_NOPE - B_ROPE,), F32)
    cb = jnp.concatenate([ones, cos_b, cos_b, zeros], axis=-1)
    sb = jnp.concatenate([0.0 * ones, -sin_b, sin_b, zeros], axis=-1)
    return ca, sa, cb, sb


def kernel(x, c, positions, ada_w, ada_b, norm1_w, w_in, a_sinks, b_q_norm_w, b_w_uq, b_kv_norm_w, b_w_ukv,
           c_ln_w, c_ln_b, c_w_s, c_b_s, out_norm_w, w_out, norm2_w, w_gate_up, w_down, final_norm_w):
    depth = w_in.shape[0]
    b, s, d = x.shape
    ffn = w_down.shape[1]
    tm_in, tm_out, tq = min(IN_TILE, s), min(OUT_TILE, s), min(ATTN_TILE, s)

    mod = _modulation(c, ada_w, ada_b).reshape(depth, b, N_MOD, d)
    tabs = _rope_tables(positions)

    col_scale = np.ones((IN_COLS_PAD,), np.float32)
    col_scale[OFF_AQ:OFF_AQ + A_WIDTH] = HEAD_DIM ** -0.5
    w_in_p = (_take_static(w_in, _in_col_index(), 2) * col_scale).astype(BF16)

    w_uq = b_w_uq.reshape(depth, B_Q_RANK, B_HEADS, B_NOPE + B_ROPE)
    w_uq = jnp.pad(w_uq, ((0, 0), (0, 0), (0, 0), (0, B_QK_PAD - B_NOPE - B_ROPE)))
    w_uq = w_uq.reshape(depth, B_Q_RANK, B_HEADS * B_QK_PAD).astype(BF16)
    w_ukv = b_w_ukv.reshape(depth, B_KV_RANK, B_HEADS, B_NOPE + B_V)
    w_uk = jnp.pad(w_ukv[..., :B_NOPE], ((0, 0), (0, 0), (0, 0), (0, B_QK_PAD - B_NOPE)))
    w_ukv = jnp.concatenate([w_uk.reshape(depth, B_KV_RANK, B_HEADS * B_QK_PAD),
                             w_ukv[..., B_NOPE:].reshape(depth, B_KV_RANK, B_WIDTH)], axis=-1).astype(BF16)

    bs_full = jnp.repeat(jnp.swapaxes(c_b_s, 1, 2), C_GROUP_DIM, axis=-1)
    order = np.asarray(A_HEAD_ORDER)
    sink_tab = jnp.broadcast_to((a_sinks * LOG2E)[:, order, None], (depth, A_Q_HEADS, LANES))
    sink_tab = jnp.pad(sink_tab, ((0, 0), (0, 8 - A_Q_HEADS), (0, 0)))

    a_perm = (order[:, None] * HEAD_DIM + np.arange(HEAD_DIM)[None, :]).reshape(-1)
    out_perm = np.concatenate([a_perm, np.arange(A_WIDTH, d)])
    gw = _take_static(out_norm_w, out_perm, 1)
    w_out_p = _take_static(w_out, out_perm, 1).astype(BF16)

    assert ffn % FFN_CHUNK == 0
    w_gu = w_gate_up.astype(BF16)
    w_dn = w_down.astype(BF16)

    row = lambda a: a.reshape(1, -1)
    for l in range(depth):
        ya, qb, kb, vb, yc = _in_proj(
            x, mod[l], row(norm1_w[l]), w_in_p, w_uq, w_ukv, row(b_q_norm_w[l]), row(b_kv_norm_w[l]),
            tabs, row(c_ln_w[l]), row(c_ln_b[l]), c_w_s, bs_full, row(gw[l, A_WIDTH + B_WIDTH:]), sink_tab,
            tm_in, l)
        yb = _mla(qb, kb, vb, tq)
        x = _out_ffn(x, ya, yb, yc, mod[l], row(gw[l, :A_WIDTH]), row(gw[l, A_WIDTH:A_WIDTH + B_WIDTH]),
                     w_out_p, row(norm2_w[l]), w_gu, w_dn, row(final_norm_w), tm_out, l, l == depth - 1)
    return x
```

```python
import functools

import numpy as np
import jax
import jax.numpy as jnp
from jax import lax
from jax.experimental import pallas as pl
from jax.experimental.pallas import tpu as pltpu

F32 = jnp.float32
BF16 = jnp.bfloat16

LANES = 128
HEAD_DIM = 64
ROPE_THETA = 10000.0
NORM_EPS = 1e-6
NEG_INF = -1e30
LOG2E = float(np.log2(np.e))
BLOCK = 128

A_Q_HEADS = 6
A_KV_HEADS = 2
A_WIDTH = A_Q_HEADS * HEAD_DIM
A_KV_WIDTH = A_KV_HEADS * HEAD_DIM
A_HEAD_ORDER = (0, 3, 1, 4, 2, 5)

B_HEADS = 6
B_Q_RANK = 384
B_KV_RANK = 256
B_NOPE = 64
B_ROPE = 32
B_V = 64
B_WIDTH = B_HEADS * B_V
B_QK_PAD = LANES

C_GROUPS = 4
C_GROUP_DIM = 64
C_WIDTH = C_GROUPS * C_GROUP_DIM

N_MOD = 6

OFF_AQ = 0
OFF_AK = OFF_AQ + A_WIDTH
OFF_AV = OFF_AK + A_KV_WIDTH
OFF_BCQ = OFF_AV + A_KV_WIDTH
OFF_BCKV = OFF_BCQ + B_Q_RANK
OFF_CU = OFF_BCKV + B_KV_RANK
OFF_CV = OFF_CU + C_WIDTH
OFF_BKR = OFF_CV + C_WIDTH
IN_COLS_PAD = OFF_BKR + LANES

VMEM_LIMIT = 56 * 1024 * 1024
MLA_KV_SEG = 2048
IN_ROWS = 256
IN_TILE = 1024
IN_COL_CHUNK = 512
OUT_TILE = 1024
OUT_ROWS = 512
ATTN_TILE = 2048
MLA_ROWS = 512
FFN_CHUNK = 256


def _rms(x, w):
    return x * lax.rsqrt(jnp.mean(x * x, axis=-1, keepdims=True) + NORM_EPS) * w


def _gelu(x):
    return 0.5 * x * (1.0 + lax.erf(x * float(np.sqrt(0.5))))


def _lane_iota(shape):
    return lax.broadcasted_iota(jnp.int32, shape, len(shape) - 1)


def _mod_kernel(c_ref, w_ref, b_ref, o_ref):
    c = c_ref[...]
    act = (c * jax.nn.sigmoid(c)).astype(BF16)
    o_ref[0] = jnp.dot(act, w_ref[0].astype(BF16), preferred_element_type=F32) + b_ref[0]


def _modulation(c, ada_w, ada_b):
    depth, d, n = ada_w.shape
    b = c.shape[0]
    tn = 1536
    return pl.pallas_call(
        _mod_kernel,
        grid=(depth, n // tn),
        in_specs=[
            pl.BlockSpec((b, d), lambda l, j: (0, 0)),
            pl.BlockSpec((1, d, tn), lambda l, j: (l, 0, j)),
            pl.BlockSpec((1, 1, tn), lambda l, j: (l, 0, j)),
        ],
        out_specs=pl.BlockSpec((1, b, tn), lambda l, j: (l, 0, j)),
        out_shape=jax.ShapeDtypeStruct((depth, b, n), F32),
        compiler_params=pltpu.CompilerParams(
            dimension_semantics=("arbitrary", "arbitrary"), vmem_limit_bytes=VMEM_LIMIT),
        name="modulation",
    )(c, ada_w, ada_b.reshape(depth, 1, n))


def _swa_scores(q_pairs, kcat):
    low = _lane_iota((BLOCK, LANES)) < HEAD_DIM
    out = []
    for half in range(2):
        keep = low if half == 0 else jnp.logical_not(low)
        qs = jnp.concatenate([jnp.where(keep, qp, jnp.zeros_like(qp)) for qp in q_pairs], axis=0)
        out.append(lax.dot_general(qs, kcat, (((1,), (1,)), ((), ())), preferred_element_type=F32))
    return out


def _in_kernel(x_ref, mod_ref, n1w_ref, win_ref, wuq_ref, wukv_ref, qnw_ref, kvnw_ref,
               ca_ref, sa_ref, cb_ref, sb_ref, lnw_ref, lnb_ref, ws_ref, bs_ref, gwc_ref, sink_ref,
               ya_ref, qb_ref, kb_ref, vb_ref, yc_ref, kprev_ref, vprev_ref, band_ref):
    tm = x_ref.shape[1]
    mod = mod_ref[0]
    n1w = n1w_ref[...]
    lane = _lane_iota((IN_ROWS, LANES))
    a_first = (lane % HEAD_DIM) < (HEAD_DIM // 2)
    b_first = lane < (B_NOPE + B_ROPE // 2)

    first_tile = pl.program_id(1) == 0

    @pl.when(first_tile)
    def _():
        kprev_ref[...] = jnp.zeros_like(kprev_ref)
        vprev_ref[...] = jnp.zeros_like(vprev_ref)

    pairs = A_WIDTH // LANES
    qi = lax.broadcasted_iota(jnp.int32, (pairs * BLOCK, 2 * BLOCK), 0) % BLOCK
    kj = lax.broadcasted_iota(jnp.int32, (pairs * BLOCK, 2 * BLOCK), 1)
    rel = qi + BLOCK - kj
    band = (rel >= 0) & (rel < BLOCK)
    band_ref[...] = jnp.where(band, 0.0, NEG_INF)
    sinks = [jnp.concatenate([jnp.broadcast_to(sink_ref[2 * p + half:2 * p + half + 1, :], (BLOCK, LANES))
                              for p in range(pairs)], axis=0) for half in range(2)]
    ones_blk = jnp.ones((2 * BLOCK, LANES), BF16)
    low_a = _lane_iota((BLOCK, LANES)) < HEAD_DIM
    k_last = kprev_ref[...]
    v_last = vprev_ref[...]
    b_scale = float((B_NOPE + B_ROPE) ** -0.5) * LOG2E
    row = lax.broadcasted_iota(jnp.int32, (BLOCK, BLOCK), 0)
    col = lax.broadcasted_iota(jnp.int32, (BLOCK, BLOCK), 1)
    w_s = [jnp.where(col <= row, ws_ref[g], 0.0).astype(BF16) for g in range(C_GROUPS)]
    low = _lane_iota((BLOCK, LANES)) < C_GROUP_DIM
    bias = bs_ref[...]
    gwc = gwc_ref[...]

    def normed(r0):
        x = x_ref[0, r0:r0 + IN_ROWS, :]
        return (_rms(x, n1w) * (1.0 + mod[1:2]) + mod[0:1]).astype(BF16)

    n_col_chunks = -(-IN_COLS_PAD // IN_COL_CHUNK)

    def project_chunk(h, c):
        lo = c * IN_COL_CHUNK
        return jnp.dot(h, win_ref[:, lo:min(lo + IN_COL_CHUNK, IN_COLS_PAD)], preferred_element_type=F32)

    starts = list(range(0, tm, IN_ROWS))
    h_first = normed(starts[0])
    cur = [project_chunk(h_first, c) for c in range(n_col_chunks)]
    for idx, r0 in enumerate(starts):
        has_next = idx + 1 < len(starts)
        h_next = normed(starts[idx + 1]) if has_next else None
        nxt = []

        def ahead():
            if has_next and len(nxt) < n_col_chunks:
                nxt.append(project_chunk(h_next, len(nxt)))

        def cols(lo, width, chunks=cur):
            c = lo // IN_COL_CHUNK
            assert (lo + width - 1) // IN_COL_CHUNK == c
            return chunks[c][:, lo - c * IN_COL_CHUNK:lo + width - c * IN_COL_CHUNK]

        rs = slice(r0, r0 + IN_ROWS)
        ahead()

        ca = ca_ref[0, rs, :]
        sa = sa_ref[0, rs, :]

        def rope_a(t):
            sw = jnp.where(a_first, pltpu.roll(t, LANES - HEAD_DIM // 2, 1), pltpu.roll(t, HEAD_DIM // 2, 1))
            return t * ca + sw * sa

        q_pairs = [(rope_a(cols(OFF_AQ + p * LANES, LANES)) * LOG2E).astype(BF16) for p in range(pairs)]
        ka = rope_a(cols(OFF_AK, A_KV_WIDTH)).astype(BF16)
        va = cols(OFF_AV, A_KV_WIDTH).astype(BF16)

        n_blk = IN_ROWS // BLOCK
        kcats, vcats, scs = [], [], []
        for nb in range(n_blk):
            if nb == 0:
                kcats.append(jnp.concatenate([k_last, ka[0:BLOCK]], axis=0))
                vcats.append(jnp.concatenate([v_last, va[0:BLOCK]], axis=0))
            else:
                kcats.append(ka[(nb - 1) * BLOCK:(nb + 1) * BLOCK])
                vcats.append(va[(nb - 1) * BLOCK:(nb + 1) * BLOCK])
            scs.append(_swa_scores([qp[nb * BLOCK:(nb + 1) * BLOCK] for qp in q_pairs], kcats[nb]))
        k_last = ka[IN_ROWS - BLOCK:]
        v_last = va[IN_ROWS - BLOCK:]

        cb = cb_ref[0, rs, :]
        sb = sb_ref[0, rs, :]

        def rope_b(t):
            sw = jnp.where(b_first, pltpu.roll(t, LANES - B_ROPE // 2, 1), pltpu.roll(t, B_ROPE // 2, 1))
            return t * cb + sw * sb

        ahead()
        cq = _rms(cols(OFF_BCQ, B_Q_RANK), qnw_ref[...]).astype(BF16)
        qb = jnp.dot(cq, wuq_ref[...], preferred_element_type=F32)
        ckv = _rms(cols(OFF_BCKV, B_KV_RANK), kvnw_ref[...]).astype(BF16)
        kv = jnp.dot(ckv, wukv_ref[...], preferred_element_type=F32)

        ms = []
        for nb in range(n_blk):
            for half in range(2):
                if idx == 0 and nb == 0:
                    seen = band & (kj >= jnp.where(first_tile, BLOCK, 0))
                    scs[nb][half] = jnp.where(seen, scs[nb][half], NEG_INF)
                else:
                    scs[nb][half] = scs[nb][half] + band_ref[...]
            ms.append([jnp.maximum(jnp.max(scs[nb][half], axis=-1, keepdims=True), sinks[half])
                       for half in range(2)])

        ahead()
        kr = rope_b(cols(OFF_BKR, LANES))
        for hh in range(B_HEADS):
            sl = slice(hh * LANES, (hh + 1) * LANES)
            qb_ref[0, rs, sl] = (rope_b(qb[:, sl]) * b_scale).astype(BF16)
            kb_ref[0, rs, sl] = (kv[:, sl] + kr).astype(BF16)
        vb_ref[0, rs, :] = kv[:, B_HEADS * LANES:].astype(BF16)

        pvs = []
        for nb in range(n_blk):
            pvs.append([])
            for half in range(2):
                e = jnp.concatenate([jnp.exp2(scs[nb][half][:, t * LANES:(t + 1) * LANES] - ms[nb][half])
                                     for t in range(2 * BLOCK // LANES)], axis=1).astype(BF16)
                pvs[nb].append(jnp.dot(e, jnp.concatenate([vcats[nb], ones_blk], axis=1),
                                       preferred_element_type=F32))

        ahead()
        u = _gelu(cols(OFF_CU, C_WIDTH))
        v = _gelu(cols(OFF_CV, C_WIDTH))
        mu = jnp.mean(v, axis=-1, keepdims=True)
        vc = v - mu
        var = jnp.mean(vc * vc, axis=-1, keepdims=True)
        v = (vc * lax.rsqrt(var + NORM_EPS) * lnw_ref[...] + lnb_ref[...]).astype(BF16)
        for c in range(IN_ROWS // BLOCK):
            rows = slice(c * BLOCK, (c + 1) * BLOCK)
            parts = []
            for p in range(C_WIDTH // LANES):
                vp = v[rows, p * LANES:(p + 1) * LANES]
                r_lo = jnp.dot(w_s[2 * p], vp, preferred_element_type=F32)
                r_hi = jnp.dot(w_s[2 * p + 1], vp, preferred_element_type=F32)
                parts.append(jnp.where(low, r_lo, r_hi))
            mixed = jnp.concatenate(parts, axis=-1) + bias
            yc = u[rows] * mixed
            yc_ref[0, r0 + c * BLOCK:r0 + (c + 1) * BLOCK, :] = _rms(yc, gwc).astype(BF16)

        for nb in range(n_blk):
            outs = [pvs[nb][half][:, :LANES] / (pvs[nb][half][:, LANES:] + jnp.exp2(sinks[half] - ms[nb][half]))
                    for half in range(2)]
            for p in range(pairs):
                blk = slice(p * BLOCK, (p + 1) * BLOCK)
                ya_ref[0, r0 + nb * BLOCK:r0 + (nb + 1) * BLOCK, p * LANES:(p + 1) * LANES] = jnp.where(
                    low_a, outs[0][blk], outs[1][blk])

        while has_next and len(nxt) < n_col_chunks:
            ahead()
        cur = nxt

    kprev_ref[...] = k_last
    vprev_ref[...] = v_last


def _in_proj(x, mod, n1w, win, wuq, wukv, qnw, kvnw, tabs, lnw, lnb, ws, bs_full, gwc, sink_tab, tm, layer):
    b, s, d = x.shape
    tok = lambda w: pl.BlockSpec((1, tm, w), lambda i, j: (i, j, 0))
    full = lambda a: pl.BlockSpec(a.shape, lambda i, j: (0,) * a.ndim)
    stacked = lambda a: pl.BlockSpec((None,) + a.shape[1:], lambda i, j: (layer,) + (0,) * (a.ndim - 1))
    ca, sa, cb, sb = tabs
    out_widths = (A_WIDTH, B_HEADS * B_QK_PAD, B_HEADS * B_QK_PAD, B_WIDTH, C_WIDTH)
    out_specs = [tok(w) for w in out_widths]
    out_shape = [jax.ShapeDtypeStruct((b, s, w), F32 if n == 0 else BF16) for n, w in enumerate(out_widths)]
    return pl.pallas_call(
        _in_kernel,
        grid=(b, s // tm),
        in_specs=[
            tok(d),
            pl.BlockSpec((1, N_MOD, d), lambda i, j: (i, 0, 0)),
            full(n1w), stacked(win), stacked(wuq), stacked(wukv), full(qnw), full(kvnw),
            tok(LANES), tok(LANES), tok(LANES), tok(LANES),
            full(lnw), full(lnb), stacked(ws), stacked(bs_full), full(gwc), stacked(sink_tab),
        ],
        out_specs=out_specs,
        out_shape=out_shape,
        scratch_shapes=[pltpu.VMEM((BLOCK, A_KV_WIDTH), BF16), pltpu.VMEM((BLOCK, A_KV_WIDTH), BF16),
                        pltpu.VMEM((A_WIDTH // LANES * BLOCK, 2 * BLOCK), F32)],
        compiler_params=pltpu.CompilerParams(
            dimension_semantics=("arbitrary", "arbitrary"), vmem_limit_bytes=VMEM_LIMIT),
        name="in_proj",
    )(x, mod, n1w, win, wuq, wukv, qnw, kvnw, ca, sa, cb, sb, lnw, lnb, ws, bs_full, gwc, sink_tab)


def _mla_tile(q_ref, k_ref, v_ref, o_ref, tile):
    tq = q_ref.shape[1]
    rows = MLA_ROWS
    qi = lax.broadcasted_iota(jnp.int32, (rows, rows), 0)
    kj = lax.broadcasted_iota(jnp.int32, (rows, rows), 1)
    causal = kj <= qi
    nt = (((1,), (1,)), ((), ()))

    subs = list(range(tq // rows))
    segments = {}
    for sub in subs:
        first_row = tile * tq + sub * rows
        segments[sub] = [(lo, min(MLA_KV_SEG, first_row - lo), False) for lo in range(0, first_row, MLA_KV_SEG)]
        segments[sub].append((first_row, rows, True))
    chains = [(sub, hh) for sub in subs for hh in range(2)]
    state = {ch: None for ch in chains}

    for rnd in range(max(len(s) for s in segments.values())):
        live = [ch for ch in chains if rnd < len(segments[ch[0]])]
        v_aug = {}
        for sub in sorted({ch[0] for ch in live}):
            lo, width, _ = segments[sub][rnd]
            v = v_ref[0, lo:lo + width, :]
            low_v = _lane_iota(v.shape) < B_V
            one = jnp.ones_like(v)
            v_aug[sub] = (jnp.where(low_v, v, one), jnp.where(low_v, one, v))
        sc, m_new = {}, {}
        for sub, hh in live:
            lo, width, _ = segments[sub][rnd]
            q = q_ref[0, sub * rows:(sub + 1) * rows, hh * LANES:(hh + 1) * LANES]
            k = k_ref[0, lo:lo + width, hh * LANES:(hh + 1) * LANES]
            sc[sub, hh] = lax.dot_general(q, k, nt, preferred_element_type=F32)
        for ch in live:
            if segments[ch[0]][rnd][2]:
                sc[ch] = jnp.where(causal, sc[ch], NEG_INF)
            m_blk = jnp.max(sc[ch], axis=-1, keepdims=True)
            m_new[ch] = m_blk if state[ch] is None else jnp.maximum(state[ch][0], m_blk)
        pv = {ch: jnp.dot(jnp.exp2(sc[ch] - m_new[ch]).astype(BF16), v_aug[ch[0]][ch[1]],
                          preferred_element_type=F32) for ch in live}
        for ch in live:
            if state[ch] is None:
                acc = pv[ch]
            else:
                m, acc = state[ch]
                acc = jnp.exp2(m - m_new[ch]) * acc + pv[ch]
            state[ch] = (m_new[ch], acc)

    low = _lane_iota((rows, LANES)) < B_V
    for sub in subs:
        acc0, acc1 = state[sub, 0][1], state[sub, 1][1]
        num = jnp.where(low, acc0, acc1)
        den = pltpu.roll(jnp.where(low, acc1, acc0), B_V, 1)
        o_ref[0, sub * rows:(sub + 1) * rows, :] = num / den


def _mla_kernel(q_ref, k_ref, v_ref, o_ref):
    i = pl.program_id(2)
    for tile in range(k_ref.shape[1] // q_ref.shape[1]):
        pl.when(i == tile)(functools.partial(_mla_tile, q_ref, k_ref, v_ref, o_ref, tile))


def _mla(qb, kb, vb, tq):
    b, s, _ = qb.shape
    pairs = B_HEADS // 2
    return pl.pallas_call(
        _mla_kernel,
        grid=(b, pairs, s // tq),
        in_specs=[
            pl.BlockSpec((1, tq, 2 * B_QK_PAD), lambda i, p, j: (i, j, p)),
            pl.BlockSpec((1, s, 2 * B_QK_PAD), lambda i, p, j: (i, 0, p)),
            pl.BlockSpec((1, s, LANES), lambda i, p, j: (i, 0, p)),
        ],
        out_specs=pl.BlockSpec((1, tq, LANES), lambda i, p, j: (i, j, p)),
        out_shape=jax.ShapeDtypeStruct((b, s, B_WIDTH), F32),
        compiler_params=pltpu.CompilerParams(
            dimension_semantics=("arbitrary", "arbitrary", "arbitrary"), vmem_limit_bytes=VMEM_LIMIT),
        name="mla",
    )(qb, kb, vb)


def _out_kernel(x_ref, ya_ref, yb_ref, yc_ref, mod_ref, gwa_ref, gwb_ref, wout_ref, n2w_ref,
                wgu_ref, wdn_ref, fw_ref, o_ref, *, final):
    tm = x_ref.shape[1]
    mod = mod_ref[0]
    ffn = wdn_ref.shape[0]

    def mix(r0):
        rs = slice(r0, r0 + OUT_ROWS)
        y = jnp.concatenate([
            _rms(ya_ref[0, rs, :], gwa_ref[...]).astype(BF16),
            _rms(yb_ref[0, rs, :], gwb_ref[...]).astype(BF16),
            yc_ref[0, rs, :],
        ], axis=-1)
        x1 = x_ref[0, rs, :] + mod[2:3] * jnp.dot(y, wout_ref[...], preferred_element_type=F32)
        h = (_rms(x1, n2w_ref[...]) * (1.0 + mod[4:5]) + mod[3:4]).astype(BF16)
        return x1, h

    starts = list(range(0, tm, OUT_ROWS))
    mixed = {starts[0]: mix(starts[0])}
    for idx, r0 in enumerate(starts):
        if idx + 1 < len(starts):
            mixed[starts[idx + 1]] = mix(starts[idx + 1])
        x1, h = mixed.pop(r0)
        acc = None
        for lo in range(0, ffn, FFN_CHUNK):
            g = jnp.dot(h, wgu_ref[:, lo:lo + FFN_CHUNK], preferred_element_type=F32)
            u = jnp.dot(h, wgu_ref[:, ffn + lo:ffn + lo + FFN_CHUNK], preferred_element_type=F32)
            a = (g * jax.nn.sigmoid(g) * u).astype(BF16)
            part = jnp.dot(a, wdn_ref[lo:lo + FFN_CHUNK, :], preferred_element_type=F32)
            acc = part if acc is None else acc + part
        x2 = x1 + mod[5:6] * acc
        if final:
            x2 = _rms(x2, fw_ref[...])
        o_ref[0, r0:r0 + OUT_ROWS, :] = x2


def _out_ffn(x, ya, yb, yc, mod, gwa, gwb, wout, n2w, wgu, wdn, fw, tm, layer, final):
    b, s, d = x.shape
    tok = lambda w: pl.BlockSpec((1, tm, w), lambda i, j: (i, j, 0))
    full = lambda a: pl.BlockSpec(a.shape, lambda i, j: (0,) * a.ndim, pipeline_mode=pl.Buffered(1))
    stacked = lambda a: pl.BlockSpec((None,) + a.shape[1:], lambda i, j: (layer,) + (0,) * (a.ndim - 1),
                                     pipeline_mode=pl.Buffered(1))
    return pl.pallas_call(
        functools.partial(_out_kernel, final=final),
        grid=(b, s // tm),
        in_specs=[
            tok(d), tok(A_WIDTH), tok(B_WIDTH), tok(C_WIDTH),
            pl.BlockSpec((1, N_MOD, d), lambda i, j: (i, 0, 0)),
            full(gwa), full(gwb), stacked(wout), full(n2w), stacked(wgu), stacked(wdn), full(fw),
        ],
        out_specs=tok(d),
        out_shape=jax.ShapeDtypeStruct((b, s, d), F32),
        compiler_params=pltpu.CompilerParams(
            dimension_semantics=("arbitrary", "arbitrary"), vmem_limit_bytes=VMEM_LIMIT),
        name="out_ffn",
    )(x, ya, yb, yc, mod, gwa, gwb, wout, n2w, wgu, wdn, fw)


def _in_col_index():
    src_aq, src_ak, src_av = 0, A_WIDTH, A_WIDTH + A_KV_WIDTH
    src_bcq = src_av + A_KV_WIDTH
    src_bckv = src_bcq + B_Q_RANK
    src_bkr = src_bckv + B_KV_RANK
    src_cu = src_bkr + B_ROPE
    src_cv = src_cu + C_WIDTH
    zero = src_cv + C_WIDTH
    idx = []
    for hh in A_HEAD_ORDER:
        idx += list(range(src_aq + hh * HEAD_DIM, src_aq + (hh + 1) * HEAD_DIM))
    idx += list(range(src_ak, src_bkr))
    idx += list(range(src_cu, zero))
    idx += [zero] * B_NOPE + list(range(src_bkr, src_bkr + B_ROPE)) + [zero] * (LANES - B_NOPE - B_ROPE)
    assert len(idx) == IN_COLS_PAD
    return np.asarray(idx, np.int32)


def _take_static(a, idx, axis):
    n = a.shape[axis]
    idx = [int(t) for t in idx]
    parts, start = [], 0
    for pos in range(1, len(idx) + 1):
        if pos < len(idx):
            prev, cur = idx[pos - 1], idx[pos]
            same_run = (prev == n and cur == n) or (prev != n and cur != n and cur == prev + 1)
        else:
            same_run = False
        if not same_run:
            first, count = idx[start], pos - start
            if first == n:
                shape = a.shape[:axis] + (count,) + a.shape[axis + 1:]
                parts.append(jnp.zeros(shape, a.dtype))
            else:
                parts.append(lax.slice_in_dim(a, first, first + count, axis=axis))
            start = pos
    return jnp.concatenate(parts, axis=axis)


def _rope_tables(positions):
    b, s = positions.shape
    pos = positions.astype(F32)[..., None]

    def tables(dim):
        half = dim // 2
        per_row = LANES // half
        inv = 1.0 / (ROPE_THETA ** (jnp.arange(0, dim, 2, dtype=F32) / dim))
        ang = lax.optimization_barrier(
            (pos.reshape(b, s // per_row, per_row, 1) * inv).reshape(b, s // per_row, LANES))
        cos, sin = lax.optimization_barrier((jnp.cos(ang), jnp.sin(ang)))
        return cos.reshape(b, s, half), sin.reshape(b, s, half)

    cos_a, sin_a = tables(HEAD_DIM)
    ca = jnp.concatenate([cos_a] * 4, axis=-1)
    sa = jnp.concatenate([-sin_a, sin_a] * 2, axis=-1)
    cos_b, sin_b = tables(B_ROPE)
    ones = jnp.ones(pos.shape[:-1] + (B_NOPE,), F32)
    zeros = jnp.zeros(pos.shape[:-1] + (LANES - B_NOPE - B_ROPE,), F32)
    cb = jnp.concatenate([ones, cos_b, cos_b, zeros], axis=-1)
    sb = jnp.concatenate([0.0 * ones, -sin_b, sin_b, zeros], axis=-1)
    return ca, sa, cb, sb


def kernel(x, c, positions, ada_w, ada_b, norm1_w, w_in, a_sinks, b_q_norm_w, b_w_uq, b_kv_norm_w, b_w_ukv,
           c_ln_w, c_ln_b, c_w_s, c_b_s, out_norm_w, w_out, norm2_w, w_gate_up, w_down, final_norm_w):
    depth = w_in.shape[0]
    b, s, d = x.shape
    ffn = w_down.shape[1]
    tm_in, tm_out, tq = min(IN_TILE, s), min(OUT_TILE, s), min(ATTN_TILE, s)

    mod = _modulation(c, ada_w, ada_b).reshape(depth, b, N_MOD, d)
    tabs = _rope_tables(positions)

    col_scale = np.ones((IN_COLS_PAD,), np.float32)
    col_scale[OFF_AQ:OFF_AQ + A_WIDTH] = HEAD_DIM ** -0.5
    w_in_p = (_take_static(w_in, _in_col_index(), 2) * col_scale).astype(BF16)

    w_uq = b_w_uq.reshape(depth, B_Q_RANK, B_HEADS, B_NOPE + B_ROPE)
    w_uq = jnp.pad(w_uq, ((0, 0), (0, 0), (0, 0), (0, B_QK_PAD - B_NOPE - B_ROPE)))
    w_uq = w_uq.reshape(depth, B_Q_RANK, B_HEADS * B_QK_PAD).astype(BF16)
    w_ukv = b_w_ukv.reshape(depth, B_KV_RANK, B_HEADS, B_NOPE + B_V)
    w_uk = jnp.pad(w_ukv[..., :B_NOPE], ((0, 0), (0, 0), (0, 0), (0, B_QK_PAD - B_NOPE)))
    w_ukv = jnp.concatenate([w_uk.reshape(depth, B_KV_RANK, B_HEADS * B_QK_PAD),
                             w_ukv[..., B_NOPE:].reshape(depth, B_KV_RANK, B_WIDTH)], axis=-1).astype(BF16)

    bs_full = jnp.repeat(jnp.swapaxes(c_b_s, 1, 2), C_GROUP_DIM, axis=-1)
    order = np.asarray(A_HEAD_ORDER)
    sink_tab = jnp.broadcast_to((a_sinks * LOG2E)[:, order, None], (depth, A_Q_HEADS, LANES))
    sink_tab = jnp.pad(sink_tab, ((0, 0), (0, 8 - A_Q_HEADS), (0, 0)))

    a_perm = (order[:, None] * HEAD_DIM + np.arange(HEAD_DIM)[None, :]).reshape(-1)
    out_perm = np.concatenate([a_perm, np.arange(A_WIDTH, d)])
    gw = _take_static(out_norm_w, out_perm, 1)
    w_out_p = _take_static(w_out, out_perm, 1).astype(BF16)

    assert ffn % FFN_CHUNK == 0
    w_gu = w_gate_up.astype(BF16)
    w_dn = w_down.astype(BF16)

    row = lambda a: a.reshape(1, -1)
    for l in range(depth):
        ya, qb, kb, vb, yc = _in_proj(
            x, mod[l], row(norm1_w[l]), w_in_p, w_uq, w_ukv, row(b_q_norm_w[l]), row(b_kv_norm_w[l]),
            tabs, row(c_ln_w[l]), row(c_ln_b[l]), c_w_s, bs_full, row(gw[l, A_WIDTH + B_WIDTH:]), sink_tab,
            tm_in, l)
        yb = _mla(qb, kb, vb, tq)
        x = _out_ffn(x, ya, yb, yc, mod[l], row(gw[l, :A_WIDTH]), row(gw[l, A_WIDTH:A_WIDTH + B_WIDTH]),
                     w_out_p, row(norm2_w[l]), w_gu, w_dn, row(final_norm_w), tm_out, l, l == depth - 1)
    return x
```

```python
import functools

import numpy as np
import jax
import jax.numpy as jnp
from jax import lax
from jax.experimental import pallas as pl
from jax.experimental.pallas import tpu as pltpu

F32 = jnp.float32
BF16 = jnp.bfloat16

LANES = 128
HEAD_DIM = 64
ROPE_THETA = 10000.0
NORM_EPS = 1e-6
NEG_INF = -1e30
LOG2E = float(np.log2(np.e))
BLOCK = 128

A_Q_HEADS = 6
A_KV_HEADS = 2
A_WIDTH = A_Q_HEADS * HEAD_DIM
A_KV_WIDTH = A_KV_HEADS * HEAD_DIM
A_HEAD_ORDER = (0, 3, 1, 4, 2, 5)

B_HEADS = 6
B_Q_RANK = 384
B_KV_RANK = 256
B_NOPE = 64
B_ROPE = 32
B_V = 64
B_WIDTH = B_HEADS * B_V
B_QK_PAD = LANES

C_GROUPS = 4
C_GROUP_DIM = 64
C_WIDTH = C_GROUPS * C_GROUP_DIM

N_MOD = 6

OFF_AQ = 0
OFF_AK = OFF_AQ + A_WIDTH
OFF_AV = OFF_AK + A_KV_WIDTH
OFF_BCQ = OFF_AV + A_KV_WIDTH
OFF_BCKV = OFF_BCQ + B_Q_RANK
OFF_CU = OFF_BCKV + B_KV_RANK
OFF_CV = OFF_CU + C_WIDTH
OFF_BKR = OFF_CV + C_WIDTH
IN_COLS_PAD = OFF_BKR + LANES

VMEM_LIMIT = 56 * 1024 * 1024
MLA_KV_SEG = 2048
IN_ROWS = 256
IN_TILE = 1024
IN_COL_CHUNK = 512
OUT_TILE = 1024
OUT_ROWS = 512
ATTN_TILE = 2048
MLA_ROWS = 256
FFN_CHUNK = 256


def _rms(x, w):
    return x * lax.rsqrt(jnp.mean(x * x, axis=-1, keepdims=True) + NORM_EPS) * w


def _gelu(x):
    return 0.5 * x * (1.0 + lax.erf(x * float(np.sqrt(0.5))))


def _lane_iota(shape):
    return lax.broadcasted_iota(jnp.int32, shape, len(shape) - 1)


def _mod_kernel(c_ref, w_ref, b_ref, o_ref):
    c = c_ref[...]
    act = (c * jax.nn.sigmoid(c)).astype(BF16)
    o_ref[0] = jnp.dot(act, w_ref[0].astype(BF16), preferred_element_type=F32) + b_ref[0]


def _modulation(c, ada_w, ada_b):
    depth, d, n = ada_w.shape
    b = c.shape[0]
    tn = 1536
    return pl.pallas_call(
        _mod_kernel,
        grid=(depth, n // tn),
        in_specs=[
            pl.BlockSpec((b, d), lambda l, j: (0, 0)),
            pl.BlockSpec((1, d, tn), lambda l, j: (l, 0, j)),
            pl.BlockSpec((1, 1, tn), lambda l, j: (l, 0, j)),
        ],
        out_specs=pl.BlockSpec((1, b, tn), lambda l, j: (l, 0, j)),
        out_shape=jax.ShapeDtypeStruct((depth, b, n), F32),
        compiler_params=pltpu.CompilerParams(
            dimension_semantics=("arbitrary", "arbitrary"), vmem_limit_bytes=VMEM_LIMIT),
        name="modulation",
    )(c, ada_w, ada_b.reshape(depth, 1, n))


def _swa_scores(q_pairs, kcat):
    low = _lane_iota((BLOCK, LANES)) < HEAD_DIM
    out = []
    for half in range(2):
        keep = low if half == 0 else jnp.logical_not(low)
        qs = jnp.concatenate([jnp.where(keep, qp, jnp.zeros_like(qp)) for qp in q_pairs], axis=0)
        out.append(lax.dot_general(qs, kcat, (((1,), (1,)), ((), ())), preferred_element_type=F32))
    return out


def _in_kernel(x_ref, mod_ref, n1w_ref, win_ref, wuq_ref, wukv_ref, qnw_ref, kvnw_ref,
               ca_ref, sa_ref, cb_ref, sb_ref, lnw_ref, lnb_ref, ws_ref, bs_ref, gwc_ref, sink_ref,
               ya_ref, qb_ref, kb_ref, vb_ref, yc_ref, kprev_ref, vprev_ref, band_ref):
    tm = x_ref.shape[1]
    mod = mod_ref[0]
    n1w = n1w_ref[...]
    lane = _lane_iota((IN_ROWS, LANES))
    a_first = (lane % HEAD_DIM) < (HEAD_DIM // 2)
    b_first = lane < (B_NOPE + B_ROPE // 2)

    first_tile = pl.program_id(1) == 0

    @pl.when(first_tile)
    def _():
        kprev_ref[...] = jnp.zeros_like(kprev_ref)
        vprev_ref[...] = jnp.zeros_like(vprev_ref)

    pairs = A_WIDTH // LANES
    qi = lax.broadcasted_iota(jnp.int32, (pairs * BLOCK, 2 * BLOCK), 0) % BLOCK
    kj = lax.broadcasted_iota(jnp.int32, (pairs * BLOCK, 2 * BLOCK), 1)
    rel = qi + BLOCK - kj
    band = (rel >= 0) & (rel < BLOCK)
    band_ref[...] = jnp.where(band, 0.0, NEG_INF)
    sinks = [jnp.concatenate([jnp.broadcast_to(sink_ref[2 * p + half:2 * p + half + 1, :], (BLOCK, LANES))
                              for p in range(pairs)], axis=0) for half in range(2)]
    ones_blk = jnp.ones((2 * BLOCK, LANES), BF16)
    low_a = _lane_iota((BLOCK, LANES)) < HEAD_DIM
    k_last = kprev_ref[...]
    v_last = vprev_ref[...]
    b_scale = float((B_NOPE + B_ROPE) ** -0.5) * LOG2E
    row = lax.broadcasted_iota(jnp.int32, (BLOCK, BLOCK), 0)
    col = lax.broadcasted_iota(jnp.int32, (BLOCK, BLOCK), 1)
    w_s = [jnp.where(col <= row, ws_ref[g], 0.0).astype(BF16) for g in range(C_GROUPS)]
    low = _lane_iota((BLOCK, LANES)) < C_GROUP_DIM
    bias = bs_ref[...]
    gwc = gwc_ref[...]

    def normed(r0):
        x = x_ref[0, r0:r0 + IN_ROWS, :]
        return (_rms(x, n1w) * (1.0 + mod[1:2]) + mod[0:1]).astype(BF16)

    n_col_chunks = -(-IN_COLS_PAD // IN_COL_CHUNK)

    def project_chunk(h, c):
        lo = c * IN_COL_CHUNK
        return jnp.dot(h, win_ref[:, lo:min(lo + IN_COL_CHUNK, IN_COLS_PAD)], preferred_element_type=F32)

    starts = list(range(0, tm, IN_ROWS))
    h_first = normed(starts[0])
    cur = [project_chunk(h_first, c) for c in range(n_col_chunks)]
    for idx, r0 in enumerate(starts):
        has_next = idx + 1 < len(starts)
        h_next = normed(starts[idx + 1]) if has_next else None
        nxt = []

        def ahead():
            if has_next and len(nxt) < n_col_chunks:
                nxt.append(project_chunk(h_next, len(nxt)))

        def cols(lo, width, chunks=cur):
            c = lo // IN_COL_CHUNK
            assert (lo + width - 1) // IN_COL_CHUNK == c
            return chunks[c][:, lo - c * IN_COL_CHUNK:lo + width - c * IN_COL_CHUNK]

        rs = slice(r0, r0 + IN_ROWS)
        ahead()

        ca = ca_ref[0, rs, :]
        sa = sa_ref[0, rs, :]

        def rope_a(t):
            sw = jnp.where(a_first, pltpu.roll(t, LANES - HEAD_DIM // 2, 1), pltpu.roll(t, HEAD_DIM // 2, 1))
            return t * ca + sw * sa

        q_pairs = [(rope_a(cols(OFF_AQ + p * LANES, LANES)) * LOG2E).astype(BF16) for p in range(pairs)]
        ka = rope_a(cols(OFF_AK, A_KV_WIDTH)).astype(BF16)
        va = cols(OFF_AV, A_KV_WIDTH).astype(BF16)

        n_blk = IN_ROWS // BLOCK
        kcats, vcats, scs = [], [], []
        for nb in range(n_blk):
            if nb == 0:
                kcats.append(jnp.concatenate([k_last, ka[0:BLOCK]], axis=0))
                vcats.append(jnp.concatenate([v_last, va[0:BLOCK]], axis=0))
            else:
                kcats.append(ka[(nb - 1) * BLOCK:(nb + 1) * BLOCK])
                vcats.append(va[(nb - 1) * BLOCK:(nb + 1) * BLOCK])
            scs.append(_swa_scores([qp[nb * BLOCK:(nb + 1) * BLOCK] for qp in q_pairs], kcats[nb]))
        k_last = ka[IN_ROWS - BLOCK:]
        v_last = va[IN_ROWS - BLOCK:]

        cb = cb_ref[0, rs, :]
        sb = sb_ref[0, rs, :]

        def rope_b(t):
            sw = jnp.where(b_first, pltpu.roll(t, LANES - B_ROPE // 2, 1), pltpu.roll(t, B_ROPE // 2, 1))
            return t * cb + sw * sb

        ahead()
        cq = _rms(cols(OFF_BCQ, B_Q_RANK), qnw_ref[...]).astype(BF16)
        qb = jnp.dot(cq, wuq_ref[...], preferred_element_type=F32)
        ckv = _rms(cols(OFF_BCKV, B_KV_RANK), kvnw_ref[...]).astype(BF16)
        kv = jnp.dot(ckv, wukv_ref[...], preferred_element_type=F32)

        ms = []
        for nb in range(n_blk):
            for half in range(2):
                if idx == 0 and nb == 0:
                    seen = band & (kj >= jnp.where(first_tile, BLOCK, 0))
                    scs[nb][half] = jnp.where(seen, scs[nb][half], NEG_INF)
                else:
                    scs[nb][half] = scs[nb][half] + band_ref[...]
            ms.append([jnp.maximum(jnp.max(scs[nb][half], axis=-1, keepdims=True), sinks[half])
                       for half in range(2)])

        ahead()
        kr = rope_b(cols(OFF_BKR, LANES))
        for hh in range(B_HEADS):
            sl = slice(hh * LANES, (hh + 1) * LANES)
            qb_ref[0, rs, sl] = (rope_b(qb[:, sl]) * b_scale).astype(BF16)
            kb_ref[0, rs, sl] = (kv[:, sl] + kr).astype(BF16)
        vb_ref[0, rs, :] = kv[:, B_HEADS * LANES:].astype(BF16)

        pvs = []
        for nb in range(n_blk):
            pvs.append([])
            for half in range(2):
                e = jnp.concatenate([jnp.exp2(scs[nb][half][:, t * LANES:(t + 1) * LANES] - ms[nb][half])
                                     for t in range(2 * BLOCK // LANES)], axis=1).astype(BF16)
                pvs[nb].append(jnp.dot(e, jnp.concatenate([vcats[nb], ones_blk], axis=1),
                                       preferred_element_type=F32))

        ahead()
        u = _gelu(cols(OFF_CU, C_WIDTH))
        v = _gelu(cols(OFF_CV, C_WIDTH))
        mu = jnp.mean(v, axis=-1, keepdims=True)
        vc = v - mu
        var = jnp.mean(vc * vc, axis=-1, keepdims=True)
        v = (vc * lax.rsqrt(var + NORM_EPS) * lnw_ref[...] + lnb_ref[...]).astype(BF16)
        for c in range(IN_ROWS // BLOCK):
            rows = slice(c * BLOCK, (c + 1) * BLOCK)
            parts = []
            for p in range(C_WIDTH // LANES):
                vp = v[rows, p * LANES:(p + 1) * LANES]
                r_lo = jnp.dot(w_s[2 * p], vp, preferred_element_type=F32)
                r_hi = jnp.dot(w_s[2 * p + 1], vp, preferred_element_type=F32)
                parts.append(jnp.where(low, r_lo, r_hi))
            mixed = jnp.concatenate(parts, axis=-1) + bias
            yc = u[rows] * mixed
            yc_ref[0, r0 + c * BLOCK:r0 + (c + 1) * BLOCK, :] = _rms(yc, gwc).astype(BF16)

        for nb in range(n_blk):
            outs = [pvs[nb][half][:, :LANES] / (pvs[nb][half][:, LANES:] + jnp.exp2(sinks[half] - ms[nb][half]))
                    for half in range(2)]
            for p in range(pairs):
                blk = slice(p * BLOCK, (p + 1) * BLOCK)
                ya_ref[0, r0 + nb * BLOCK:r0 + (nb + 1) * BLOCK, p * LANES:(p + 1) * LANES] = jnp.where(
                    low_a, outs[0][blk], outs[1][blk])

        while has_next and len(nxt) < n_col_chunks:
            ahead()
        cur = nxt

    kprev_ref[...] = k_last
    vprev_ref[...] = v_last


def _in_proj(x, mod, n1w, win, wuq, wukv, qnw, kvnw, tabs, lnw, lnb, ws, bs_full, gwc, sink_tab, tm, layer):
    b, s, d = x.shape
    tok = lambda w: pl.BlockSpec((1, tm, w), lambda i, j: (i, j, 0))
    full = lambda a: pl.BlockSpec(a.shape, lambda i, j: (0,) * a.ndim)
    stacked = lambda a: pl.BlockSpec((None,) + a.shape[1:], lambda i, j: (layer,) + (0,) * (a.ndim - 1))
    ca, sa, cb, sb = tabs
    out_widths = (A_WIDTH, B_HEADS * B_QK_PAD, B_HEADS * B_QK_PAD, B_WIDTH, C_WIDTH)
    out_specs = [tok(w) for w in out_widths]
    out_shape = [jax.ShapeDtypeStruct((b, s, w), F32 if n == 0 else BF16) for n, w in enumerate(out_widths)]
    return pl.pallas_call(
        _in_kernel,
        grid=(b, s // tm),
        in_specs=[
            tok(d),
            pl.BlockSpec((1, N_MOD, d), lambda i, j: (i, 0, 0)),
            full(n1w), stacked(win), stacked(wuq), stacked(wukv), full(qnw), full(kvnw),
            tok(LANES), tok(LANES), tok(LANES), tok(LANES),
            full(lnw), full(lnb), stacked(ws), stacked(bs_full), full(gwc), stacked(sink_tab),
        ],
        out_specs=out_specs,
        out_shape=out_shape,
        scratch_shapes=[pltpu.VMEM((BLOCK, A_KV_WIDTH), BF16), pltpu.VMEM((BLOCK, A_KV_WIDTH), BF16),
                        pltpu.VMEM((A_WIDTH // LANES * BLOCK, 2 * BLOCK), F32)],
        compiler_params=pltpu.CompilerParams(
            dimension_semantics=("arbitrary", "arbitrary"), vmem_limit_bytes=VMEM_LIMIT),
        name="in_proj",
    )(x, mod, n1w, win, wuq, wukv, qnw, kvnw, ca, sa, cb, sb, lnw, lnb, ws, bs_full, gwc, sink_tab)


def _mla_tile(q_ref, k_ref, v_ref, o_ref, tile):
    tq = q_ref.shape[1]
    rows = MLA_ROWS
    qi = lax.broadcasted_iota(jnp.int32, (rows, rows), 0)
    kj = lax.broadcasted_iota(jnp.int32, (rows, rows), 1)
    causal = kj <= qi
    nt = (((1,), (1,)), ((), ()))

    subs = list(range(tq // rows))
    segments = {}
    for sub in subs:
        first_row = tile * tq + sub * rows
        segments[sub] = [(lo, min(MLA_KV_SEG, first_row - lo), False) for lo in range(0, first_row, MLA_KV_SEG)]
        segments[sub].append((first_row, rows, True))
    chains = [(sub, hh) for sub in subs for hh in range(2)]
    state = {ch: None for ch in chains}

    for rnd in range(max(len(s) for s in segments.values())):
        live = [ch for ch in chains if rnd < len(segments[ch[0]])]
        v_aug = {}
        for sub in sorted({ch[0] for ch in live}):
            lo, width, _ = segments[sub][rnd]
            v = v_ref[0, lo:lo + width, :]
            low_v = _lane_iota(v.shape) < B_V
            one = jnp.ones_like(v)
            v_aug[sub] = (jnp.where(low_v, v, one), jnp.where(low_v, one, v))
        sc, m_new = {}, {}
        for sub, hh in live:
            lo, width, _ = segments[sub][rnd]
            q = q_ref[0, sub * rows:(sub + 1) * rows, hh * LANES:(hh + 1) * LANES]
            k = k_ref[0, lo:lo + width, hh * LANES:(hh + 1) * LANES]
            sc[sub, hh] = lax.dot_general(q, k, nt, preferred_element_type=F32)
        for ch in live:
            if segments[ch[0]][rnd][2]:
                sc[ch] = jnp.where(causal, sc[ch], NEG_INF)
            m_blk = jnp.max(sc[ch], axis=-1, keepdims=True)
            m_new[ch] = m_blk if state[ch] is None else jnp.maximum(state[ch][0], m_blk)
        pv = {ch: jnp.dot(jnp.exp2(sc[ch] - m_new[ch]).astype(BF16), v_aug[ch[0]][ch[1]],
                          preferred_element_type=F32) for ch in live}
        for ch in live:
            if state[ch] is None:
                acc = pv[ch]
            else:
                m, acc = state[ch]
                acc = jnp.exp2(m - m_new[ch]) * acc + pv[ch]
            state[ch] = (m_new[ch], acc)

    low = _lane_iota((rows, LANES)) < B_V
    for sub in subs:
        acc0, acc1 = state[sub, 0][1], state[sub, 1][1]
        num = jnp.where(low, acc0, acc1)
        den = pltpu.roll(jnp.where(low, acc1, acc0), B_V, 1)
        o_ref[0, sub * rows:(sub + 1) * rows, :] = num / den


def _mla_kernel(q_ref, k_ref, v_ref, o_ref):
    i = pl.program_id(2)
    for tile in range(k_ref.shape[1] // q_ref.shape[1]):
        pl.when(i == tile)(functools.partial(_mla_tile, q_ref, k_ref, v_ref, o_ref, tile))


def _mla(qb, kb, vb, tq):
    b, s, _ = qb.shape
    pairs = B_HEADS // 2
    return pl.pallas_call(
        _mla_kernel,
        grid=(b, pairs, s // tq),
        in_specs=[
            pl.BlockSpec((1, tq, 2 * B_QK_PAD), lambda i, p, j: (i, j, p)),
            pl.BlockSpec((1, s, 2 * B_QK_PAD), lambda i, p, j: (i, 0, p)),
            pl.BlockSpec((1, s, LANES), lambda i, p, j: (i, 0, p)),
        ],
        out_specs=pl.BlockSpec((1, tq, LANES), lambda i, p, j: (i, j, p)),
        out_shape=jax.ShapeDtypeStruct((b, s, B_WIDTH), F32),
        compiler_params=pltpu.CompilerParams(
            dimension_semantics=("arbitrary", "arbitrary", "arbitrary"), vmem_limit_bytes=VMEM_LIMIT),
        name="mla",
    )(qb, kb, vb)


def _out_kernel(x_ref, ya_ref, yb_ref, yc_ref, mod_ref, gwa_ref, gwb_ref, wout_ref, n2w_ref,
                wgu_ref, wdn_ref, fw_ref, o_ref, *, final):
    tm = x_ref.shape[1]
    mod = mod_ref[0]
    ffn = wdn_ref.shape[0]

    def mix(r0):
        rs = slice(r0, r0 + OUT_ROWS)
        y = jnp.concatenate([
            _rms(ya_ref[0, rs, :], gwa_ref[...]).astype(BF16),
            _rms(yb_ref[0, rs, :], gwb_ref[...]).astype(BF16),
            yc_ref[0, rs, :],
        ], axis=-1)
        x1 = x_ref[0, rs, :] + mod[2:3] * jnp.dot(y, wout_ref[...], preferred_element_type=F32)
        h = (_rms(x1, n2w_ref[...]) * (1.0 + mod[4:5]) + mod[3:4]).astype(BF16)
        return x1, h

    starts = list(range(0, tm, OUT_ROWS))
    mixed = {starts[0]: mix(starts[0])}
    for idx, r0 in enumerate(starts):
        if idx + 1 < len(starts):
            mixed[starts[idx + 1]] = mix(starts[idx + 1])
        x1, h = mixed.pop(r0)
        acc = None
        for lo in range(0, ffn, FFN_CHUNK):
            g = jnp.dot(h, wgu_ref[:, lo:lo + FFN_CHUNK], preferred_element_type=F32)
            u = jnp.dot(h, wgu_ref[:, ffn + lo:ffn + lo + FFN_CHUNK], preferred_element_type=F32)
            a = (g * jax.nn.sigmoid(g) * u).astype(BF16)
            part = jnp.dot(a, wdn_ref[lo:lo + FFN_CHUNK, :], preferred_element_type=F32)
            acc = part if acc is None else acc + part
        x2 = x1 + mod[5:6] * acc
        if final:
            x2 = _rms(x2, fw_ref[...])
        o_ref[0, r0:r0 + OUT_ROWS, :] = x2


def _out_ffn(x, ya, yb, yc, mod, gwa, gwb, wout, n2w, wgu, wdn, fw, tm, layer, final):
    b, s, d = x.shape
    tok = lambda w: pl.BlockSpec((1, tm, w), lambda i, j: (i, j, 0))
    full = lambda a: pl.BlockSpec(a.shape, lambda i, j: (0,) * a.ndim, pipeline_mode=pl.Buffered(1))
    stacked = lambda a: pl.BlockSpec((None,) + a.shape[1:], lambda i, j: (layer,) + (0,) * (a.ndim - 1),
                                     pipeline_mode=pl.Buffered(1))
    return pl.pallas_call(
        functools.partial(_out_kernel, final=final),
        grid=(b, s // tm),
        in_specs=[
            tok(d), tok(A_WIDTH), tok(B_WIDTH), tok(C_WIDTH),
            pl.BlockSpec((1, N_MOD, d), lambda i, j: (i, 0, 0)),
            full(gwa), full(gwb), stacked(wout), full(n2w), stacked(wgu), stacked(wdn), full(fw),
        ],
        out_specs=tok(d),
        out_shape=jax.ShapeDtypeStruct((b, s, d), F32),
        compiler_params=pltpu.CompilerParams(
            dimension_semantics=("arbitrary", "arbitrary"), vmem_limit_bytes=VMEM_LIMIT),
        name="out_ffn",
    )(x, ya, yb, yc, mod, gwa, gwb, wout, n2w, wgu, wdn, fw)


def _in_col_index():
    src_aq, src_ak, src_av = 0, A_WIDTH, A_WIDTH + A_KV_WIDTH
    src_bcq = src_av + A_KV_WIDTH
    src_bckv = src_bcq + B_Q_RANK
    src_bkr = src_bckv + B_KV_RANK
    src_cu = src_bkr + B_ROPE
    src_cv = src_cu + C_WIDTH
    zero = src_cv + C_WIDTH
    idx = []
    for hh in A_HEAD_ORDER:
        idx += list(range(src_aq + hh * HEAD_DIM, src_aq + (hh + 1) * HEAD_DIM))
    idx += list(range(src_ak, src_bkr))
    idx += list(range(src_cu, zero))
    idx += [zero] * B_NOPE + list(range(src_bkr, src_bkr + B_ROPE)) + [zero] * (LANES - B_NOPE - B_ROPE)
    assert len(idx) == IN_COLS_PAD
    return np.asarray(idx, np.int32)


def _take_static(a, idx, axis):
    n = a.shape[axis]
    idx = [int(t) for t in idx]
    parts, start = [], 0
    for pos in range(1, len(idx) + 1):
        if pos < len(idx):
            prev, cur = idx[pos - 1], idx[pos]
            same_run = (prev == n and cur == n) or (prev != n and cur != n and cur == prev + 1)
        else:
            same_run = False
        if not same_run:
            first, count = idx[start], pos - start
            if first == n:
                shape = a.shape[:axis] + (count,) + a.shape[axis + 1:]
                parts.append(jnp.zeros(shape, a.dtype))
            else:
                parts.append(lax.slice_in_dim(a, first, first + count, axis=axis))
            start = pos
    return jnp.concatenate(parts, axis=axis)


def _rope_tables(positions):
    b, s = positions.shape
    pos = positions.astype(F32)[..., None]

    def tables(dim):
        half = dim // 2
        per_row = LANES // half
        inv = 1.0 / (ROPE_THETA ** (jnp.arange(0, dim, 2, dtype=F32) / dim))
        ang = lax.optimization_barrier(
            (pos.reshape(b, s // per_row, per_row, 1) * inv).reshape(b, s // per_row, LANES))
        cos, sin = lax.optimization_barrier((jnp.cos(ang), jnp.sin(ang)))
        return cos.reshape(b, s, half), sin.reshape(b, s, half)

    cos_a, sin_a = tables(HEAD_DIM)
    ca = jnp.concatenate([cos_a] * 4, axis=-1)
    sa = jnp.concatenate([-sin_a, sin_a] * 2, axis=-1)
    cos_b, sin_b = tables(B_ROPE)
    ones = jnp.ones(pos.shape[:-1] + (B_NOPE,), F32)
    zeros = jnp.zeros(pos.shape[:-1] + (LANES - B_NOPE - B_ROPE,), F32)
    cb = jnp.concatenate([ones, cos_b, cos_b, zeros], axis=-1)
    sb = jnp.concatenate([0.0 * ones, -sin_b, sin_b, zeros], axis=-1)
    return ca, sa, cb, sb


def kernel(x, c, positions, ada_w, ada_b, norm1_w, w_in, a_sinks, b_q_norm_w, b_w_uq, b_kv_norm_w, b_w_ukv,
           c_ln_w, c_ln_b, c_w_s, c_b_s, out_norm_w, w_out, norm2_w, w_gate_up, w_down, final_norm_w):
    depth = w_in.shape[0]
    b, s, d = x.shape
    ffn = w_down.shape[1]
    tm_in, tm_out, tq = min(IN_TILE, s), min(OUT_TILE, s), min(ATTN_TILE, s)

    mod = _modulation(c, ada_w, ada_b).reshape(depth, b, N_MOD, d)
    tabs = _rope_tables(positions)

    col_scale = np.ones((IN_COLS_PAD,), np.float32)
    col_scale[OFF_AQ:OFF_AQ + A_WIDTH] = HEAD_DIM ** -0.5
    w_in_p = (_take_static(w_in, _in_col_index(), 2) * col_scale).astype(BF16)

    w_uq = b_w_uq.reshape(depth, B_Q_RANK, B_HEADS, B_NOPE + B_ROPE)
    w_uq = jnp.pad(w_uq, ((0, 0), (0, 0), (0, 0), (0, B_QK_PAD - B_NOPE - B_ROPE)))
    w_uq = w_uq.reshape(depth, B_Q_RANK, B_HEADS * B_QK_PAD).astype(BF16)
    w_ukv = b_w_ukv.reshape(depth, B_KV_RANK, B_HEADS, B_NOPE + B_V)
    w_uk = jnp.pad(w_ukv[..., :B_NOPE], ((0, 0), (0, 0), (0, 0), (0, B_QK_PAD - B_NOPE)))
    w_ukv = jnp.concatenate([w_uk.reshape(depth, B_KV_RANK, B_HEADS * B_QK_PAD),
                             w_ukv[..., B_NOPE:].reshape(depth, B_KV_RANK, B_WIDTH)], axis=-1).astype(BF16)

    bs_full = jnp.repeat(jnp.swapaxes(c_b_s, 1, 2), C_GROUP_DIM, axis=-1)
    order = np.asarray(A_HEAD_ORDER)
    sink_tab = jnp.broadcast_to((a_sinks * LOG2E)[:, order, None], (depth, A_Q_HEADS, LANES))
    sink_tab = jnp.pad(sink_tab, ((0, 0), (0, 8 - A_Q_HEADS), (0, 0)))

    a_perm = (order[:, None] * HEAD_DIM + np.arange(HEAD_DIM)[None, :]).reshape(-1)
    out_perm = np.concatenate([a_perm, np.arange(A_WIDTH, d)])
    gw = _take_static(out_norm_w, out_perm, 1)
    w_out_p = _take_static(w_out, out_perm, 1).astype(BF16)

    assert ffn % FFN_CHUNK == 0
    w_gu = w_gate_up.astype(BF16)
    w_dn = w_down.astype(BF16)

    row = lambda a: a.reshape(1, -1)
    for l in range(depth):
        ya, qb, kb, vb, yc = _in_proj(
            x, mod[l], row(norm1_w[l]), w_in_p, w_uq, w_ukv, row(b_q_norm_w[l]), row(b_kv_norm_w[l]),
            tabs, row(c_ln_w[l]), row(c_ln_b[l]), c_w_s, bs_full, row(gw[l, A_WIDTH + B_WIDTH:]), sink_tab,
            tm_in, l)
        yb = _mla(qb, kb, vb, tq)
        x = _out_ffn(x, ya, yb, yc, mod[l], row(gw[l, :A_WIDTH]), row(gw[l, A_WIDTH:A_WIDTH + B_WIDTH]),
                     w_out_p, row(norm2_w[l]), w_gu, w_dn, row(final_norm_w), tm_out, l, l == depth - 1)
    return x
```

```python
import functools

import numpy as np
import jax
import jax.numpy as jnp
from jax import lax
from jax.experimental import pallas as pl
from jax.experimental.pallas import tpu as pltpu

F32 = jnp.float32
BF16 = jnp.bfloat16

LANES = 128
HEAD_DIM = 64
ROPE_THETA = 10000.0
NORM_EPS = 1e-6
NEG_INF = -1e30
LOG2E = float(np.log2(np.e))
BLOCK = 128

A_Q_HEADS = 6
A_KV_HEADS = 2
A_WIDTH = A_Q_HEADS * HEAD_DIM
A_KV_WIDTH = A_KV_HEADS * HEAD_DIM
A_HEAD_ORDER = (0, 3, 1, 4, 2, 5)

B_HEADS = 6
B_Q_RANK = 384
B_KV_RANK = 256
B_NOPE = 64
B_ROPE = 32
B_V = 64
B_WIDTH = B_HEADS * B_V
B_QK_PAD = LANES

C_GROUPS = 4
C_GROUP_DIM = 64
C_WIDTH = C_GROUPS * C_GROUP_DIM

N_MOD = 6

OFF_AQ = 0
OFF_AK = OFF_AQ + A_WIDTH
OFF_AV = OFF_AK + A_KV_WIDTH
OFF_BCQ = OFF_AV + A_KV_WIDTH
OFF_BCKV = OFF_BCQ + B_Q_RANK
OFF_CU = OFF_BCKV + B_KV_RANK
OFF_CV = OFF_CU + C_WIDTH
OFF_BKR = OFF_CV + C_WIDTH
IN_COLS_PAD = OFF_BKR + LANES

VMEM_LIMIT = 56 * 1024 * 1024
MLA_KV_SEG = 2048
IN_ROWS = 256
IN_TILE = 1024
IN_COL_CHUNK = 512
OUT_TILE = 1024
OUT_ROWS = 512
ATTN_TILE = 2048
MLA_ROWS = 256
FFN_CHUNK = 256
MOD_COLS = 1536


def _rms(x, w):
    return x * lax.rsqrt(jnp.mean(x * x, axis=-1, keepdims=True) + NORM_EPS) * w


def _gelu(x):
    return 0.5 * x * (1.0 + lax.erf(x * float(np.sqrt(0.5))))


def _lane_iota(shape):
    return lax.broadcasted_iota(jnp.int32, shape, len(shape) - 1)


def _mod_kernel(c_ref, w_ref, b_ref, o_ref):
    c = c_ref[...]
    act = (c * jax.nn.sigmoid(c)).astype(BF16)
    o_ref[0] = jnp.dot(act, w_ref[0].astype(BF16), preferred_element_type=F32) + b_ref[0]


def _modulation(c, ada_w, ada_b):
    depth, d, n = ada_w.shape
    b = c.shape[0]
    tn = MOD_COLS
    return pl.pallas_call(
        _mod_kernel,
        grid=(depth, n // tn),
        in_specs=[
            pl.BlockSpec((b, d), lambda l, j: (0, 0)),
            pl.BlockSpec((1, d, tn), lambda l, j: (l, 0, j)),
            pl.BlockSpec((1, 1, tn), lambda l, j: (l, 0, j)),
        ],
        out_specs=pl.BlockSpec((1, b, tn), lambda l, j: (l, 0, j)),
        out_shape=jax.ShapeDtypeStruct((depth, b, n), F32),
        compiler_params=pltpu.CompilerParams(
            dimension_semantics=("arbitrary", "arbitrary"), vmem_limit_bytes=VMEM_LIMIT),
        name="modulation",
    )(c, ada_w, ada_b.reshape(depth, 1, n))


def _swa_scores(q_pairs, kcat):
    low = _lane_iota((BLOCK, LANES)) < HEAD_DIM
    out = []
    for half in range(2):
        keep = low if half == 0 else jnp.logical_not(low)
        qs = jnp.concatenate([jnp.where(keep, qp, jnp.zeros_like(qp)) for qp in q_pairs], axis=0)
        out.append(lax.dot_general(qs, kcat, (((1,), (1,)), ((), ())), preferred_element_type=F32))
    return out


def _in_kernel(x_ref, mod_ref, n1w_ref, win_ref, wuq_ref, wukv_ref, qnw_ref, kvnw_ref,
               ca_ref, sa_ref, cb_ref, sb_ref, lnw_ref, lnb_ref, ws_ref, bs_ref, gwc_ref, sink_ref,
               ya_ref, qb_ref, kb_ref, vb_ref, yc_ref, kprev_ref, vprev_ref, band_ref):
    tm = x_ref.shape[1]
    mod = mod_ref[0]
    n1w = n1w_ref[...] * (1.0 + mod[1:2])
    lane = _lane_iota((IN_ROWS, LANES))
    a_first = (lane % HEAD_DIM) < (HEAD_DIM // 2)
    b_first = lane < (B_NOPE + B_ROPE // 2)

    first_tile = pl.program_id(1) == 0

    @pl.when(first_tile)
    def _():
        kprev_ref[...] = jnp.zeros_like(kprev_ref)
        vprev_ref[...] = jnp.zeros_like(vprev_ref)

    pairs = A_WIDTH // LANES
    qi = lax.broadcasted_iota(jnp.int32, (pairs * BLOCK, 2 * BLOCK), 0) % BLOCK
    kj = lax.broadcasted_iota(jnp.int32, (pairs * BLOCK, 2 * BLOCK), 1)
    rel = qi + BLOCK - kj
    band = (rel >= 0) & (rel < BLOCK)
    band_ref[...] = jnp.where(band, 0.0, NEG_INF)
    sinks = [jnp.concatenate([jnp.broadcast_to(sink_ref[2 * p + half:2 * p + half + 1, :], (BLOCK, LANES))
                              for p in range(pairs)], axis=0) for half in range(2)]
    ones_blk = jnp.ones((2 * BLOCK, LANES), BF16)
    low_a = _lane_iota((BLOCK, LANES)) < HEAD_DIM
    k_last = kprev_ref[...]
    v_last = vprev_ref[...]
    b_scale = float((B_NOPE + B_ROPE) ** -0.5) * LOG2E
    row = lax.broadcasted_iota(jnp.int32, (BLOCK, BLOCK), 0)
    col = lax.broadcasted_iota(jnp.int32, (BLOCK, BLOCK), 1)
    w_s = [jnp.where(col <= row, ws_ref[g], 0.0).astype(BF16) for g in range(C_GROUPS)]
    low = _lane_iota((BLOCK, LANES)) < C_GROUP_DIM
    bias = bs_ref[...]
    gwc = gwc_ref[...]

    def normed(r0):
        x = x_ref[0, r0:r0 + IN_ROWS, :]
        return (_rms(x, n1w) + mod[0:1]).astype(BF16)

    n_col_chunks = -(-IN_COLS_PAD // IN_COL_CHUNK)

    def project_chunk(h, c):
        lo = c * IN_COL_CHUNK
        return jnp.dot(h, win_ref[:, lo:min(lo + IN_COL_CHUNK, IN_COLS_PAD)], preferred_element_type=F32)

    starts = list(range(0, tm, IN_ROWS))
    h_first = normed(starts[0])
    cur = [project_chunk(h_first, c) for c in range(n_col_chunks)]
    for idx, r0 in enumerate(starts):
        has_next = idx + 1 < len(starts)
        h_next = normed(starts[idx + 1]) if has_next else None
        nxt = []

        def ahead():
            if has_next and len(nxt) < n_col_chunks:
                nxt.append(project_chunk(h_next, len(nxt)))

        def cols(lo, width, chunks=cur):
            c = lo // IN_COL_CHUNK
            assert (lo + width - 1) // IN_COL_CHUNK == c
            return chunks[c][:, lo - c * IN_COL_CHUNK:lo + width - c * IN_COL_CHUNK]

        rs = slice(r0, r0 + IN_ROWS)
        ahead()

        ca = ca_ref[0, rs, :]
        sa = sa_ref[0, rs, :]

        def rope_a(t):
            sw = jnp.where(a_first, pltpu.roll(t, LANES - HEAD_DIM // 2, 1), pltpu.roll(t, HEAD_DIM // 2, 1))
            return t * ca + sw * sa

        q_pairs = [(rope_a(cols(OFF_AQ + p * LANES, LANES)) * LOG2E).astype(BF16) for p in range(pairs)]
        ka = rope_a(cols(OFF_AK, A_KV_WIDTH)).astype(BF16)
        va = cols(OFF_AV, A_KV_WIDTH).astype(BF16)

        n_blk = IN_ROWS // BLOCK
        kcats, vcats, scs = [], [], []
        for nb in range(n_blk):
            if nb == 0:
                kcats.append(jnp.concatenate([k_last, ka[0:BLOCK]], axis=0))
                vcats.append(jnp.concatenate([v_last, va[0:BLOCK]], axis=0))
            else:
                kcats.append(ka[(nb - 1) * BLOCK:(nb + 1) * BLOCK])
                vcats.append(va[(nb - 1) * BLOCK:(nb + 1) * BLOCK])
            scs.append(_swa_scores([qp[nb * BLOCK:(nb + 1) * BLOCK] for qp in q_pairs], kcats[nb]))
        k_last = ka[IN_ROWS - BLOCK:]
        v_last = va[IN_ROWS - BLOCK:]

        cb = cb_ref[0, rs, :]
        sb = sb_ref[0, rs, :]

        def rope_b(t):
            sw = jnp.where(b_first, pltpu.roll(t, LANES - B_ROPE // 2, 1), pltpu.roll(t, B_ROPE // 2, 1))
            return t * cb + sw * sb

        ahead()
        cq = _rms(cols(OFF_BCQ, B_Q_RANK), qnw_ref[...]).astype(BF16)
        qb = jnp.dot(cq, wuq_ref[...], preferred_element_type=F32)
        ckv = _rms(cols(OFF_BCKV, B_KV_RANK), kvnw_ref[...]).astype(BF16)
        kv = jnp.dot(ckv, wukv_ref[...], preferred_element_type=F32)

        ms = []
        for nb in range(n_blk):
            for half in range(2):
                if idx == 0 and nb == 0:
                    seen = band & (kj >= jnp.where(first_tile, BLOCK, 0))
                    scs[nb][half] = jnp.where(seen, scs[nb][half], NEG_INF)
                else:
                    scs[nb][half] = scs[nb][half] + band_ref[...]
            ms.append([jnp.maximum(jnp.max(scs[nb][half], axis=-1, keepdims=True), sinks[half])
                       for half in range(2)])

        ahead()
        kr = rope_b(cols(OFF_BKR, LANES))
        for hh in range(B_HEADS):
            sl = slice(hh * LANES, (hh + 1) * LANES)
            qb_ref[0, rs, sl] = (rope_b(qb[:, sl]) * b_scale).astype(BF16)
            kb_ref[0, rs, sl] = (kv[:, sl] + kr).astype(BF16)
        vb_ref[0, rs, :] = kv[:, B_HEADS * LANES:].astype(BF16)

        pvs = []
        for nb in range(n_blk):
            pvs.append([])
            for half in range(2):
                e = jnp.concatenate([jnp.exp2(scs[nb][half][:, t * LANES:(t + 1) * LANES] - ms[nb][half])
                                     for t in range(2 * BLOCK // LANES)], axis=1).astype(BF16)
                pvs[nb].append(jnp.dot(e, jnp.concatenate([vcats[nb], ones_blk], axis=1),
                                       preferred_element_type=F32))

        ahead()
        u = _gelu(cols(OFF_CU, C_WIDTH))
        v = _gelu(cols(OFF_CV, C_WIDTH))
        mu = jnp.mean(v, axis=-1, keepdims=True)
        vc = v - mu
        var = jnp.mean(vc * vc, axis=-1, keepdims=True)
        v = (vc * lax.rsqrt(var + NORM_EPS) * lnw_ref[...] + lnb_ref[...]).astype(BF16)
        for c in range(IN_ROWS // BLOCK):
            rows = slice(c * BLOCK, (c + 1) * BLOCK)
            parts = []
            for p in range(C_WIDTH // LANES):
                vp = v[rows, p * LANES:(p + 1) * LANES]
                r_lo = jnp.dot(w_s[2 * p], vp, preferred_element_type=F32)
                r_hi = jnp.dot(w_s[2 * p + 1], vp, preferred_element_type=F32)
                parts.append(jnp.where(low, r_lo, r_hi))
            mixed = jnp.concatenate(parts, axis=-1) + bias
            yc = u[rows] * mixed
            yc_ref[0, r0 + c * BLOCK:r0 + (c + 1) * BLOCK, :] = _rms(yc, gwc).astype(BF16)

        for nb in range(n_blk):
            outs = [pvs[nb][half][:, :LANES] / (pvs[nb][half][:, LANES:] + jnp.exp2(sinks[half] - ms[nb][half]))
                    for half in range(2)]
            for p in range(pairs):
                blk = slice(p * BLOCK, (p + 1) * BLOCK)
                ya_ref[0, r0 + nb * BLOCK:r0 + (nb + 1) * BLOCK, p * LANES:(p + 1) * LANES] = jnp.where(
                    low_a, outs[0][blk], outs[1][blk])

        while has_next and len(nxt) < n_col_chunks:
            ahead()
        cur = nxt

    kprev_ref[...] = k_last
    vprev_ref[...] = v_last


def _in_proj(x, mod, n1w, win, wuq, wukv, qnw, kvnw, tabs, lnw, lnb, ws, bs_full, gwc, sink_tab, tm, layer):
    b, s, d = x.shape
    tok = lambda w: pl.BlockSpec((1, tm, w), lambda i, j: (i, j, 0))
    full = lambda a: pl.BlockSpec(a.shape, lambda i, j: (0,) * a.ndim)
    stacked = lambda a: pl.BlockSpec((None,) + a.shape[1:], lambda i, j: (layer,) + (0,) * (a.ndim - 1))
    ca, sa, cb, sb = tabs
    out_widths = (A_WIDTH, B_HEADS * B_QK_PAD, B_HEADS * B_QK_PAD, B_WIDTH, C_WIDTH)
    out_specs = [tok(w) for w in out_widths]
    out_shape = [jax.ShapeDtypeStruct((b, s, w), F32 if n == 0 else BF16) for n, w in enumerate(out_widths)]
    return pl.pallas_call(
        _in_kernel,
        grid=(b, s // tm),
        in_specs=[
            tok(d),
            pl.BlockSpec((1, N_MOD, d), lambda i, j: (i, 0, 0)),
            full(n1w), stacked(win), stacked(wuq), stacked(wukv), full(qnw), full(kvnw),
            tok(LANES), tok(LANES), tok(LANES), tok(LANES),
            full(lnw), full(lnb), stacked(ws), stacked(bs_full), full(gwc), stacked(sink_tab),
        ],
        out_specs=out_specs,
        out_shape=out_shape,
        scratch_shapes=[pltpu.VMEM((BLOCK, A_KV_WIDTH), BF16), pltpu.VMEM((BLOCK, A_KV_WIDTH), BF16),
                        pltpu.VMEM((A_WIDTH // LANES * BLOCK, 2 * BLOCK), F32)],
        compiler_params=pltpu.CompilerParams(
            dimension_semantics=("arbitrary", "arbitrary"), vmem_limit_bytes=VMEM_LIMIT),
        name="in_proj",
    )(x, mod, n1w, win, wuq, wukv, qnw, kvnw, ca, sa, cb, sb, lnw, lnb, ws, bs_full, gwc, sink_tab)


def _mla_tile(q_ref, k_ref, v_ref, o_ref, tile):
    tq = q_ref.shape[1]
    rows = MLA_ROWS
    qi = lax.broadcasted_iota(jnp.int32, (rows, rows), 0)
    kj = lax.broadcasted_iota(jnp.int32, (rows, rows), 1)
    causal = kj <= qi
    nt = (((1,), (1,)), ((), ()))

    subs = list(range(tq // rows))
    segments = {}
    for sub in subs:
        first_row = tile * tq + sub * rows
        segments[sub] = [(lo, min(MLA_KV_SEG, first_row - lo), False) for lo in range(0, first_row, MLA_KV_SEG)]
        segments[sub].append((first_row, rows, True))
    chains = [(sub, hh) for sub in subs for hh in range(2)]
    state = {ch: None for ch in chains}

    for rnd in range(max(len(s) for s in segments.values())):
        live = [ch for ch in chains if rnd < len(segments[ch[0]])]
        v_aug = {}
        for sub in sorted({ch[0] for ch in live}):
            lo, width, _ = segments[sub][rnd]
            v = v_ref[0, lo:lo + width, :]
            low_v = _lane_iota(v.shape) < B_V
            one = jnp.ones_like(v)
            v_aug[sub] = (jnp.where(low_v, v, one), jnp.where(low_v, one, v))
        sc, m_new = {}, {}
        for sub, hh in live:
            lo, width, _ = segments[sub][rnd]
            q = q_ref[0, sub * rows:(sub + 1) * rows, hh * LANES:(hh + 1) * LANES]
            k = k_ref[0, lo:lo + width, hh * LANES:(hh + 1) * LANES]
            sc[sub, hh] = lax.dot_general(q, k, nt, preferred_element_type=F32)
        for ch in live:
            if segments[ch[0]][rnd][2]:
                sc[ch] = jnp.where(causal, sc[ch], NEG_INF)
            m_blk = jnp.max(sc[ch], axis=-1, keepdims=True)
            m_new[ch] = m_blk if state[ch] is None else jnp.maximum(state[ch][0], m_blk)
        pv = {ch: jnp.dot(jnp.exp2(sc[ch] - m_new[ch]).astype(BF16), v_aug[ch[0]][ch[1]],
                          preferred_element_type=F32) for ch in live}
        for ch in live:
            if state[ch] is None:
                acc = pv[ch]
            else:
                m, acc = state[ch]
                acc = jnp.exp2(m - m_new[ch]) * acc + pv[ch]
            state[ch] = (m_new[ch], acc)

    low = _lane_iota((rows, LANES)) < B_V
    for sub in subs:
        acc0, acc1 = state[sub, 0][1], state[sub, 1][1]
        num = jnp.where(low, acc0, acc1)
        den = pltpu.roll(jnp.where(low, acc1, acc0), B_V, 1)
        o_ref[0, sub * rows:(sub + 1) * rows, :] = num / den


def _mla_kernel(q_ref, k_ref, v_ref, o_ref):
    i = pl.program_id(2)
    for tile in range(k_ref.shape[1] // q_ref.shape[1]):
        pl.when(i == tile)(functools.partial(_mla_tile, q_ref, k_ref, v_ref, o_ref, tile))


def _mla(qb, kb, vb, tq):
    b, s, _ = qb.shape
    pairs = B_HEADS // 2
    return pl.pallas_call(
        _mla_kernel,
        grid=(b, pairs, s // tq),
        in_specs=[
            pl.BlockSpec((1, tq, 2 * B_QK_PAD), lambda i, p, j: (i, j, p)),
            pl.BlockSpec((1, s, 2 * B_QK_PAD), lambda i, p, j: (i, 0, p)),
            pl.BlockSpec((1, s, LANES), lambda i, p, j: (i, 0, p)),
        ],
        out_specs=pl.BlockSpec((1, tq, LANES), lambda i, p, j: (i, j, p)),
        out_shape=jax.ShapeDtypeStruct((b, s, B_WIDTH), F32),
        compiler_params=pltpu.CompilerParams(
            dimension_semantics=("arbitrary", "arbitrary", "arbitrary"), vmem_limit_bytes=VMEM_LIMIT),
        name="mla",
    )(qb, kb, vb)


def _out_kernel(x_ref, ya_ref, yb_ref, yc_ref, mod_ref, gwa_ref, gwb_ref, wout_ref, n2w_ref,
                wgu_ref, wdn_ref, fw_ref, o_ref, *, final):
    tm = x_ref.shape[1]
    mod = mod_ref[0]
    ffn = wdn_ref.shape[0]
    n2w = n2w_ref[...] * (1.0 + mod[4:5])

    def mix(r0):
        rs = slice(r0, r0 + OUT_ROWS)
        y = jnp.concatenate([
            _rms(ya_ref[0, rs, :], gwa_ref[...]).astype(BF16),
            _rms(yb_ref[0, rs, :], gwb_ref[...]).astype(BF16),
            yc_ref[0, rs, :],
        ], axis=-1)
        x1 = x_ref[0, rs, :] + mod[2:3] * jnp.dot(y, wout_ref[...], preferred_element_type=F32)
        h = (_rms(x1, n2w) + mod[3:4]).astype(BF16)
        return x1, h

    starts = list(range(0, tm, OUT_ROWS))
    mixed = {starts[0]: mix(starts[0])}
    for idx, r0 in enumerate(starts):
        if idx + 1 < len(starts):
            mixed[starts[idx + 1]] = mix(starts[idx + 1])
        x1, h = mixed.pop(r0)
        acc = None
        for lo in range(0, ffn, FFN_CHUNK):
            g = jnp.dot(h, wgu_ref[:, lo:lo + FFN_CHUNK], preferred_element_type=F32)
            u = jnp.dot(h, wgu_ref[:, ffn + lo:ffn + lo + FFN_CHUNK], preferred_element_type=F32)
            a = (g * jax.nn.sigmoid(g) * u).astype(BF16)
            part = jnp.dot(a, wdn_ref[lo:lo + FFN_CHUNK, :], preferred_element_type=F32)
            acc = part if acc is None else acc + part
        x2 = x1 + mod[5:6] * acc
        if final:
            x2 = _rms(x2, fw_ref[...])
        o_ref[0, r0:r0 + OUT_ROWS, :] = x2


def _out_ffn(x, ya, yb, yc, mod, gwa, gwb, wout, n2w, wgu, wdn, fw, tm, layer, final):
    b, s, d = x.shape
    tok = lambda w: pl.BlockSpec((1, tm, w), lambda i, j: (i, j, 0))
    full = lambda a: pl.BlockSpec(a.shape, lambda i, j: (0,) * a.ndim, pipeline_mode=pl.Buffered(1))
    stacked = lambda a: pl.BlockSpec((None,) + a.shape[1:], lambda i, j: (layer,) + (0,) * (a.ndim - 1),
                                     pipeline_mode=pl.Buffered(1))
    return pl.pallas_call(
        functools.partial(_out_kernel, final=final),
        grid=(b, s // tm),
        in_specs=[
            tok(d), tok(A_WIDTH), tok(B_WIDTH), tok(C_WIDTH),
            pl.BlockSpec((1, N_MOD, d), lambda i, j: (i, 0, 0)),
            full(gwa), full(gwb), stacked(wout), full(n2w), stacked(wgu), stacked(wdn), full(fw),
        ],
        out_specs=tok(d),
        out_shape=jax.ShapeDtypeStruct((b, s, d), F32),
        compiler_params=pltpu.CompilerParams(
            dimension_semantics=("arbitrary", "arbitrary"), vmem_limit_bytes=VMEM_LIMIT),
        name="out_ffn",
    )(x, ya, yb, yc, mod, gwa, gwb, wout, n2w, wgu, wdn, fw)


def _in_col_index():
    src_aq, src_ak, src_av = 0, A_WIDTH, A_WIDTH + A_KV_WIDTH
    src_bcq = src_av + A_KV_WIDTH
    src_bckv = src_bcq + B_Q_RANK
    src_bkr = src_bckv + B_KV_RANK
    src_cu = src_bkr + B_ROPE
    src_cv = src_cu + C_WIDTH
    zero = src_cv + C_WIDTH
    idx = []
    for hh in A_HEAD_ORDER:
        idx += list(range(src_aq + hh * HEAD_DIM, src_aq + (hh + 1) * HEAD_DIM))
    idx += list(range(src_ak, src_bkr))
    idx += list(range(src_cu, zero))
    idx += [zero] * B_NOPE + list(range(src_bkr, src_bkr + B_ROPE)) + [zero] * (LANES - B_NOPE - B_ROPE)
    assert len(idx) == IN_COLS_PAD
    return np.asarray(idx, np.int32)


def _take_static(a, idx, axis):
    n = a.shape[axis]
    idx = [int(t) for t in idx]
    parts, start = [], 0
    for pos in range(1, len(idx) + 1):
        if pos < len(idx):
            prev, cur = idx[pos - 1], idx[pos]
            same_run = (prev == n and cur == n) or (prev != n and cur != n and cur == prev + 1)
        else:
            same_run = False
        if not same_run:
            first, count = idx[start], pos - start
            if first == n:
                shape = a.shape[:axis] + (count,) + a.shape[axis + 1:]
                parts.append(jnp.zeros(shape, a.dtype))
            else:
                parts.append(lax.slice_in_dim(a, first, first + count, axis=axis))
            start = pos
    return jnp.concatenate(parts, axis=axis)


def _rope_tables(positions):
    b, s = positions.shape
    pos = positions.astype(F32)[..., None]

    def tables(dim):
        half = dim // 2
        per_row = LANES // half
        inv = 1.0 / (ROPE_THETA ** (jnp.arange(0, dim, 2, dtype=F32) / dim))
        ang = lax.optimization_barrier(
            (pos.reshape(b, s // per_row, per_row, 1) * inv).reshape(b, s // per_row, LANES))
        cos, sin = lax.optimization_barrier((jnp.cos(ang), jnp.sin(ang)))
        return cos.reshape(b, s, half), sin.reshape(b, s, half)

    cos_a, sin_a = tables(HEAD_DIM)
    ca = jnp.concatenate([cos_a] * 4, axis=-1)
    sa = jnp.concatenate([-sin_a, sin_a] * 2, axis=-1)
    cos_b, sin_b = tables(B_ROPE)
    ones = jnp.ones(pos.shape[:-1] + (B_NOPE,), F32)
    zeros = jnp.zeros(pos.shape[:-1] + (LANES - B_NOPE - B_ROPE,), F32)
    cb = jnp.concatenate([ones, cos_b, cos_b, zeros], axis=-1)
    sb = jnp.concatenate([0.0 * ones, -sin_b, sin_b, zeros], axis=-1)
    return ca, sa, cb, sb


def kernel(x, c, positions, ada_w, ada_b, norm1_w, w_in, a_sinks, b_q_norm_w, b_w_uq, b_kv_norm_w, b_w_ukv,
           c_ln_w, c_ln_b, c_w_s, c_b_s, out_norm_w, w_out, norm2_w, w_gate_up, w_down, final_norm_w):
    depth = w_in.shape[0]
    b, s, d = x.shape
    ffn = w_down.shape[1]
    tm_in, tm_out, tq = min(IN_TILE, s), min(OUT_TILE, s), min(ATTN_TILE, s)

    mod = _modulation(c, ada_w, ada_b).reshape(depth, b, N_MOD, d)
    tabs = _rope_tables(positions)

    col_scale = np.ones((w_in.shape[2],), np.float32)
    col_scale[:A_WIDTH] = HEAD_DIM ** -0.5
    w_in_p = _take_static((w_in * col_scale).astype(BF16), _in_col_index(), 2)

    w_uq = b_w_uq.reshape(depth, B_Q_RANK, B_HEADS, B_NOPE + B_ROPE)
    w_uq = jnp.pad(w_uq, ((0, 0), (0, 0), (0, 0), (0, B_QK_PAD - B_NOPE - B_ROPE)))
    w_uq = w_uq.reshape(depth, B_Q_RANK, B_HEADS * B_QK_PAD).astype(BF16)
    w_ukv = b_w_ukv.reshape(depth, B_KV_RANK, B_HEADS, B_NOPE + B_V)
    w_uk = jnp.pad(w_ukv[..., :B_NOPE], ((0, 0), (0, 0), (0, 0), (0, B_QK_PAD - B_NOPE)))
    w_ukv = jnp.concatenate([w_uk.reshape(depth, B_KV_RANK, B_HEADS * B_QK_PAD),
                             w_ukv[..., B_NOPE:].reshape(depth, B_KV_RANK, B_WIDTH)], axis=-1).astype(BF16)

    bs_full = jnp.repeat(jnp.swapaxes(c_b_s, 1, 2), C_GROUP_DIM, axis=-1)
    order = np.asarray(A_HEAD_ORDER)
    sink_tab = jnp.broadcast_to((a_sinks * LOG2E)[:, order, None], (depth, A_Q_HEADS, LANES))
    sink_tab = jnp.pad(sink_tab, ((0, 0), (0, 8 - A_Q_HEADS), (0, 0)))

    a_perm = (order[:, None] * HEAD_DIM + np.arange(HEAD_DIM)[None, :]).reshape(-1)
    out_perm = np.concatenate([a_perm, np.arange(A_WIDTH, d)])
    gw = _take_static(out_norm_w, out_perm, 1)
    w_out_p = _take_static(w_out.astype(BF16), out_perm, 1)

    assert ffn % FFN_CHUNK == 0
    w_gu = w_gate_up.astype(BF16)
    w_dn = w_down.astype(BF16)

    row = lambda a: a.reshape(1, -1)
    for l in range(depth):
        ya, qb, kb, vb, yc = _in_proj(
            x, mod[l], row(norm1_w[l]), w_in_p, w_uq, w_ukv, row(b_q_norm_w[l]), row(b_kv_norm_w[l]),
            tabs, row(c_ln_w[l]), row(c_ln_b[l]), c_w_s, bs_full, row(gw[l, A_WIDTH + B_WIDTH:]), sink_tab,
            tm_in, l)
        yb = _mla(qb, kb, vb, tq)
        x = _out_ffn(x, ya, yb, yc, mod[l], row(gw[l, :A_WIDTH]), row(gw[l, A_WIDTH:A_WIDTH + B_WIDTH]),
                     w_out_p, row(norm2_w[l]), w_gu, w_dn, row(final_norm_w), tm_out, l, l == depth - 1)
    return x
```

```python
import functools

import numpy as np
import jax
import jax.numpy as jnp
from jax import lax
from jax.experimental import pallas as pl
from jax.experimental.pallas import tpu as pltpu

F32 = jnp.float32
BF16 = jnp.bfloat16

LANES = 128
HEAD_DIM = 64
ROPE_THETA = 10000.0
NORM_EPS = 1e-6
NEG_INF = -1e30
LOG2E = float(np.log2(np.e))
BLOCK = 128

A_Q_HEADS = 6
A_KV_HEADS = 2
A_WIDTH = A_Q_HEADS * HEAD_DIM
A_KV_WIDTH = A_KV_HEADS * HEAD_DIM
A_HEAD_ORDER = (0, 3, 1, 4, 2, 5)

B_HEADS = 6
B_Q_RANK = 384
B_KV_RANK = 256
B_NOPE = 64
B_ROPE = 32
B_V = 64
B_WIDTH = B_HEADS * B_V
B_QK_PAD = LANES

C_GROUPS = 4
C_GROUP_DIM = 64
C_WIDTH = C_GROUPS * C_GROUP_DIM

N_MOD = 6

OFF_AQ = 0
OFF_AK = OFF_AQ + A_WIDTH
OFF_AV = OFF_AK + A_KV_WIDTH
OFF_BCQ = OFF_AV + A_KV_WIDTH
OFF_BCKV = OFF_BCQ + B_Q_RANK
OFF_CU = OFF_BCKV + B_KV_RANK
OFF_CV = OFF_CU + C_WIDTH
OFF_BKR = OFF_CV + C_WIDTH
IN_COLS_PAD = OFF_BKR + LANES

VMEM_LIMIT = 56 * 1024 * 1024
MLA_KV_SEG = 2048
IN_ROWS = 256
IN_TILE = 1024
IN_COL_CHUNK = 512
OUT_TILE = 1024
OUT_ROWS = 512
ATTN_TILE = 2048
MLA_ROWS = 256
FFN_CHUNK = 256
MOD_COLS = 1536


def _rms(x, w):
    return x * lax.rsqrt(jnp.mean(x * x, axis=-1, keepdims=True) + NORM_EPS) * w


def _gelu(x):
    return 0.5 * x * (1.0 + lax.erf(x * float(np.sqrt(0.5))))


def _lane_iota(shape):
    return lax.broadcasted_iota(jnp.int32, shape, len(shape) - 1)


def _mod_kernel(c_ref, w_ref, b_ref, o_ref):
    c = c_ref[...]
    act = (c * jax.nn.sigmoid(c)).astype(BF16)
    o_ref[0] = jnp.dot(act, w_ref[0].astype(BF16), preferred_element_type=F32) + b_ref[0]


def _modulation(c, ada_w, ada_b):
    depth, d, n = ada_w.shape
    b = c.shape[0]
    tn = MOD_COLS
    return pl.pallas_call(
        _mod_kernel,
        grid=(depth, n // tn),
        in_specs=[
            pl.BlockSpec((b, d), lambda l, j: (0, 0)),
            pl.BlockSpec((1, d, tn), lambda l, j: (l, 0, j)),
            pl.BlockSpec((1, 1, tn), lambda l, j: (l, 0, j)),
        ],
        out_specs=pl.BlockSpec((1, b, tn), lambda l, j: (l, 0, j)),
        out_shape=jax.ShapeDtypeStruct((depth, b, n), F32),
        compiler_params=pltpu.CompilerParams(
            dimension_semantics=("arbitrary", "arbitrary"), vmem_limit_bytes=VMEM_LIMIT),
        name="modulation",
    )(c, ada_w, ada_b.reshape(depth, 1, n))


def _swa_scores(q_pairs, kcat):
    low = _lane_iota((BLOCK, LANES)) < HEAD_DIM
    out = []
    for half in range(2):
        keep = low if half == 0 else jnp.logical_not(low)
        qs = jnp.concatenate([jnp.where(keep, qp, jnp.zeros_like(qp)) for qp in q_pairs], axis=0)
        out.append(lax.dot_general(qs, kcat, (((1,), (1,)), ((), ())), preferred_element_type=F32))
    return out


def _in_kernel(x_ref, mod_ref, n1w_ref, win_ref, wuq_ref, wukv_ref, qnw_ref, kvnw_ref,
               ca_ref, sa_ref, cb_ref, sb_ref, lnw_ref, lnb_ref, ws_ref, bs_ref, gwc_ref, sink_ref,
               ya_ref, qb_ref, kb_ref, vb_ref, yc_ref, kprev_ref, vprev_ref, band_ref):
    tm = x_ref.shape[1]
    mod = mod_ref[0]
    n1w = n1w_ref[...] * (1.0 + mod[1:2])
    lane = _lane_iota((IN_ROWS, LANES))
    a_first = (lane % HEAD_DIM) < (HEAD_DIM // 2)
    b_first = lane < (B_NOPE + B_ROPE // 2)

    first_tile = pl.program_id(1) == 0

    @pl.when(first_tile)
    def _():
        kprev_ref[...] = jnp.zeros_like(kprev_ref)
        vprev_ref[...] = jnp.zeros_like(vprev_ref)

    pairs = A_WIDTH // LANES
    qi = lax.broadcasted_iota(jnp.int32, (pairs * BLOCK, 2 * BLOCK), 0) % BLOCK
    kj = lax.broadcasted_iota(jnp.int32, (pairs * BLOCK, 2 * BLOCK), 1)
    rel = qi + BLOCK - kj
    band = (rel >= 0) & (rel < BLOCK)
    band_ref[...] = jnp.where(band, 0.0, NEG_INF)
    sinks = [jnp.concatenate([jnp.broadcast_to(sink_ref[2 * p + half:2 * p + half + 1, :], (BLOCK, LANES))
                              for p in range(pairs)], axis=0) for half in range(2)]
    ones_blk = jnp.ones((2 * BLOCK, LANES), BF16)
    low_a = _lane_iota((BLOCK, LANES)) < HEAD_DIM
    k_last = kprev_ref[...]
    v_last = vprev_ref[...]
    b_scale = float((B_NOPE + B_ROPE) ** -0.5) * LOG2E
    row = lax.broadcasted_iota(jnp.int32, (BLOCK, BLOCK), 0)
    col = lax.broadcasted_iota(jnp.int32, (BLOCK, BLOCK), 1)
    w_s = [jnp.where(col <= row, ws_ref[g], 0.0).astype(BF16) for g in range(C_GROUPS)]
    low = _lane_iota((BLOCK, LANES)) < C_GROUP_DIM
    bias = bs_ref[...]
    gwc = gwc_ref[...]

    def normed(r0):
        x = x_ref[0, r0:r0 + IN_ROWS, :]
        return (_rms(x, n1w) + mod[0:1]).astype(BF16)

    n_col_chunks = -(-IN_COLS_PAD // IN_COL_CHUNK)

    def project_chunk(h, c):
        lo = c * IN_COL_CHUNK
        return jnp.dot(h, win_ref[:, lo:min(lo + IN_COL_CHUNK, IN_COLS_PAD)], preferred_element_type=F32)

    starts = list(range(0, tm, IN_ROWS))
    h_first = normed(starts[0])
    cur = [project_chunk(h_first, c) for c in range(n_col_chunks)]
    for idx, r0 in enumerate(starts):
        has_next = idx + 1 < len(starts)
        h_next = normed(starts[idx + 1]) if has_next else None
        nxt = []

        def ahead():
            if has_next and len(nxt) < n_col_chunks:
                nxt.append(project_chunk(h_next, len(nxt)))

        def cols(lo, width, chunks=cur):
            c = lo // IN_COL_CHUNK
            assert (lo + width - 1) // IN_COL_CHUNK == c
            return chunks[c][:, lo - c * IN_COL_CHUNK:lo + width - c * IN_COL_CHUNK]

        rs = slice(r0, r0 + IN_ROWS)
        ahead()

        ca = ca_ref[0, rs, :]
        sa = sa_ref[0, rs, :]

        def rope_a(t):
            sw = jnp.where(a_first, pltpu.roll(t, LANES - HEAD_DIM // 2, 1), pltpu.roll(t, HEAD_DIM // 2, 1))
            return t * ca + sw * sa

        q_pairs = [(rope_a(cols(OFF_AQ + p * LANES, LANES)) * LOG2E).astype(BF16) for p in range(pairs)]
        ka = rope_a(cols(OFF_AK, A_KV_WIDTH)).astype(BF16)
        va = cols(OFF_AV, A_KV_WIDTH).astype(BF16)

        n_blk = IN_ROWS // BLOCK
        kcats, vcats, scs = [], [], []
        for nb in range(n_blk):
            if nb == 0:
                kcats.append(jnp.concatenate([k_last, ka[0:BLOCK]], axis=0))
                vcats.append(jnp.concatenate([v_last, va[0:BLOCK]], axis=0))
            else:
                kcats.append(ka[(nb - 1) * BLOCK:(nb + 1) * BLOCK])
                vcats.append(va[(nb - 1) * BLOCK:(nb + 1) * BLOCK])
            scs.append(_swa_scores([qp[nb * BLOCK:(nb + 1) * BLOCK] for qp in q_pairs], kcats[nb]))
        k_last = ka[IN_ROWS - BLOCK:]
        v_last = va[IN_ROWS - BLOCK:]

        cb = cb_ref[0, rs, :]
        sb = sb_ref[0, rs, :]

        def rope_b(t):
            sw = jnp.where(b_first, pltpu.roll(t, LANES - B_ROPE // 2, 1), pltpu.roll(t, B_ROPE // 2, 1))
            return t * cb + sw * sb

        ahead()
        cq = _rms(cols(OFF_BCQ, B_Q_RANK), qnw_ref[...]).astype(BF16)
        qb = jnp.dot(cq, wuq_ref[...], preferred_element_type=F32)
        ckv = _rms(cols(OFF_BCKV, B_KV_RANK), kvnw_ref[...]).astype(BF16)
        kv = jnp.dot(ckv, wukv_ref[...], preferred_element_type=F32)

        ms = []
        for nb in range(n_blk):
            for half in range(2):
                if idx == 0 and nb == 0:
                    seen = band & (kj >= jnp.where(first_tile, BLOCK, 0))
                    scs[nb][half] = jnp.where(seen, scs[nb][half], NEG_INF)
                else:
                    scs[nb][half] = scs[nb][half] + band_ref[...]
            ms.append([jnp.maximum(jnp.max(scs[nb][half], axis=-1, keepdims=True), sinks[half])
                       for half in range(2)])

        ahead()
        kr = rope_b(cols(OFF_BKR, LANES))
        for hh in range(B_HEADS):
            sl = slice(hh * LANES, (hh + 1) * LANES)
            qb_ref[0, rs, sl] = (rope_b(qb[:, sl]) * b_scale).astype(BF16)
            kb_ref[0, rs, sl] = (kv[:, sl] + kr).astype(BF16)
        vb_ref[0, rs, :] = kv[:, B_HEADS * LANES:].astype(BF16)

        pvs = []
        for nb in range(n_blk):
            pvs.append([])
            for half in range(2):
                e = jnp.concatenate([jnp.exp2(scs[nb][half][:, t * LANES:(t + 1) * LANES] - ms[nb][half])
                                     for t in range(2 * BLOCK // LANES)], axis=1).astype(BF16)
                pvs[nb].append(jnp.dot(e, jnp.concatenate([vcats[nb], ones_blk], axis=1),
                                       preferred_element_type=F32))

        ahead()
        u = _gelu(cols(OFF_CU, C_WIDTH))
        v = _gelu(cols(OFF_CV, C_WIDTH))
        mu = jnp.mean(v, axis=-1, keepdims=True)
        vc = v - mu
        var = jnp.mean(vc * vc, axis=-1, keepdims=True)
        v = (vc * lax.rsqrt(var + NORM_EPS) * lnw_ref[...] + lnb_ref[...]).astype(BF16)
        for c in range(IN_ROWS // BLOCK):
            rows = slice(c * BLOCK, (c + 1) * BLOCK)
            parts = []
            for p in range(C_WIDTH // LANES):
                vp = v[rows, p * LANES:(p + 1) * LANES]
                r_lo = jnp.dot(w_s[2 * p], vp, preferred_element_type=F32)
                r_hi = jnp.dot(w_s[2 * p + 1], vp, preferred_element_type=F32)
                parts.append(jnp.where(low, r_lo, r_hi))
            mixed = jnp.concatenate(parts, axis=-1) + bias
            yc = u[rows] * mixed
            yc_ref[0, r0 + c * BLOCK:r0 + (c + 1) * BLOCK, :] = _rms(yc, gwc).astype(BF16)

        for nb in range(n_blk):
            outs = [pvs[nb][half][:, :LANES] / (pvs[nb][half][:, LANES:] + jnp.exp2(sinks[half] - ms[nb][half]))
                    for half in range(2)]
            for p in range(pairs):
                blk = slice(p * BLOCK, (p + 1) * BLOCK)
                ya_ref[0, r0 + nb * BLOCK:r0 + (nb + 1) * BLOCK, p * LANES:(p + 1) * LANES] = jnp.where(
                    low_a, outs[0][blk], outs[1][blk])

        while has_next and len(nxt) < n_col_chunks:
            ahead()
        cur = nxt

    kprev_ref[...] = k_last
    vprev_ref[...] = v_last


def _in_proj(x, mod, n1w, win, wuq, wukv, qnw, kvnw, tabs, lnw, lnb, ws, bs_full, gwc, sink_tab, tm, layer):
    b, s, d = x.shape
    tok = lambda w: pl.BlockSpec((1, tm, w), lambda i, j: (i, j, 0))
    full = lambda a: pl.BlockSpec(a.shape, lambda i, j: (0,) * a.ndim)
    stacked = lambda a: pl.BlockSpec((None,) + a.shape[1:], lambda i, j: (layer,) + (0,) * (a.ndim - 1))
    ca, sa, cb, sb = tabs
    out_widths = (A_WIDTH, B_HEADS * B_QK_PAD, B_HEADS * B_QK_PAD, B_WIDTH, C_WIDTH)
    out_specs = [tok(w) for w in out_widths]
    out_shape = [jax.ShapeDtypeStruct((b, s, w), F32 if n == 0 else BF16) for n, w in enumerate(out_widths)]
    return pl.pallas_call(
        _in_kernel,
        grid=(b, s // tm),
        in_specs=[
            tok(d),
            pl.BlockSpec((1, N_MOD, d), lambda i, j: (i, 0, 0)),
            full(n1w), stacked(win), stacked(wuq), stacked(wukv), full(qnw), full(kvnw),
            tok(LANES), tok(LANES), tok(LANES), tok(LANES),
            full(lnw), full(lnb), stacked(ws), stacked(bs_full), full(gwc), stacked(sink_tab),
        ],
        out_specs=out_specs,
        out_shape=out_shape,
        scratch_shapes=[pltpu.VMEM((BLOCK, A_KV_WIDTH), BF16), pltpu.VMEM((BLOCK, A_KV_WIDTH), BF16),
                        pltpu.VMEM((A_WIDTH // LANES * BLOCK, 2 * BLOCK), F32)],
        compiler_params=pltpu.CompilerParams(
            dimension_semantics=("arbitrary", "arbitrary"), vmem_limit_bytes=VMEM_LIMIT),
        name="in_proj",
    )(x, mod, n1w, win, wuq, wukv, qnw, kvnw, ca, sa, cb, sb, lnw, lnb, ws, bs_full, gwc, sink_tab)


def _mla_tile(q_ref, k_ref, v_ref, o_ref, tile):
    tq = q_ref.shape[1]
    rows = MLA_ROWS
    qi = lax.broadcasted_iota(jnp.int32, (rows, rows), 0)
    kj = lax.broadcasted_iota(jnp.int32, (rows, rows), 1)
    causal = kj <= qi
    nt = (((1,), (1,)), ((), ()))

    subs = list(range(tq // rows))
    segments = {}
    for sub in subs:
        first_row = tile * tq + sub * rows
        segments[sub] = [(lo, min(MLA_KV_SEG, first_row - lo), False) for lo in range(0, first_row, MLA_KV_SEG)]
        segments[sub].append((first_row, rows, True))
    chains = [(sub, hh) for sub in subs for hh in range(2)]
    state = {ch: None for ch in chains}

    for rnd in range(max(len(s) for s in segments.values())):
        live = [ch for ch in chains if rnd < len(segments[ch[0]])]
        v_aug = {}
        for sub in sorted({ch[0] for ch in live}):
            lo, width, _ = segments[sub][rnd]
            v = v_ref[0, lo:lo + width, :]
            low_v = _lane_iota(v.shape) < B_V
            one = jnp.ones_like(v)
            v_aug[sub] = (jnp.where(low_v, v, one), jnp.where(low_v, one, v))
        sc, m_new = {}, {}
        for sub, hh in live:
            lo, width, _ = segments[sub][rnd]
            q = q_ref[0, sub * rows:(sub + 1) * rows, hh * LANES:(hh + 1) * LANES]
            k = k_ref[0, lo:lo + width, hh * LANES:(hh + 1) * LANES]
            sc[sub, hh] = lax.dot_general(q, k, nt, preferred_element_type=F32)
        for ch in live:
            if segments[ch[0]][rnd][2]:
                sc[ch] = jnp.where(causal, sc[ch], NEG_INF)
            m_blk = jnp.max(sc[ch], axis=-1, keepdims=True)
            m_new[ch] = m_blk if state[ch] is None else jnp.maximum(state[ch][0], m_blk)
        pv = {ch: jnp.dot(jnp.exp2(sc[ch] - m_new[ch]).astype(BF16), v_aug[ch[0]][ch[1]],
                          preferred_element_type=F32) for ch in live}
        for ch in live:
            if state[ch] is None:
                acc = pv[ch]
            else:
                m, acc = state[ch]
                acc = jnp.exp2(m - m_new[ch]) * acc + pv[ch]
            state[ch] = (m_new[ch], acc)

    low = _lane_iota((rows, LANES)) < B_V
    for sub in subs:
        acc0, acc1 = state[sub, 0][1], state[sub, 1][1]
        num = jnp.where(low, acc0, acc1)
        den = pltpu.roll(jnp.where(low, acc1, acc0), B_V, 1)
        o_ref[0, sub * rows:(sub + 1) * rows, :] = num / den


def _mla_kernel(q_ref, k_ref, v_ref, o_ref):
    i = pl.program_id(2)
    for tile in range(k_ref.shape[1] // q_ref.shape[1]):
        pl.when(i == tile)(functools.partial(_mla_tile, q_ref, k_ref, v_ref, o_ref, tile))


def _mla(qb, kb, vb, tq):
    b, s, _ = qb.shape
    pairs = B_HEADS // 2
    return pl.pallas_call(
        _mla_kernel,
        grid=(b, pairs, s // tq),
        in_specs=[
            pl.BlockSpec((1, tq, 2 * B_QK_PAD), lambda i, p, j: (i, j, p)),
            pl.BlockSpec((1, s, 2 * B_QK_PAD), lambda i, p, j: (i, 0, p)),
            pl.BlockSpec((1, s, LANES), lambda i, p, j: (i, 0, p)),
        ],
        out_specs=pl.BlockSpec((1, tq, LANES), lambda i, p, j: (i, j, p)),
        out_shape=jax.ShapeDtypeStruct((b, s, B_WIDTH), F32),
        compiler_params=pltpu.CompilerParams(
            dimension_semantics=("arbitrary", "arbitrary", "arbitrary"), vmem_limit_bytes=VMEM_LIMIT),
        name="mla",
    )(qb, kb, vb)


def _out_kernel(x_ref, ya_ref, yb_ref, yc_ref, mod_ref, gwa_ref, gwb_ref, wout_ref, n2w_ref,
                wgu_ref, wdn_ref, fw_ref, o_ref, *, final):
    tm = x_ref.shape[1]
    mod = mod_ref[0]
    ffn = wdn_ref.shape[0]
    n2w = n2w_ref[...] * (1.0 + mod[4:5])

    def mix(r0):
        rs = slice(r0, r0 + OUT_ROWS)
        y = jnp.concatenate([
            _rms(ya_ref[0, rs, :], gwa_ref[...]).astype(BF16),
            _rms(yb_ref[0, rs, :], gwb_ref[...]).astype(BF16),
            yc_ref[0, rs, :],
        ], axis=-1)
        x1 = x_ref[0, rs, :] + mod[2:3] * jnp.dot(y, wout_ref[...], preferred_element_type=F32)
        h = (_rms(x1, n2w) + mod[3:4]).astype(BF16)
        return x1, h

    starts = list(range(0, tm, OUT_ROWS))
    mixed = {starts[0]: mix(starts[0])}
    for idx, r0 in enumerate(starts):
        if idx + 1 < len(starts):
            mixed[starts[idx + 1]] = mix(starts[idx + 1])
        x1, h = mixed.pop(r0)
        acc = None
        for lo in range(0, ffn, FFN_CHUNK):
            g = jnp.dot(h, wgu_ref[:, lo:lo + FFN_CHUNK], preferred_element_type=F32)
            u = jnp.dot(h, wgu_ref[:, ffn + lo:ffn + lo + FFN_CHUNK], preferred_element_type=F32)
            a = (g * jax.nn.sigmoid(g) * u).astype(BF16)
            part = jnp.dot(a, wdn_ref[lo:lo + FFN_CHUNK, :], preferred_element_type=F32)
            acc = part if acc is None else acc + part
        x2 = x1 + mod[5:6] * acc
        if final:
            x2 = _rms(x2, fw_ref[...])
        o_ref[0, r0:r0 + OUT_ROWS, :] = x2


def _out_ffn(x, ya, yb, yc, mod, gwa, gwb, wout, n2w, wgu, wdn, fw, tm, layer, final):
    b, s, d = x.shape
    tok = lambda w: pl.BlockSpec((1, tm, w), lambda i, j: (i, j, 0))
    full = lambda a: pl.BlockSpec(a.shape, lambda i, j: (0,) * a.ndim, pipeline_mode=pl.Buffered(1))
    stacked = lambda a: pl.BlockSpec((None,) + a.shape[1:], lambda i, j: (layer,) + (0,) * (a.ndim - 1),
                                     pipeline_mode=pl.Buffered(1))
    return pl.pallas_call(
        functools.partial(_out_kernel, final=final),
        grid=(b, s // tm),
        in_specs=[
            tok(d), tok(A_WIDTH), tok(B_WIDTH), tok(C_WIDTH),
            pl.BlockSpec((1, N_MOD, d), lambda i, j: (i, 0, 0)),
            full(gwa), full(gwb), stacked(wout), full(n2w), stacked(wgu), stacked(wdn), full(fw),
        ],
        out_specs=tok(d),
        out_shape=jax.ShapeDtypeStruct((b, s, d), F32),
        compiler_params=pltpu.CompilerParams(
            dimension_semantics=("arbitrary", "arbitrary"), vmem_limit_bytes=VMEM_LIMIT),
        name="out_ffn",
    )(x, ya, yb, yc, mod, gwa, gwb, wout, n2w, wgu, wdn, fw)


def _in_col_index():
    src_aq, src_ak, src_av = 0, A_WIDTH, A_WIDTH + A_KV_WIDTH
    src_bcq = src_av + A_KV_WIDTH
    src_bckv = src_bcq + B_Q_RANK
    src_bkr = src_bckv + B_KV_RANK
    src_cu = src_bkr + B_ROPE
    src_cv = src_cu + C_WIDTH
    zero = src_cv + C_WIDTH
    idx = []
    for hh in A_HEAD_ORDER:
        idx += list(range(src_aq + hh * HEAD_DIM, src_aq + (hh + 1) * HEAD_DIM))
    idx += list(range(src_ak, src_bkr))
    idx += list(range(src_cu, zero))
    idx += [zero] * B_NOPE + list(range(src_bkr, src_bkr + B_ROPE)) + [zero] * (LANES - B_NOPE - B_ROPE)
    assert len(idx) == IN_COLS_PAD
    return np.asarray(idx, np.int32)


def _take_static(a, idx, axis):
    n = a.shape[axis]
    idx = [int(t) for t in idx]
    parts, start = [], 0
    for pos in range(1, len(idx) + 1):
        if pos < len(idx):
            prev, cur = idx[pos - 1], idx[pos]
            same_run = (prev == n and cur == n) or (prev != n and cur != n and cur == prev + 1)
        else:
            same_run = False
        if not same_run:
            first, count = idx[start], pos - start
            if first == n:
                shape = a.shape[:axis] + (count,) + a.shape[axis + 1:]
                parts.append(jnp.zeros(shape, a.dtype))
            else:
                parts.append(lax.slice_in_dim(a, first, first + count, axis=axis))
            start = pos
    return jnp.concatenate(parts, axis=axis)


def _rope_tables(positions):
    b, s = positions.shape
    pos = positions.astype(F32)[..., None]

    def tables(dim):
        half = dim // 2
        per_row = LANES // half
        inv = 1.0 / (ROPE_THETA ** (jnp.arange(0, dim, 2, dtype=F32) / dim))
        ang = lax.optimization_barrier(
            (pos.reshape(b, s // per_row, per_row, 1) * inv).reshape(b, s // per_row, LANES))
        cos, sin = lax.optimization_barrier((jnp.cos(ang), jnp.sin(ang)))
        return cos.reshape(b, s, half), sin.reshape(b, s, half)

    half_a, half_b = HEAD_DIM // 2, B_ROPE // 2
    sign_a = np.tile(np.repeat(np.asarray([-1.0, 1.0], np.float32), half_a), LANES // HEAD_DIM)
    sign_b = np.ones((LANES,), np.float32)
    sign_b[B_NOPE:B_NOPE + half_b] = -1.0

    cos_a, sin_a = tables(HEAD_DIM)
    ca = jnp.concatenate([cos_a] * 4, axis=-1)
    sa = jnp.concatenate([sin_a] * 4, axis=-1) * sign_a
    cos_b, sin_b = tables(B_ROPE)
    ones = jnp.ones(pos.shape[:-1] + (B_NOPE,), F32)
    zeros = jnp.zeros(pos.shape[:-1] + (LANES - B_NOPE - B_ROPE,), F32)
    cb = jnp.concatenate([ones, cos_b, cos_b, zeros], axis=-1)
    sb = jnp.concatenate([0.0 * ones, sin_b, sin_b, zeros], axis=-1) * sign_b
    return ca, sa, cb, sb


def kernel(x, c, positions, ada_w, ada_b, norm1_w, w_in, a_sinks, b_q_norm_w, b_w_uq, b_kv_norm_w, b_w_ukv,
           c_ln_w, c_ln_b, c_w_s, c_b_s, out_norm_w, w_out, norm2_w, w_gate_up, w_down, final_norm_w):
    depth = w_in.shape[0]
    b, s, d = x.shape
    ffn = w_down.shape[1]
    tm_in, tm_out, tq = min(IN_TILE, s), min(OUT_TILE, s), min(ATTN_TILE, s)

    mod = _modulation(c, ada_w, ada_b).reshape(depth, b, N_MOD, d)
    tabs = _rope_tables(positions)

    col_scale = np.ones((w_in.shape[2],), np.float32)
    col_scale[:A_WIDTH] = HEAD_DIM ** -0.5
    w_in_p = _take_static((w_in * col_scale).astype(BF16), _in_col_index(), 2)

    w_uq = b_w_uq.reshape(depth, B_Q_RANK, B_HEADS, B_NOPE + B_ROPE)
    w_uq = jnp.pad(w_uq, ((0, 0), (0, 0), (0, 0), (0, B_QK_PAD - B_NOPE - B_ROPE)))
    w_uq = w_uq.reshape(depth, B_Q_RANK, B_HEADS * B_QK_PAD).astype(BF16)
    w_ukv = b_w_ukv.reshape(depth, B_KV_RANK, B_HEADS, B_NOPE + B_V)
    w_uk = jnp.pad(w_ukv[..., :B_NOPE], ((0, 0), (0, 0), (0, 0), (0, B_QK_PAD - B_NOPE)))
    w_ukv = jnp.concatenate([w_uk.reshape(depth, B_KV_RANK, B_HEADS * B_QK_PAD),
                             w_ukv[..., B_NOPE:].reshape(depth, B_KV_RANK, B_WIDTH)], axis=-1).astype(BF16)

    bs_full = jnp.repeat(jnp.swapaxes(c_b_s, 1, 2), C_GROUP_DIM, axis=-1)
    order = np.asarray(A_HEAD_ORDER)
    sink_tab = jnp.broadcast_to((a_sinks * LOG2E)[:, order, None], (depth, A_Q_HEADS, LANES))
    sink_tab = jnp.pad(sink_tab, ((0, 0), (0, 8 - A_Q_HEADS), (0, 0)))

    a_perm = (order[:, None] * HEAD_DIM + np.arange(HEAD_DIM)[None, :]).reshape(-1)
    out_perm = np.concatenate([a_perm, np.arange(A_WIDTH, d)])
    gw = _take_static(out_norm_w, out_perm, 1)
    w_out_p = _take_static(w_out.astype(BF16), out_perm, 1)

    assert ffn % FFN_CHUNK == 0
    w_gu = w_gate_up.astype(BF16)
    w_dn = w_down.astype(BF16)

    row = lambda a: a.reshape(1, -1)
    for l in range(depth):
        ya, qb, kb, vb, yc = _in_proj(
            x, mod[l], row(norm1_w[l]), w_in_p, w_uq, w_ukv, row(b_q_norm_w[l]), row(b_kv_norm_w[l]),
            tabs, row(c_ln_w[l]), row(c_ln_b[l]), c_w_s, bs_full, row(gw[l, A_WIDTH + B_WIDTH:]), sink_tab,
            tm_in, l)
        yb = _mla(qb, kb, vb, tq)
        x = _out_ffn(x, ya, yb, yc, mod[l], row(gw[l, :A_WIDTH]), row(gw[l, A_WIDTH:A_WIDTH + B_WIDTH]),
                     w_out_p, row(norm2_w[l]), w_gu, w_dn, row(final_norm_w), tm_out, l, l == depth - 1)
    return x
```

```python
import functools

import numpy as np
import jax
import jax.numpy as jnp
from jax import lax
from jax.experimental import pallas as pl
from jax.experimental.pallas import tpu as pltpu

F32 = jnp.float32
BF16 = jnp.bfloat16

LANES = 128
HEAD_DIM = 64
ROPE_THETA = 10000.0
NORM_EPS = 1e-6
NEG_INF = -1e30
LOG2E = float(np.log2(np.e))
BLOCK = 128

A_Q_HEADS = 6
A_KV_HEADS = 2
A_WIDTH = A_Q_HEADS * HEAD_DIM
A_KV_WIDTH = A_KV_HEADS * HEAD_DIM
A_HEAD_ORDER = (0, 3, 1, 4, 2, 5)

B_HEADS = 6
B_Q_RANK = 384
B_KV_RANK = 256
B_NOPE = 64
B_ROPE = 32
B_V = 64
B_WIDTH = B_HEADS * B_V
B_QK_PAD = LANES

C_GROUPS = 4
C_GROUP_DIM = 64
C_WIDTH = C_GROUPS * C_GROUP_DIM

N_MOD = 6

OFF_AQ = 0
OFF_AK = OFF_AQ + A_WIDTH
OFF_AV = OFF_AK + A_KV_WIDTH
OFF_BCQ = OFF_AV + A_KV_WIDTH
OFF_BCKV = OFF_BCQ + B_Q_RANK
OFF_CU = OFF_BCKV + B_KV_RANK
OFF_CV = OFF_CU + C_WIDTH
OFF_BKR = OFF_CV + C_WIDTH
IN_COLS_PAD = OFF_BKR + LANES

VMEM_LIMIT = 56 * 1024 * 1024
MLA_KV_SEG = 2048
IN_ROWS = 256
IN_TILE = 1024
IN_COL_CHUNK = 512
OUT_TILE = 1024
OUT_ROWS = 512
ATTN_TILE = 2048
MLA_ROWS = 256
FFN_CHUNK = 256
MOD_COLS = 1536
ROPE_TILE = 2048


def _rms(x, w):
    return x * lax.rsqrt(jnp.mean(x * x, axis=-1, keepdims=True) + NORM_EPS) * w


def _gelu(x):
    return 0.5 * x * (1.0 + lax.erf(x * float(np.sqrt(0.5))))


def _lane_iota(shape):
    return lax.broadcasted_iota(jnp.int32, shape, len(shape) - 1)


def _mod_kernel(c_ref, w_ref, b_ref, o_ref):
    c = c_ref[...]
    act = (c * jax.nn.sigmoid(c)).astype(BF16)
    o_ref[0] = jnp.dot(act, w_ref[0].astype(BF16), preferred_element_type=F32) + b_ref[0]


def _modulation(c, ada_w, ada_b):
    depth, d, n = ada_w.shape
    b = c.shape[0]
    tn = MOD_COLS
    return pl.pallas_call(
        _mod_kernel,
        grid=(depth, n // tn),
        in_specs=[
            pl.BlockSpec((b, d), lambda l, j: (0, 0)),
            pl.BlockSpec((1, d, tn), lambda l, j: (l, 0, j)),
            pl.BlockSpec((1, 1, tn), lambda l, j: (l, 0, j)),
        ],
        out_specs=pl.BlockSpec((1, b, tn), lambda l, j: (l, 0, j)),
        out_shape=jax.ShapeDtypeStruct((depth, b, n), F32),
        compiler_params=pltpu.CompilerParams(
            dimension_semantics=("arbitrary", "arbitrary"), vmem_limit_bytes=VMEM_LIMIT),
        name="modulation",
    )(c, ada_w, ada_b.reshape(depth, 1, n))


def _swa_scores(q_pairs, kcat):
    low = _lane_iota((BLOCK, LANES)) < HEAD_DIM
    out = []
    for half in range(2):
        keep = low if half == 0 else jnp.logical_not(low)
        qs = jnp.concatenate([jnp.where(keep, qp, jnp.zeros_like(qp)) for qp in q_pairs], axis=0)
        out.append(lax.dot_general(qs, kcat, (((1,), (1,)), ((), ())), preferred_element_type=F32))
    return out


def _in_kernel(x_ref, mod_ref, n1w_ref, win_ref, wuq_ref, wukv_ref, qnw_ref, kvnw_ref,
               ca_ref, sa_ref, cb_ref, sb_ref, lnw_ref, lnb_ref, ws_ref, bs_ref, gwc_ref, sink_ref,
               ya_ref, qb_ref, kb_ref, vb_ref, yc_ref, kprev_ref, vprev_ref, band_ref):
    tm = x_ref.shape[1]
    mod = mod_ref[0]
    n1w = n1w_ref[...] * (1.0 + mod[1:2])
    lane = _lane_iota((IN_ROWS, LANES))
    a_first = (lane % HEAD_DIM) < (HEAD_DIM // 2)
    b_first = lane < (B_NOPE + B_ROPE // 2)

    first_tile = pl.program_id(1) == 0

    @pl.when(first_tile)
    def _():
        kprev_ref[...] = jnp.zeros_like(kprev_ref)
        vprev_ref[...] = jnp.zeros_like(vprev_ref)

    pairs = A_WIDTH // LANES
    qi = lax.broadcasted_iota(jnp.int32, (pairs * BLOCK, 2 * BLOCK), 0) % BLOCK
    kj = lax.broadcasted_iota(jnp.int32, (pairs * BLOCK, 2 * BLOCK), 1)
    rel = qi + BLOCK - kj
    band = (rel >= 0) & (rel < BLOCK)
    band_ref[...] = jnp.where(band, 0.0, NEG_INF)
    sinks = [jnp.concatenate([jnp.broadcast_to(sink_ref[2 * p + half:2 * p + half + 1, :], (BLOCK, LANES))
                              for p in range(pairs)], axis=0) for half in range(2)]
    ones_blk = jnp.ones((2 * BLOCK, LANES), BF16)
    low_a = _lane_iota((BLOCK, LANES)) < HEAD_DIM
    k_last = kprev_ref[...]
    v_last = vprev_ref[...]
    b_scale = float((B_NOPE + B_ROPE) ** -0.5) * LOG2E
    row = lax.broadcasted_iota(jnp.int32, (BLOCK, BLOCK), 0)
    col = lax.broadcasted_iota(jnp.int32, (BLOCK, BLOCK), 1)
    w_s = [jnp.where(col <= row, ws_ref[g], 0.0).astype(BF16) for g in range(C_GROUPS)]
    low = _lane_iota((BLOCK, LANES)) < C_GROUP_DIM
    bias = bs_ref[...]
    gwc = gwc_ref[...]

    def normed(r0):
        x = x_ref[0, r0:r0 + IN_ROWS, :]
        return (_rms(x, n1w) + mod[0:1]).astype(BF16)

    n_col_chunks = -(-IN_COLS_PAD // IN_COL_CHUNK)

    def project_chunk(h, c):
        lo = c * IN_COL_CHUNK
        return jnp.dot(h, win_ref[:, lo:min(lo + IN_COL_CHUNK, IN_COLS_PAD)], preferred_element_type=F32)

    starts = list(range(0, tm, IN_ROWS))
    h_first = normed(starts[0])
    cur = [project_chunk(h_first, c) for c in range(n_col_chunks)]
    for idx, r0 in enumerate(starts):
        has_next = idx + 1 < len(starts)
        h_next = normed(starts[idx + 1]) if has_next else None
        nxt = []

        def ahead():
            if has_next and len(nxt) < n_col_chunks:
                nxt.append(project_chunk(h_next, len(nxt)))

        def cols(lo, width, chunks=cur):
            c = lo // IN_COL_CHUNK
            assert (lo + width - 1) // IN_COL_CHUNK == c
            return chunks[c][:, lo - c * IN_COL_CHUNK:lo + width - c * IN_COL_CHUNK]

        rs = slice(r0, r0 + IN_ROWS)
        ahead()

        ca = ca_ref[0, rs, :]
        sa = sa_ref[0, rs, :]

        def rope_a(t):
            sw = jnp.where(a_first, pltpu.roll(t, LANES - HEAD_DIM // 2, 1), pltpu.roll(t, HEAD_DIM // 2, 1))
            return t * ca + sw * sa

        q_pairs = [(rope_a(cols(OFF_AQ + p * LANES, LANES)) * LOG2E).astype(BF16) for p in range(pairs)]
        ka = rope_a(cols(OFF_AK, A_KV_WIDTH)).astype(BF16)
        va = cols(OFF_AV, A_KV_WIDTH).astype(BF16)

        n_blk = IN_ROWS // BLOCK
        kcats, vcats, scs = [], [], []
        for nb in range(n_blk):
            if nb == 0:
                kcats.append(jnp.concatenate([k_last, ka[0:BLOCK]], axis=0))
                vcats.append(jnp.concatenate([v_last, va[0:BLOCK]], axis=0))
            else:
                kcats.append(ka[(nb - 1) * BLOCK:(nb + 1) * BLOCK])
                vcats.append(va[(nb - 1) * BLOCK:(nb + 1) * BLOCK])
            scs.append(_swa_scores([qp[nb * BLOCK:(nb + 1) * BLOCK] for qp in q_pairs], kcats[nb]))
        k_last = ka[IN_ROWS - BLOCK:]
        v_last = va[IN_ROWS - BLOCK:]

        cb = cb_ref[0, rs, :]
        sb = sb_ref[0, rs, :]

        def rope_b(t):
            sw = jnp.where(b_first, pltpu.roll(t, LANES - B_ROPE // 2, 1), pltpu.roll(t, B_ROPE // 2, 1))
            return t * cb + sw * sb

        ahead()
        cq = _rms(cols(OFF_BCQ, B_Q_RANK), qnw_ref[...]).astype(BF16)
        qb = jnp.dot(cq, wuq_ref[...], preferred_element_type=F32)
        ckv = _rms(cols(OFF_BCKV, B_KV_RANK), kvnw_ref[...]).astype(BF16)
        kv = jnp.dot(ckv, wukv_ref[...], preferred_element_type=F32)

        ms = []
        for nb in range(n_blk):
            for half in range(2):
                if idx == 0 and nb == 0:
                    seen = band & (kj >= jnp.where(first_tile, BLOCK, 0))
                    scs[nb][half] = jnp.where(seen, scs[nb][half], NEG_INF)
                else:
                    scs[nb][half] = scs[nb][half] + band_ref[...]
            ms.append([jnp.maximum(jnp.max(scs[nb][half], axis=-1, keepdims=True), sinks[half])
                       for half in range(2)])

        ahead()
        kr = rope_b(cols(OFF_BKR, LANES))
        for hh in range(B_HEADS):
            sl = slice(hh * LANES, (hh + 1) * LANES)
            qb_ref[0, rs, sl] = (rope_b(qb[:, sl]) * b_scale).astype(BF16)
            kb_ref[0, rs, sl] = (kv[:, sl] + kr).astype(BF16)
        vb_ref[0, rs, :] = kv[:, B_HEADS * LANES:].astype(BF16)

        pvs = []
        for nb in range(n_blk):
            pvs.append([])
            for half in range(2):
                e = jnp.concatenate([jnp.exp2(scs[nb][half][:, t * LANES:(t + 1) * LANES] - ms[nb][half])
                                     for t in range(2 * BLOCK // LANES)], axis=1).astype(BF16)
                pvs[nb].append(jnp.dot(e, jnp.concatenate([vcats[nb], ones_blk], axis=1),
                                       preferred_element_type=F32))

        ahead()
        u = _gelu(cols(OFF_CU, C_WIDTH))
        v = _gelu(cols(OFF_CV, C_WIDTH))
        mu = jnp.mean(v, axis=-1, keepdims=True)
        vc = v - mu
        var = jnp.mean(vc * vc, axis=-1, keepdims=True)
        v = (vc * lax.rsqrt(var + NORM_EPS) * lnw_ref[...] + lnb_ref[...]).astype(BF16)
        for c in range(IN_ROWS // BLOCK):
            rows = slice(c * BLOCK, (c + 1) * BLOCK)
            parts = []
            for p in range(C_WIDTH // LANES):
                vp = v[rows, p * LANES:(p + 1) * LANES]
                r_lo = jnp.dot(w_s[2 * p], vp, preferred_element_type=F32)
                r_hi = jnp.dot(w_s[2 * p + 1], vp, preferred_element_type=F32)
                parts.append(jnp.where(low, r_lo, r_hi))
            mixed = jnp.concatenate(parts, axis=-1) + bias
            yc = u[rows] * mixed
            yc_ref[0, r0 + c * BLOCK:r0 + (c + 1) * BLOCK, :] = _rms(yc, gwc).astype(BF16)

        for nb in range(n_blk):
            outs = [pvs[nb][half][:, :LANES] / (pvs[nb][half][:, LANES:] + jnp.exp2(sinks[half] - ms[nb][half]))
                    for half in range(2)]
            for p in range(pairs):
                blk = slice(p * BLOCK, (p + 1) * BLOCK)
                ya_ref[0, r0 + nb * BLOCK:r0 + (nb + 1) * BLOCK, p * LANES:(p + 1) * LANES] = jnp.where(
                    low_a, outs[0][blk], outs[1][blk])

        while has_next and len(nxt) < n_col_chunks:
            ahead()
        cur = nxt

    kprev_ref[...] = k_last
    vprev_ref[...] = v_last


def _in_proj(x, mod, n1w, win, wuq, wukv, qnw, kvnw, tabs, lnw, lnb, ws, bs_full, gwc, sink_tab, tm, layer):
    b, s, d = x.shape
    tok = lambda w: pl.BlockSpec((1, tm, w), lambda i, j: (i, j, 0))
    full = lambda a: pl.BlockSpec(a.shape, lambda i, j: (0,) * a.ndim)
    stacked = lambda a: pl.BlockSpec((None,) + a.shape[1:], lambda i, j: (layer,) + (0,) * (a.ndim - 1))
    ca, sa, cb, sb = tabs
    out_widths = (A_WIDTH, B_HEADS * B_QK_PAD, B_HEADS * B_QK_PAD, B_WIDTH, C_WIDTH)
    out_specs = [tok(w) for w in out_widths]
    out_shape = [jax.ShapeDtypeStruct((b, s, w), F32 if n == 0 else BF16) for n, w in enumerate(out_widths)]
    return pl.pallas_call(
        _in_kernel,
        grid=(b, s // tm),
        in_specs=[
            tok(d),
            pl.BlockSpec((1, N_MOD, d), lambda i, j: (i, 0, 0)),
            full(n1w), stacked(win), stacked(wuq), stacked(wukv), full(qnw), full(kvnw),
            tok(LANES), tok(LANES), tok(LANES), tok(LANES),
            full(lnw), full(lnb), stacked(ws), stacked(bs_full), full(gwc), stacked(sink_tab),
        ],
        out_specs=out_specs,
        out_shape=out_shape,
        scratch_shapes=[pltpu.VMEM((BLOCK, A_KV_WIDTH), BF16), pltpu.VMEM((BLOCK, A_KV_WIDTH), BF16),
                        pltpu.VMEM((A_WIDTH // LANES * BLOCK, 2 * BLOCK), F32)],
        compiler_params=pltpu.CompilerParams(
            dimension_semantics=("arbitrary", "arbitrary"), vmem_limit_bytes=VMEM_LIMIT),
        name="in_proj",
    )(x, mod, n1w, win, wuq, wukv, qnw, kvnw, ca, sa, cb, sb, lnw, lnb, ws, bs_full, gwc, sink_tab)


def _mla_tile(q_ref, k_ref, v_ref, o_ref, tile):
    tq = q_ref.shape[1]
    rows = MLA_ROWS
    qi = lax.broadcasted_iota(jnp.int32, (rows, rows), 0)
    kj = lax.broadcasted_iota(jnp.int32, (rows, rows), 1)
    causal = kj <= qi
    nt = (((1,), (1,)), ((), ()))

    subs = list(range(tq // rows))
    segments = {}
    for sub in subs:
        first_row = tile * tq + sub * rows
        segments[sub] = [(lo, min(MLA_KV_SEG, first_row - lo), False) for lo in range(0, first_row, MLA_KV_SEG)]
        segments[sub].append((first_row, rows, True))
    chains = [(sub, hh) for sub in subs for hh in range(2)]
    state = {ch: None for ch in chains}

    for rnd in range(max(len(s) for s in segments.values())):
        live = [ch for ch in chains if rnd < len(segments[ch[0]])]
        v_aug = {}
        for sub in sorted({ch[0] for ch in live}):
            lo, width, _ = segments[sub][rnd]
            v = v_ref[0, lo:lo + width, :]
            low_v = _lane_iota(v.shape) < B_V
            one = jnp.ones_like(v)
            v_aug[sub] = (jnp.where(low_v, v, one), jnp.where(low_v, one, v))
        sc, m_new = {}, {}
        for sub, hh in live:
            lo, width, _ = segments[sub][rnd]
            q = q_ref[0, sub * rows:(sub + 1) * rows, hh * LANES:(hh + 1) * LANES]
            k = k_ref[0, lo:lo + width, hh * LANES:(hh + 1) * LANES]
            sc[sub, hh] = lax.dot_general(q, k, nt, preferred_element_type=F32)
        for ch in live:
            if segments[ch[0]][rnd][2]:
                sc[ch] = jnp.where(causal, sc[ch], NEG_INF)
            m_blk = jnp.max(sc[ch], axis=-1, keepdims=True)
            m_new[ch] = m_blk if state[ch] is None else jnp.maximum(state[ch][0], m_blk)
        pv = {ch: jnp.dot(jnp.exp2(sc[ch] - m_new[ch]).astype(BF16), v_aug[ch[0]][ch[1]],
                          preferred_element_type=F32) for ch in live}
        for ch in live:
            if state[ch] is None:
                acc = pv[ch]
            else:
                m, acc = state[ch]
                acc = jnp.exp2(m - m_new[ch]) * acc + pv[ch]
            state[ch] = (m_new[ch], acc)

    low = _lane_iota((rows, LANES)) < B_V
    for sub in subs:
        acc0, acc1 = state[sub, 0][1], state[sub, 1][1]
        num = jnp.where(low, acc0, acc1)
        den = pltpu.roll(jnp.where(low, acc1, acc0), B_V, 1)
        o_ref[0, sub * rows:(sub + 1) * rows, :] = num / den


def _mla_kernel(q_ref, k_ref, v_ref, o_ref):
    i = pl.program_id(2)
    for tile in range(k_ref.shape[1] // q_ref.shape[1]):
        pl.when(i == tile)(functools.partial(_mla_tile, q_ref, k_ref, v_ref, o_ref, tile))


def _mla(qb, kb, vb, tq):
    b, s, _ = qb.shape
    pairs = B_HEADS // 2
    return pl.pallas_call(
        _mla_kernel,
        grid=(b, pairs, s // tq),
        in_specs=[
            pl.BlockSpec((1, tq, 2 * B_QK_PAD), lambda i, p, j: (i, j, p)),
            pl.BlockSpec((1, s, 2 * B_QK_PAD), lambda i, p, j: (i, 0, p)),
            pl.BlockSpec((1, s, LANES), lambda i, p, j: (i, 0, p)),
        ],
        out_specs=pl.BlockSpec((1, tq, LANES), lambda i, p, j: (i, j, p)),
        out_shape=jax.ShapeDtypeStruct((b, s, B_WIDTH), F32),
        compiler_params=pltpu.CompilerParams(
            dimension_semantics=("arbitrary", "arbitrary", "arbitrary"), vmem_limit_bytes=VMEM_LIMIT),
        name="mla",
    )(qb, kb, vb)


def _out_kernel(x_ref, ya_ref, yb_ref, yc_ref, mod_ref, gwa_ref, gwb_ref, wout_ref, n2w_ref,
                wgu_ref, wdn_ref, fw_ref, o_ref, *, final):
    tm = x_ref.shape[1]
    mod = mod_ref[0]
    ffn = wdn_ref.shape[0]
    n2w = n2w_ref[...] * (1.0 + mod[4:5])

    def mix(r0):
        rs = slice(r0, r0 + OUT_ROWS)
        y = jnp.concatenate([
            _rms(ya_ref[0, rs, :], gwa_ref[...]).astype(BF16),
            _rms(yb_ref[0, rs, :], gwb_ref[...]).astype(BF16),
            yc_ref[0, rs, :],
        ], axis=-1)
        x1 = x_ref[0, rs, :] + mod[2:3] * jnp.dot(y, wout_ref[...], preferred_element_type=F32)
        h = (_rms(x1, n2w) + mod[3:4]).astype(BF16)
        return x1, h

    starts = list(range(0, tm, OUT_ROWS))
    mixed = {starts[0]: mix(starts[0])}
    for idx, r0 in enumerate(starts):
        if idx + 1 < len(starts):
            mixed[starts[idx + 1]] = mix(starts[idx + 1])
        x1, h = mixed.pop(r0)
        acc = None
        for lo in range(0, ffn, FFN_CHUNK):
            g = jnp.dot(h, wgu_ref[:, lo:lo + FFN_CHUNK], preferred_element_type=F32)
            u = jnp.dot(h, wgu_ref[:, ffn + lo:ffn + lo + FFN_CHUNK], preferred_element_type=F32)
            a = (g * jax.nn.sigmoid(g) * u).astype(BF16)
            part = jnp.dot(a, wdn_ref[lo:lo + FFN_CHUNK, :], preferred_element_type=F32)
            acc = part if acc is None else acc + part
        x2 = x1 + mod[5:6] * acc
        if final:
            x2 = _rms(x2, fw_ref[...])
        o_ref[0, r0:r0 + OUT_ROWS, :] = x2


def _out_ffn(x, ya, yb, yc, mod, gwa, gwb, wout, n2w, wgu, wdn, fw, tm, layer, final):
    b, s, d = x.shape
    tok = lambda w: pl.BlockSpec((1, tm, w), lambda i, j: (i, j, 0))
    full = lambda a: pl.BlockSpec(a.shape, lambda i, j: (0,) * a.ndim, pipeline_mode=pl.Buffered(1))
    stacked = lambda a: pl.BlockSpec((None,) + a.shape[1:], lambda i, j: (layer,) + (0,) * (a.ndim - 1),
                                     pipeline_mode=pl.Buffered(1))
    return pl.pallas_call(
        functools.partial(_out_kernel, final=final),
        grid=(b, s // tm),
        in_specs=[
            tok(d), tok(A_WIDTH), tok(B_WIDTH), tok(C_WIDTH),
            pl.BlockSpec((1, N_MOD, d), lambda i, j: (i, 0, 0)),
            full(gwa), full(gwb), stacked(wout), full(n2w), stacked(wgu), stacked(wdn), full(fw),
        ],
        out_specs=tok(d),
        out_shape=jax.ShapeDtypeStruct((b, s, d), F32),
        compiler_params=pltpu.CompilerParams(
            dimension_semantics=("arbitrary", "arbitrary"), vmem_limit_bytes=VMEM_LIMIT),
        name="out_ffn",
    )(x, ya, yb, yc, mod, gwa, gwb, wout, n2w, wgu, wdn, fw)


def _in_col_index():
    src_aq, src_ak, src_av = 0, A_WIDTH, A_WIDTH + A_KV_WIDTH
    src_bcq = src_av + A_KV_WIDTH
    src_bckv = src_bcq + B_Q_RANK
    src_bkr = src_bckv + B_KV_RANK
    src_cu = src_bkr + B_ROPE
    src_cv = src_cu + C_WIDTH
    zero = src_cv + C_WIDTH
    idx = []
    for hh in A_HEAD_ORDER:
        idx += list(range(src_aq + hh * HEAD_DIM, src_aq + (hh + 1) * HEAD_DIM))
    idx += list(range(src_ak, src_bkr))
    idx += list(range(src_cu, zero))
    idx += [zero] * B_NOPE + list(range(src_bkr, src_bkr + B_ROPE)) + [zero] * (LANES - B_NOPE - B_ROPE)
    assert len(idx) == IN_COLS_PAD
    return np.asarray(idx, np.int32)


def _take_static(a, idx, axis):
    n = a.shape[axis]
    idx = [int(t) for t in idx]
    parts, start = [], 0
    for pos in range(1, len(idx) + 1):
        if pos < len(idx):
            prev, cur = idx[pos - 1], idx[pos]
            same_run = (prev == n and cur == n) or (prev != n and cur != n and cur == prev + 1)
        else:
            same_run = False
        if not same_run:
            first, count = idx[start], pos - start
            if first == n:
                shape = a.shape[:axis] + (count,) + a.shape[axis + 1:]
                parts.append(jnp.zeros(shape, a.dtype))
            else:
                parts.append(lax.slice_in_dim(a, first, first + count, axis=axis))
            start = pos
    return jnp.concatenate(parts, axis=axis)


def _rope_expand_kernel(ac_ref, as_ref, bc_ref, bs_ref, ca_ref, sa_ref, cb_ref, sb_ref):
    half_a, half_b = HEAD_DIM // 2, B_ROPE // 2
    ra, rb = ac_ref.shape[1], bc_ref.shape[1]
    lane_a = _lane_iota((ra, LANES))
    lane_b = _lane_iota((rb, LANES))
    sign_a = jnp.where((lane_a % HEAD_DIM) < half_a, -1.0, 1.0)
    sign_b = jnp.where(lane_b < B_NOPE + half_b, -1.0, 1.0)
    ones_b = jnp.where(lane_b < B_NOPE, 1.0, 0.0)
    per_a, per_b = LANES // half_a, LANES // half_b
    for j in range(per_a):
        pick = (lane_a // half_a) == j
        for src, dst, sign in ((ac_ref, ca_ref, None), (as_ref, sa_ref, sign_a)):
            t = jnp.where(pick, src[0], 0.0)
            width = half_a
            while width < LANES:
                t = t + pltpu.roll(t, width, 1)
                width *= 2
            dst[0, pl.ds(j, ra, stride=per_a), :] = t if sign is None else t * sign
    for j in range(per_b):
        pick = (lane_b // half_b) == j
        for src, dst, is_cos in ((bc_ref, cb_ref, True), (bs_ref, sb_ref, False)):
            y = jnp.where(pick, src[0], 0.0)
            t = None
            for target in (B_NOPE, B_NOPE + half_b):
                shift = (target - j * half_b) % LANES
                piece = y if shift == 0 else pltpu.roll(y, shift, 1)
                t = piece if t is None else t + piece
            dst[0, pl.ds(j, rb, stride=per_b), :] = t + ones_b if is_cos else t * sign_b


def _rope_tables(positions):
    b, s = positions.shape
    pos = positions.astype(F32)[..., None]

    def packed(dim):
        half = dim // 2
        per_row = LANES // half
        inv = 1.0 / (ROPE_THETA ** (jnp.arange(0, dim, 2, dtype=F32) / dim))
        ang = (pos.reshape(b, s // per_row, per_row, 1) * inv).reshape(b, s // per_row, LANES)
        return jnp.cos(ang), jnp.sin(ang)

    ac, asn = packed(HEAD_DIM)
    bc, bsn = packed(B_ROPE)
    te = min(ROPE_TILE, s)
    per_a, per_b = LANES // (HEAD_DIM // 2), LANES // (B_ROPE // 2)
    tok = pl.BlockSpec((1, te, LANES), lambda i, j: (i, j, 0))
    return pl.pallas_call(
        _rope_expand_kernel,
        grid=(b, s // te),
        in_specs=[pl.BlockSpec((1, te // per_a, LANES), lambda i, j: (i, j, 0))] * 2
        + [pl.BlockSpec((1, te // per_b, LANES), lambda i, j: (i, j, 0))] * 2,
        out_specs=[tok] * 4,
        out_shape=[jax.ShapeDtypeStruct((b, s, LANES), F32)] * 4,
        compiler_params=pltpu.CompilerParams(
            dimension_semantics=("arbitrary", "arbitrary"), vmem_limit_bytes=VMEM_LIMIT),
        name="rope_tables",
    )(ac, asn, bc, bsn)


def kernel(x, c, positions, ada_w, ada_b, norm1_w, w_in, a_sinks, b_q_norm_w, b_w_uq, b_kv_norm_w, b_w_ukv,
           c_ln_w, c_ln_b, c_w_s, c_b_s, out_norm_w, w_out, norm2_w, w_gate_up, w_down, final_norm_w):
    depth = w_in.shape[0]
    b, s, d = x.shape
    ffn = w_down.shape[1]
    tm_in, tm_out, tq = min(IN_TILE, s), min(OUT_TILE, s), min(ATTN_TILE, s)

    mod = _modulation(c, ada_w, ada_b).reshape(depth, b, N_MOD, d)
    tabs = _rope_tables(positions)

    col_scale = np.ones((w_in.shape[2],), np.float32)
    col_scale[:A_WIDTH] = HEAD_DIM ** -0.5
    w_in_p = _take_static((w_in * col_scale).astype(BF16), _in_col_index(), 2)

    w_uq = b_w_uq.reshape(depth, B_Q_RANK, B_HEADS, B_NOPE + B_ROPE)
    w_uq = jnp.pad(w_uq, ((0, 0), (0, 0), (0, 0), (0, B_QK_PAD - B_NOPE - B_ROPE)))
    w_uq = w_uq.reshape(depth, B_Q_RANK, B_HEADS * B_QK_PAD).astype(BF16)
    w_ukv = b_w_ukv.reshape(depth, B_KV_RANK, B_HEADS, B_NOPE + B_V)
    w_uk = jnp.pad(w_ukv[..., :B_NOPE], ((0, 0), (0, 0), (0, 0), (0, B_QK_PAD - B_NOPE)))
    w_ukv = jnp.concatenate([w_uk.reshape(depth, B_KV_RANK, B_HEADS * B_QK_PAD),
                             w_ukv[..., B_NOPE:].reshape(depth, B_KV_RANK, B_WIDTH)], axis=-1).astype(BF16)

    bs_full = jnp.repeat(jnp.swapaxes(c_b_s, 1, 2), C_GROUP_DIM, axis=-1)
    order = np.asarray(A_HEAD_ORDER)
    sink_tab = jnp.broadcast_to((a_sinks * LOG2E)[:, order, None], (depth, A_Q_HEADS, LANES))
    sink_tab = jnp.pad(sink_tab, ((0, 0), (0, 8 - A_Q_HEADS), (0, 0)))

    a_perm = (order[:, None] * HEAD_DIM + np.arange(HEAD_DIM)[None, :]).reshape(-1)
    out_perm = np.concatenate([a_perm, np.arange(A_WIDTH, d)])
    gw = _take_static(out_norm_w, out_perm, 1)
    w_out_p = _take_static(w_out.astype(BF16), out_perm, 1)

    assert ffn % FFN_CHUNK == 0
    w_gu = w_gate_up.astype(BF16)
    w_dn = w_down.astype(BF16)

    row = lambda a: a.reshape(1, -1)
    for l in range(depth):
        ya, qb, kb, vb, yc = _in_proj(
            x, mod[l], row(norm1_w[l]), w_in_p, w_uq, w_ukv, row(b_q_norm_w[l]), row(b_kv_norm_w[l]),
            tabs, row(c_ln_w[l]), row(c_ln_b[l]), c_w_s, bs_full, row(gw[l, A_WIDTH + B_WIDTH:]), sink_tab,
            tm_in, l)
        yb = _mla(qb, kb, vb, tq)
        x = _out_ffn(x, ya, yb, yc, mod[l], row(gw[l, :A_WIDTH]), row(gw[l, A_WIDTH:A_WIDTH + B_WIDTH]),
                     w_out_p, row(norm2_w[l]), w_gu, w_dn, row(final_norm_w), tm_out, l, l == depth - 1)
    return x
```

```python
import functools

import numpy as np
import jax
import jax.numpy as jnp
from jax import lax
from jax.experimental import pallas as pl
from jax.experimental.pallas import tpu as pltpu

F32 = jnp.float32
BF16 = jnp.bfloat16

LANES = 128
HEAD_DIM = 64
ROPE_THETA = 10000.0
NORM_EPS = 1e-6
NEG_INF = -1e30
LOG2E = float(np.log2(np.e))
BLOCK = 128

A_Q_HEADS = 6
A_KV_HEADS = 2
A_WIDTH = A_Q_HEADS * HEAD_DIM
A_KV_WIDTH = A_KV_HEADS * HEAD_DIM
A_HEAD_ORDER = (0, 3, 1, 4, 2, 5)

B_HEADS = 6
B_Q_RANK = 384
B_KV_RANK = 256
B_NOPE = 64
B_ROPE = 32
B_V = 64
B_WIDTH = B_HEADS * B_V
B_QK_PAD = LANES

C_GROUPS = 4
C_GROUP_DIM = 64
C_WIDTH = C_GROUPS * C_GROUP_DIM

N_MOD = 6

OFF_AQ = 0
OFF_AK = OFF_AQ + A_WIDTH
OFF_AV = OFF_AK + A_KV_WIDTH
OFF_BCQ = OFF_AV + A_KV_WIDTH
OFF_BCKV = OFF_BCQ + B_Q_RANK
OFF_CU = OFF_BCKV + B_KV_RANK
OFF_CV = OFF_CU + C_WIDTH
OFF_BKR = OFF_CV + C_WIDTH
IN_COLS_PAD = OFF_BKR + LANES

VMEM_LIMIT = 56 * 1024 * 1024
MLA_KV_SEG = 2048
IN_ROWS = 256
IN_TILE = 1024
IN_COL_CHUNK = 512
OUT_TILE = 1024
OUT_ROWS = 512
ATTN_TILE = 2048
MLA_ROWS = 256
FFN_CHUNK = 256
MOD_COLS = 1536
ROPE_TILE = 2048


def _rms(x, w):
    return x * lax.rsqrt(jnp.mean(x * x, axis=-1, keepdims=True) + NORM_EPS) * w


def _gelu(x):
    return 0.5 * x * (1.0 + lax.erf(x * float(np.sqrt(0.5))))


def _lane_iota(shape):
    return lax.broadcasted_iota(jnp.int32, shape, len(shape) - 1)


def _mod_kernel(c_ref, w_ref, b_ref, o_ref):
    c = c_ref[...]
    act = (c * jax.nn.sigmoid(c)).astype(BF16)
    o_ref[0] = jnp.dot(act, w_ref[0].astype(BF16), preferred_element_type=F32) + b_ref[0]


def _modulation(c, ada_w, ada_b):
    depth, d, n = ada_w.shape
    b = c.shape[0]
    tn = MOD_COLS
    return pl.pallas_call(
        _mod_kernel,
        grid=(depth, n // tn),
        in_specs=[
            pl.BlockSpec((b, d), lambda l, j: (0, 0)),
            pl.BlockSpec((1, d, tn), lambda l, j: (l, 0, j)),
            pl.BlockSpec((1, 1, tn), lambda l, j: (l, 0, j)),
        ],
        out_specs=pl.BlockSpec((1, b, tn), lambda l, j: (l, 0, j)),
        out_shape=jax.ShapeDtypeStruct((depth, b, n), F32),
        compiler_params=pltpu.CompilerParams(
            dimension_semantics=("arbitrary", "arbitrary"), vmem_limit_bytes=VMEM_LIMIT),
        name="modulation",
    )(c, ada_w, ada_b.reshape(depth, 1, n))


def _swa_scores(q_pairs, kcat):
    low = _lane_iota((BLOCK, LANES)) < HEAD_DIM
    out = []
    for half in range(2):
        keep = low if half == 0 else jnp.logical_not(low)
        qs = jnp.concatenate([jnp.where(keep, qp, jnp.zeros_like(qp)) for qp in q_pairs], axis=0)
        out.append(lax.dot_general(qs, kcat, (((1,), (1,)), ((), ())), preferred_element_type=F32))
    return out


def _in_kernel(x_ref, mod_ref, n1w_ref, win_ref, wuq_ref, wukv_ref, qnw_ref, kvnw_ref,
               ca_ref, sa_ref, cb_ref, sb_ref, lnw_ref, lnb_ref, ws_ref, bs_ref, gwc_ref, sink_ref,
               ya_ref, qb_ref, kb_ref, vb_ref, yc_ref, kprev_ref, vprev_ref, band_ref):
    tm = x_ref.shape[1]
    mod = mod_ref[0]
    n1w = n1w_ref[...] * (1.0 + mod[1:2])
    lane = _lane_iota((IN_ROWS, LANES))
    a_first = (lane % HEAD_DIM) < (HEAD_DIM // 2)
    b_first = lane < (B_NOPE + B_ROPE // 2)

    first_tile = pl.program_id(1) == 0

    @pl.when(first_tile)
    def _():
        kprev_ref[...] = jnp.zeros_like(kprev_ref)
        vprev_ref[...] = jnp.zeros_like(vprev_ref)

    pairs = A_WIDTH // LANES
    qi = lax.broadcasted_iota(jnp.int32, (pairs * BLOCK, 2 * BLOCK), 0) % BLOCK
    kj = lax.broadcasted_iota(jnp.int32, (pairs * BLOCK, 2 * BLOCK), 1)
    rel = qi + BLOCK - kj
    band = (rel >= 0) & (rel < BLOCK)
    band_ref[...] = jnp.where(band, 0.0, NEG_INF)
    sinks = [jnp.concatenate([jnp.broadcast_to(sink_ref[2 * p + half:2 * p + half + 1, :], (BLOCK, LANES))
                              for p in range(pairs)], axis=0) for half in range(2)]
    ones_blk = jnp.ones((2 * BLOCK, LANES), BF16)
    low_a = _lane_iota((BLOCK, LANES)) < HEAD_DIM
    k_last = kprev_ref[...]
    v_last = vprev_ref[...]
    b_scale = float((B_NOPE + B_ROPE) ** -0.5) * LOG2E
    row = lax.broadcasted_iota(jnp.int32, (BLOCK, BLOCK), 0)
    col = lax.broadcasted_iota(jnp.int32, (BLOCK, BLOCK), 1)
    w_s = [jnp.where(col <= row, ws_ref[g], 0.0).astype(BF16) for g in range(C_GROUPS)]
    low = _lane_iota((BLOCK, LANES)) < C_GROUP_DIM
    bias = bs_ref[...]
    gwc = gwc_ref[...]

    def normed(r0):
        x = x_ref[0, r0:r0 + IN_ROWS, :]
        return (_rms(x, n1w) + mod[0:1]).astype(BF16)

    n_col_chunks = -(-IN_COLS_PAD // IN_COL_CHUNK)

    def project_chunk(h, c):
        lo = c * IN_COL_CHUNK
        return jnp.dot(h, win_ref[:, lo:min(lo + IN_COL_CHUNK, IN_COLS_PAD)], preferred_element_type=F32)

    starts = list(range(0, tm, IN_ROWS))
    h_first = normed(starts[0])
    cur = [project_chunk(h_first, c) for c in range(n_col_chunks)]
    for idx, r0 in enumerate(starts):
        has_next = idx + 1 < len(starts)
        h_next = normed(starts[idx + 1]) if has_next else None
        nxt = []

        def ahead():
            if has_next and len(nxt) < n_col_chunks:
                nxt.append(project_chunk(h_next, len(nxt)))

        def cols(lo, width, chunks=cur):
            c = lo // IN_COL_CHUNK
            assert (lo + width - 1) // IN_COL_CHUNK == c
            return chunks[c][:, lo - c * IN_COL_CHUNK:lo + width - c * IN_COL_CHUNK]

        rs = slice(r0, r0 + IN_ROWS)
        ahead()

        ca = ca_ref[0, rs, :]
        sa = sa_ref[0, rs, :]

        def rope_a(t):
            sw = jnp.where(a_first, pltpu.roll(t, LANES - HEAD_DIM // 2, 1), pltpu.roll(t, HEAD_DIM // 2, 1))
            return t * ca + sw * sa

        a_scale = LOG2E * HEAD_DIM ** -0.5
        q_pairs = [(rope_a(cols(OFF_AQ + p * LANES, LANES)) * a_scale).astype(BF16) for p in range(pairs)]
        ka = rope_a(cols(OFF_AK, A_KV_WIDTH)).astype(BF16)
        va = cols(OFF_AV, A_KV_WIDTH).astype(BF16)

        n_blk = IN_ROWS // BLOCK
        kcats, vcats, scs = [], [], []
        for nb in range(n_blk):
            if nb == 0:
                kcats.append(jnp.concatenate([k_last, ka[0:BLOCK]], axis=0))
                vcats.append(jnp.concatenate([v_last, va[0:BLOCK]], axis=0))
            else:
                kcats.append(ka[(nb - 1) * BLOCK:(nb + 1) * BLOCK])
                vcats.append(va[(nb - 1) * BLOCK:(nb + 1) * BLOCK])
            scs.append(_swa_scores([qp[nb * BLOCK:(nb + 1) * BLOCK] for qp in q_pairs], kcats[nb]))
        k_last = ka[IN_ROWS - BLOCK:]
        v_last = va[IN_ROWS - BLOCK:]

        cb = cb_ref[0, rs, :]
        sb = sb_ref[0, rs, :]

        def rope_b(t):
            sw = jnp.where(b_first, pltpu.roll(t, LANES - B_ROPE // 2, 1), pltpu.roll(t, B_ROPE // 2, 1))
            return t * cb + sw * sb

        ahead()
        cq = _rms(cols(OFF_BCQ, B_Q_RANK), qnw_ref[...]).astype(BF16)
        qb = jnp.dot(cq, wuq_ref[...], preferred_element_type=F32)
        ckv = _rms(cols(OFF_BCKV, B_KV_RANK), kvnw_ref[...]).astype(BF16)
        kv = jnp.dot(ckv, wukv_ref[...], preferred_element_type=F32)

        ms = []
        for nb in range(n_blk):
            for half in range(2):
                if idx == 0 and nb == 0:
                    seen = band & (kj >= jnp.where(first_tile, BLOCK, 0))
                    scs[nb][half] = jnp.where(seen, scs[nb][half], NEG_INF)
                else:
                    scs[nb][half] = scs[nb][half] + band_ref[...]
            ms.append([jnp.maximum(jnp.max(scs[nb][half], axis=-1, keepdims=True), sinks[half])
                       for half in range(2)])

        ahead()
        kr = rope_b(cols(OFF_BKR, LANES))
        for hh in range(B_HEADS):
            sl = slice(hh * LANES, (hh + 1) * LANES)
            qb_ref[0, rs, sl] = (rope_b(qb[:, sl]) * b_scale).astype(BF16)
            kb_ref[0, rs, sl] = (kv[:, sl] + kr).astype(BF16)
        vb_ref[0, rs, :] = kv[:, B_HEADS * LANES:].astype(BF16)

        pvs = []
        for nb in range(n_blk):
            pvs.append([])
            for half in range(2):
                e = jnp.concatenate([jnp.exp2(scs[nb][half][:, t * LANES:(t + 1) * LANES] - ms[nb][half])
                                     for t in range(2 * BLOCK // LANES)], axis=1).astype(BF16)
                pvs[nb].append(jnp.dot(e, jnp.concatenate([vcats[nb], ones_blk], axis=1),
                                       preferred_element_type=F32))

        ahead()
        u = _gelu(cols(OFF_CU, C_WIDTH))
        v = _gelu(cols(OFF_CV, C_WIDTH))
        mu = jnp.mean(v, axis=-1, keepdims=True)
        vc = v - mu
        var = jnp.mean(vc * vc, axis=-1, keepdims=True)
        v = (vc * lax.rsqrt(var + NORM_EPS) * lnw_ref[...] + lnb_ref[...]).astype(BF16)
        for c in range(IN_ROWS // BLOCK):
            rows = slice(c * BLOCK, (c + 1) * BLOCK)
            parts = []
            for p in range(C_WIDTH // LANES):
                vp = v[rows, p * LANES:(p + 1) * LANES]
                r_lo = jnp.dot(w_s[2 * p], vp, preferred_element_type=F32)
                r_hi = jnp.dot(w_s[2 * p + 1], vp, preferred_element_type=F32)
                parts.append(jnp.where(low, r_lo, r_hi))
            mixed = jnp.concatenate(parts, axis=-1) + bias
            yc = u[rows] * mixed
            yc_ref[0, r0 + c * BLOCK:r0 + (c + 1) * BLOCK, :] = _rms(yc, gwc).astype(BF16)

        for nb in range(n_blk):
            outs = [pvs[nb][half][:, :LANES] / (pvs[nb][half][:, LANES:] + jnp.exp2(sinks[half] - ms[nb][half]))
                    for half in range(2)]
            for p in range(pairs):
                blk = slice(p * BLOCK, (p + 1) * BLOCK)
                ya_ref[0, r0 + nb * BLOCK:r0 + (nb + 1) * BLOCK, p * LANES:(p + 1) * LANES] = jnp.where(
                    low_a, outs[0][blk], outs[1][blk])

        while has_next and len(nxt) < n_col_chunks:
            ahead()
        cur = nxt

    kprev_ref[...] = k_last
    vprev_ref[...] = v_last


def _in_proj(x, mod, n1w, win, wuq, wukv, qnw, kvnw, tabs, lnw, lnb, ws, bs_full, gwc, sink_tab, tm, layer):
    b, s, d = x.shape
    tok = lambda w: pl.BlockSpec((1, tm, w), lambda i, j: (i, j, 0))
    full = lambda a: pl.BlockSpec(a.shape, lambda i, j: (0,) * a.ndim)
    stacked = lambda a: pl.BlockSpec((None,) + a.shape[1:], lambda i, j: (layer,) + (0,) * (a.ndim - 1))
    ca, sa, cb, sb = tabs
    out_widths = (A_WIDTH, B_HEADS * B_QK_PAD, B_HEADS * B_QK_PAD, B_WIDTH, C_WIDTH)
    out_specs = [tok(w) for w in out_widths]
    out_shape = [jax.ShapeDtypeStruct((b, s, w), F32 if n == 0 else BF16) for n, w in enumerate(out_widths)]
    return pl.pallas_call(
        _in_kernel,
        grid=(b, s // tm),
        in_specs=[
            tok(d),
            pl.BlockSpec((1, N_MOD, d), lambda i, j: (i, 0, 0)),
            full(n1w), stacked(win), stacked(wuq), stacked(wukv), full(qnw), full(kvnw),
            tok(LANES), tok(LANES), tok(LANES), tok(LANES),
            full(lnw), full(lnb), stacked(ws), stacked(bs_full), full(gwc), stacked(sink_tab),
        ],
        out_specs=out_specs,
        out_shape=out_shape,
        scratch_shapes=[pltpu.VMEM((BLOCK, A_KV_WIDTH), BF16), pltpu.VMEM((BLOCK, A_KV_WIDTH), BF16),
                        pltpu.VMEM((A_WIDTH // LANES * BLOCK, 2 * BLOCK), F32)],
        compiler_params=pltpu.CompilerParams(
            dimension_semantics=("arbitrary", "arbitrary"), vmem_limit_bytes=VMEM_LIMIT),
        name="in_proj",
    )(x, mod, n1w, win, wuq, wukv, qnw, kvnw, ca, sa, cb, sb, lnw, lnb, ws, bs_full, gwc, sink_tab)


def _mla_tile(q_ref, k_ref, v_ref, o_ref, tile):
    tq = q_ref.shape[1]
    rows = MLA_ROWS
    qi = lax.broadcasted_iota(jnp.int32, (rows, rows), 0)
    kj = lax.broadcasted_iota(jnp.int32, (rows, rows), 1)
    causal = kj <= qi
    nt = (((1,), (1,)), ((), ()))

    subs = list(range(tq // rows))
    segments = {}
    for sub in subs:
        first_row = tile * tq + sub * rows
        segments[sub] = [(lo, min(MLA_KV_SEG, first_row - lo), False) for lo in range(0, first_row, MLA_KV_SEG)]
        segments[sub].append((first_row, rows, True))
    chains = [(sub, hh) for sub in subs for hh in range(2)]
    state = {ch: None for ch in chains}

    for rnd in range(max(len(s) for s in segments.values())):
        live = [ch for ch in chains if rnd < len(segments[ch[0]])]
        v_aug = {}
        for sub in sorted({ch[0] for ch in live}):
            lo, width, _ = segments[sub][rnd]
            v = v_ref[0, lo:lo + width, :]
            low_v = _lane_iota(v.shape) < B_V
            one = jnp.ones_like(v)
            v_aug[sub] = (jnp.where(low_v, v, one), jnp.where(low_v, one, v))
        sc, m_new = {}, {}
        for sub, hh in live:
            lo, width, _ = segments[sub][rnd]
            q = q_ref[0, sub * rows:(sub + 1) * rows, hh * LANES:(hh + 1) * LANES]
            k = k_ref[0, lo:lo + width, hh * LANES:(hh + 1) * LANES]
            sc[sub, hh] = lax.dot_general(q, k, nt, preferred_element_type=F32)
        for ch in live:
            if segments[ch[0]][rnd][2]:
                sc[ch] = jnp.where(causal, sc[ch], NEG_INF)
            m_blk = jnp.max(sc[ch], axis=-1, keepdims=True)
            m_new[ch] = m_blk if state[ch] is None else jnp.maximum(state[ch][0], m_blk)
        pv = {ch: jnp.dot(jnp.exp2(sc[ch] - m_new[ch]).astype(BF16), v_aug[ch[0]][ch[1]],
                          preferred_element_type=F32) for ch in live}
        for ch in live:
            if state[ch] is None:
                acc = pv[ch]
            else:
                m, acc = state[ch]
                acc = jnp.exp2(m - m_new[ch]) * acc + pv[ch]
            state[ch] = (m_new[ch], acc)

    low = _lane_iota((rows, LANES)) < B_V
    for sub in subs:
        acc0, acc1 = state[sub, 0][1], state[sub, 1][1]
        num = jnp.where(low, acc0, acc1)
        den = pltpu.roll(jnp.where(low, acc1, acc0), B_V, 1)
        o_ref[0, sub * rows:(sub + 1) * rows, :] = num / den


def _mla_kernel(q_ref, k_ref, v_ref, o_ref):
    i = pl.program_id(2)
    for tile in range(k_ref.shape[1] // q_ref.shape[1]):
        pl.when(i == tile)(functools.partial(_mla_tile, q_ref, k_ref, v_ref, o_ref, tile))


def _mla(qb, kb, vb, tq):
    b, s, _ = qb.shape
    pairs = B_HEADS // 2
    return pl.pallas_call(
        _mla_kernel,
        grid=(b, pairs, s // tq),
        in_specs=[
            pl.BlockSpec((1, tq, 2 * B_QK_PAD), lambda i, p, j: (i, j, p)),
            pl.BlockSpec((1, s, 2 * B_QK_PAD), lambda i, p, j: (i, 0, p)),
            pl.BlockSpec((1, s, LANES), lambda i, p, j: (i, 0, p)),
        ],
        out_specs=pl.BlockSpec((1, tq, LANES), lambda i, p, j: (i, j, p)),
        out_shape=jax.ShapeDtypeStruct((b, s, B_WIDTH), F32),
        compiler_params=pltpu.CompilerParams(
            dimension_semantics=("arbitrary", "arbitrary", "arbitrary"), vmem_limit_bytes=VMEM_LIMIT),
        name="mla",
    )(qb, kb, vb)


def _out_kernel(x_ref, ya_ref, yb_ref, yc_ref, mod_ref, gwa_ref, gwb_ref, wout_ref, n2w_ref,
                wgu_ref, wdn_ref, fw_ref, o_ref, *, final):
    tm = x_ref.shape[1]
    mod = mod_ref[0]
    ffn = wdn_ref.shape[0]
    n2w = n2w_ref[...] * (1.0 + mod[4:5])

    def mix(r0):
        rs = slice(r0, r0 + OUT_ROWS)
        y = jnp.concatenate([
            _rms(ya_ref[0, rs, :], gwa_ref[...]).astype(BF16),
            _rms(yb_ref[0, rs, :], gwb_ref[...]).astype(BF16),
            yc_ref[0, rs, :],
        ], axis=-1)
        x1 = x_ref[0, rs, :] + mod[2:3] * jnp.dot(y, wout_ref[...], preferred_element_type=F32)
        h = (_rms(x1, n2w) + mod[3:4]).astype(BF16)
        return x1, h

    starts = list(range(0, tm, OUT_ROWS))
    mixed = {starts[0]: mix(starts[0])}
    for idx, r0 in enumerate(starts):
        if idx + 1 < len(starts):
            mixed[starts[idx + 1]] = mix(starts[idx + 1])
        x1, h = mixed.pop(r0)
        acc = None
        for lo in range(0, ffn, FFN_CHUNK):
            g = jnp.dot(h, wgu_ref[:, lo:lo + FFN_CHUNK], preferred_element_type=F32)
            u = jnp.dot(h, wgu_ref[:, ffn + lo:ffn + lo + FFN_CHUNK], preferred_element_type=F32)
            a = (g * jax.nn.sigmoid(g) * u).astype(BF16)
            part = jnp.dot(a, wdn_ref[lo:lo + FFN_CHUNK, :], preferred_element_type=F32)
            acc = part if acc is None else acc + part
        x2 = x1 + mod[5:6] * acc
        if final:
            x2 = _rms(x2, fw_ref[...])
        o_ref[0, r0:r0 + OUT_ROWS, :] = x2


def _out_ffn(x, ya, yb, yc, mod, gwa, gwb, wout, n2w, wgu, wdn, fw, tm, layer, final):
    b, s, d = x.shape
    tok = lambda w: pl.BlockSpec((1, tm, w), lambda i, j: (i, j, 0))
    full = lambda a: pl.BlockSpec(a.shape, lambda i, j: (0,) * a.ndim, pipeline_mode=pl.Buffered(1))
    stacked = lambda a: pl.BlockSpec((None,) + a.shape[1:], lambda i, j: (layer,) + (0,) * (a.ndim - 1),
                                     pipeline_mode=pl.Buffered(1))
    return pl.pallas_call(
        functools.partial(_out_kernel, final=final),
        grid=(b, s // tm),
        in_specs=[
            tok(d), tok(A_WIDTH), tok(B_WIDTH), tok(C_WIDTH),
            pl.BlockSpec((1, N_MOD, d), lambda i, j: (i, 0, 0)),
            full(gwa), full(gwb), stacked(wout), full(n2w), stacked(wgu), stacked(wdn), full(fw),
        ],
        out_specs=tok(d),
        out_shape=jax.ShapeDtypeStruct((b, s, d), F32),
        compiler_params=pltpu.CompilerParams(
            dimension_semantics=("arbitrary", "arbitrary"), vmem_limit_bytes=VMEM_LIMIT),
        name="out_ffn",
    )(x, ya, yb, yc, mod, gwa, gwb, wout, n2w, wgu, wdn, fw)


def _in_col_index():
    src_aq, src_ak, src_av = 0, A_WIDTH, A_WIDTH + A_KV_WIDTH
    src_bcq = src_av + A_KV_WIDTH
    src_bckv = src_bcq + B_Q_RANK
    src_bkr = src_bckv + B_KV_RANK
    src_cu = src_bkr + B_ROPE
    src_cv = src_cu + C_WIDTH
    zero = src_cv + C_WIDTH
    idx = []
    for hh in A_HEAD_ORDER:
        idx += list(range(src_aq + hh * HEAD_DIM, src_aq + (hh + 1) * HEAD_DIM))
    idx += list(range(src_ak, src_bkr))
    idx += list(range(src_cu, zero))
    idx += [zero] * B_NOPE + list(range(src_bkr, src_bkr + B_ROPE)) + [zero] * (LANES - B_NOPE - B_ROPE)
    assert len(idx) == IN_COLS_PAD
    return np.asarray(idx, np.int32)


def _take_static(a, idx, axis):
    n = a.shape[axis]
    idx = [int(t) for t in idx]
    parts, start = [], 0
    for pos in range(1, len(idx) + 1):
        if pos < len(idx):
            prev, cur = idx[pos - 1], idx[pos]
            same_run = (prev == n and cur == n) or (prev != n and cur != n and cur == prev + 1)
        else:
            same_run = False
        if not same_run:
            first, count = idx[start], pos - start
            if first == n:
                shape = a.shape[:axis] + (count,) + a.shape[axis + 1:]
                parts.append(jnp.zeros(shape, a.dtype))
            else:
                parts.append(lax.slice_in_dim(a, first, first + count, axis=axis))
            start = pos
    return jnp.concatenate(parts, axis=axis)


def _rope_expand_kernel(ac_ref, as_ref, bc_ref, bs_ref, ca_ref, sa_ref, cb_ref, sb_ref):
    half_a, half_b = HEAD_DIM // 2, B_ROPE // 2
    ra, rb = ac_ref.shape[1], bc_ref.shape[1]
    lane_a = _lane_iota((ra, LANES))
    lane_b = _lane_iota((rb, LANES))
    sign_a = jnp.where((lane_a % HEAD_DIM) < half_a, -1.0, 1.0)
    sign_b = jnp.where(lane_b < B_NOPE + half_b, -1.0, 1.0)
    ones_b = jnp.where(lane_b < B_NOPE, 1.0, 0.0)
    per_a, per_b = LANES // half_a, LANES // half_b
    for j in range(per_a):
        pick = (lane_a // half_a) == j
        for src, dst, sign in ((ac_ref, ca_ref, None), (as_ref, sa_ref, sign_a)):
            t = jnp.where(pick, src[0], 0.0)
            width = half_a
            while width < LANES:
                t = t + pltpu.roll(t, width, 1)
                width *= 2
            dst[0, pl.ds(j, ra, stride=per_a), :] = t if sign is None else t * sign
    for j in range(per_b):
        pick = (lane_b // half_b) == j
        for src, dst, is_cos in ((bc_ref, cb_ref, True), (bs_ref, sb_ref, False)):
            y = jnp.where(pick, src[0], 0.0)
            t = None
            for target in (B_NOPE, B_NOPE + half_b):
                shift = (target - j * half_b) % LANES
                piece = y if shift == 0 else pltpu.roll(y, shift, 1)
                t = piece if t is None else t + piece
            dst[0, pl.ds(j, rb, stride=per_b), :] = t + ones_b if is_cos else t * sign_b


def _rope_tables(positions):
    b, s = positions.shape
    pos = positions.astype(F32)[..., None]

    def packed(dim):
        half = dim // 2
        per_row = LANES // half
        inv = 1.0 / (ROPE_THETA ** (jnp.arange(0, dim, 2, dtype=F32) / dim))
        ang = (pos.reshape(b, s // per_row, per_row, 1) * inv).reshape(b, s // per_row, LANES)
        return jnp.cos(ang), jnp.sin(ang)

    ac, asn = packed(HEAD_DIM)
    bc, bsn = packed(B_ROPE)
    te = min(ROPE_TILE, s)
    per_a, per_b = LANES // (HEAD_DIM // 2), LANES // (B_ROPE // 2)
    tok = pl.BlockSpec((1, te, LANES), lambda i, j: (i, j, 0))
    return pl.pallas_call(
        _rope_expand_kernel,
        grid=(b, s // te),
        in_specs=[pl.BlockSpec((1, te // per_a, LANES), lambda i, j: (i, j, 0))] * 2
        + [pl.BlockSpec((1, te // per_b, LANES), lambda i, j: (i, j, 0))] * 2,
        out_specs=[tok] * 4,
        out_shape=[jax.ShapeDtypeStruct((b, s, LANES), F32)] * 4,
        compiler_params=pltpu.CompilerParams(
            dimension_semantics=("arbitrary", "arbitrary"), vmem_limit_bytes=VMEM_LIMIT),
        name="rope_tables",
    )(ac, asn, bc, bsn)


def kernel(x, c, positions, ada_w, ada_b, norm1_w, w_in, a_sinks, b_q_norm_w, b_w_uq, b_kv_norm_w, b_w_ukv,
           c_ln_w, c_ln_b, c_w_s, c_b_s, out_norm_w, w_out, norm2_w, w_gate_up, w_down, final_norm_w):
    depth = w_in.shape[0]
    b, s, d = x.shape
    ffn = w_down.shape[1]
    tm_in, tm_out, tq = min(IN_TILE, s), min(OUT_TILE, s), min(ATTN_TILE, s)

    mod = _modulation(c, ada_w, ada_b).reshape(depth, b, N_MOD, d)
    tabs = _rope_tables(positions)

    w_in_p = _take_static(w_in.astype(BF16), _in_col_index(), 2)

    w_uq = b_w_uq.reshape(depth, B_Q_RANK, B_HEADS, B_NOPE + B_ROPE)
    w_uq = jnp.pad(w_uq, ((0, 0), (0, 0), (0, 0), (0, B_QK_PAD - B_NOPE - B_ROPE)))
    w_uq = w_uq.reshape(depth, B_Q_RANK, B_HEADS * B_QK_PAD).astype(BF16)
    w_ukv = b_w_ukv.reshape(depth, B_KV_RANK, B_HEADS, B_NOPE + B_V)
    w_uk = jnp.pad(w_ukv[..., :B_NOPE], ((0, 0), (0, 0), (0, 0), (0, B_QK_PAD - B_NOPE)))
    w_ukv = jnp.concatenate([w_uk.reshape(depth, B_KV_RANK, B_HEADS * B_QK_PAD),
                             w_ukv[..., B_NOPE:].reshape(depth, B_KV_RANK, B_WIDTH)], axis=-1).astype(BF16)

    bs_full = jnp.repeat(jnp.swapaxes(c_b_s, 1, 2), C_GROUP_DIM, axis=-1)
    order = np.asarray(A_HEAD_ORDER)
    sink_tab = jnp.broadcast_to((a_sinks * LOG2E)[:, order, None], (depth, A_Q_HEADS, LANES))
    sink_tab = jnp.pad(sink_tab, ((0, 0), (0, 8 - A_Q_HEADS), (0, 0)))

    a_perm = (order[:, None] * HEAD_DIM + np.arange(HEAD_DIM)[None, :]).reshape(-1)
    out_perm = np.concatenate([a_perm, np.arange(A_WIDTH, d)])
    gw = _take_static(out_norm_w, out_perm, 1)
    w_out_p = _take_static(w_out.astype(BF16), out_perm, 1)

    assert ffn % FFN_CHUNK == 0
    w_gu = w_gate_up.astype(BF16)
    w_dn = w_down.astype(BF16)

    row = lambda a: a.reshape(1, -1)
    for l in range(depth):
        ya, qb, kb, vb, yc = _in_proj(
            x, mod[l], row(norm1_w[l]), w_in_p, w_uq, w_ukv, row(b_q_norm_w[l]), row(b_kv_norm_w[l]),
            tabs, row(c_ln_w[l]), row(c_ln_b[l]), c_w_s, bs_full, row(gw[l, A_WIDTH + B_WIDTH:]), sink_tab,
            tm_in, l)
        yb = _mla(qb, kb, vb, tq)
        x = _out_ffn(x, ya, yb, yc, mod[l], row(gw[l, :A_WIDTH]), row(gw[l, A_WIDTH:A_WIDTH + B_WIDTH]),
                     w_out_p, row(norm2_w[l]), w_gu, w_dn, row(final_norm_w), tm_out, l, l == depth - 1)
    return x
```
